```python
import math
import jax, jax.numpy as jnp
from jax import lax
import numpy as np

D_MODEL = 1024
BATCH = 8
SEQ = 2048
DEPTH = 2
DEC_BATCH = 128
DEC_SEQ = 4
PAST_LEN = 16384
PAGE_SIZE = 128

RET_HEADS = 4
RET_DK = D_MODEL // 8
RET_DV = D_MODEL // 4
RET_QK = RET_HEADS * RET_DK
RET_V = RET_HEADS * RET_DV
HG_HEADS = 8
HG_DK = D_MODEL // HG_HEADS
HG_DV = D_MODEL // HG_HEADS
HG_W = HG_HEADS * HG_DK
LRU_W = D_MODEL
LRU_BLOCKS = 8
LRU_BD = LRU_W // LRU_BLOCKS
CONV_W = 4
LRU_C = 8.0
N_BRANCH = 3
FFN_HIDDEN = 2816
CHUNK = 64
ROPE_BASE = 10000.0
EPS = 1e-6
IN_SPLITS = (RET_QK, RET_QK, RET_V, RET_V, HG_W, HG_W, HG_W, HG_W, LRU_W, LRU_W, N_BRANCH * D_MODEL)
IN_WIDTH = sum(IN_SPLITS)

kernel_name = "hybrid_retention_hgrn2_rglru_decode_step"


def _rmsnorm(x, g):
    xf = x.astype(jnp.float32)
    y = xf * lax.rsqrt(jnp.mean(xf * xf, axis=-1, keepdims=True) + EPS)
    return (y * g.astype(jnp.float32)).astype(x.dtype)


def _head_rms(x):
    return x * lax.rsqrt(jnp.mean(x * x, axis=-1, keepdims=True) + EPS)


def _swiglu(x, wg, wu, wd):
    return (jax.nn.silu(x @ wg) * (x @ wu)) @ wd


def _split(z):
    out = []
    o = 0
    for w in IN_SPLITS:
        out.append(z[..., o:o + w])
        o += w
    return out


def _rotary(x, pos):
    d = x.shape[-1]
    inv = ROPE_BASE ** (-jnp.arange(0, d, 2, dtype=jnp.float32) / d)
    ang = pos.astype(jnp.float32)[:, None] * inv[None, :]
    cos = jnp.cos(ang)[None, :, None, :]
    sin = jnp.sin(ang)[None, :, None, :]
    x1, x2 = x[..., : d // 2], x[..., d // 2:]
    return jnp.concatenate([x1 * cos - x2 * sin, x1 * sin + x2 * cos], axis=-1)


def _chunk_len(t):
    return CHUNK if t % CHUNK == 0 else t


def _to_chunks(x, c):
    b, t, h, d = x.shape
    return x.reshape(b, t // c, c, h, d).transpose(1, 0, 3, 2, 4)


def _from_chunks(y):
    n, b, h, c, d = y.shape
    return y.transpose(1, 0, 3, 2, 4).reshape(b, n * c, h, d)


def _retention(q, k, v, state):
    c = _chunk_len(q.shape[1])
    log_g = jnp.log1p(-jnp.exp2(-5.0 - jnp.arange(RET_HEADS, dtype=jnp.float32)))
    idx = jnp.arange(c, dtype=jnp.float32)
    diff = idx[:, None] - idx[None, :]
    causal = diff >= 0
    d_intra = jnp.where(causal, jnp.exp(log_g[:, None, None] * jnp.where(causal, diff, 0.0)), 0.0)
    q_dec = jnp.exp(log_g[:, None] * (idx + 1.0))[:, :, None]
    k_dec = jnp.exp(log_g[:, None] * (c - 1.0 - idx))[:, :, None]
    s_dec = jnp.exp(log_g * c)[:, None, None]

    def step(s, inp):
        qc, kc, vc = inp
        scores = jnp.einsum('bhtd,bhsd->bhts', qc, kc) * d_intra
        o = jnp.einsum('bhts,bhsv->bhtv', scores, vc) + jnp.einsum('bhtd,bhdv->bhtv', qc * q_dec, s)
        s = s * s_dec + jnp.einsum('bhsd,bhsv->bhdv', kc * k_dec, vc)
        return s, o

    s_fin, o = lax.scan(step, state, (_to_chunks(q, c), _to_chunks(k, c), _to_chunks(v, c)))
    return _from_chunks(o), s_fin


def _hgrn2(q, log_f, k, v, state):
    c = _chunk_len(q.shape[1])
    tri = jnp.tril(jnp.ones((c, c), dtype=bool))[:, :, None]

    def step(s, inp):
        qc, gc, kc, vc = inp
        b = jnp.cumsum(gc, axis=2)
        rel = jnp.where(tri, b[:, :, :, None, :] - b[:, :, None, :, :], -jnp.inf)
        a = jnp.einsum('bhtd,bhsd,bhtsd->bhts', qc, kc, jnp.exp(rel))
        o = jnp.einsum('bhts,bhsv->bhtv', a, vc) + jnp.einsum('bhtd,bhdv->bhtv', qc * jnp.exp(b), s)
        bl = b[:, :, -1:, :]
        s = s * jnp.exp(bl[:, :, 0, :, None]) + jnp.einsum('bhsd,bhsv->bhdv', kc * jnp.exp(bl - b), vc)
        return s, o

    s_fin, o = lax.scan(step, state, (_to_chunks(q, c), _to_chunks(log_f, c), _to_chunks(k, c), _to_chunks(v, c)))
    return _from_chunks(o), s_fin


def _causal_conv(x, buf, w, b):
    t = x.shape[1]
    xx = jnp.concatenate([buf, x], axis=1)
    y = b + sum(xx[:, i:i + t] * w[i] for i in range(CONV_W))
    return y, xx[:, xx.shape[1] - (CONV_W - 1):]


def _block_diag(x, w, b):
    bsz, t, _ = x.shape
    y = jnp.einsum('btnd,nde->btne', x.reshape(bsz, t, LRU_BLOCKS, LRU_BD), w)
    return y.reshape(bsz, t, LRU_W) + b


def _lin_combine(c1, c2):
    a1, b1 = c1
    a2, b2 = c2
    return a1 * a2, a2 * b1 + b2


def _rg_lru(x, h0, w_a, b_a, w_x, b_x, lam):
    r = jax.nn.sigmoid(_block_diag(x, w_a, b_a))
    i = jax.nn.sigmoid(_block_diag(x, w_x, b_x))
    log_a = -LRU_C * r * jax.nn.softplus(-lam)
    a = jnp.exp(log_a)
    u = jnp.sqrt(-jnp.expm1(2.0 * log_a)) * (i * x)
    u = u.at[:, 0].add(a[:, 0] * h0)
    _, hs = lax.associative_scan(_lin_combine, (a, u), axis=1)
    return hs, hs[:, -1]


def _mixer(h, pos, l, ret_s, hg_s, lru_h, conv_buf, p):
    f32 = jnp.float32
    bsz, t, _ = h.shape
    z = (h @ p['w_in'][l]).astype(f32)
    q_r, k_r, v_r, g_r, q_h, f_h, i_h, g_h, x_l, g_l, gate = _split(z)
    q = _rotary(q_r.reshape(bsz, t, RET_HEADS, RET_DK), pos)
    k = _rotary(k_r.reshape(bsz, t, RET_HEADS, RET_DK), pos) * (RET_DK ** -0.5)
    v = v_r.reshape(bsz, t, RET_HEADS, RET_DV)
    o_r, ret_new = _retention(q, k, v, ret_s.astype(f32))
    o_r = _head_rms(o_r).reshape(bsz, t, RET_V) * jax.nn.silu(g_r)
    br_r = (o_r.astype(h.dtype) @ p['w_ret_o'][l]).astype(f32)
    lbs = jnp.cumsum(jax.nn.softmax(p['hgrn_lb_logits'].astype(f32), axis=0), axis=0)
    lb = lbs[l] - lbs[0]
    log_f = jnp.logaddexp(jnp.log(lb), jnp.log1p(-lb) + jax.nn.log_sigmoid(f_h))
    k_h = -jnp.expm1(log_f)
    o_h, hg_new = _hgrn2(jax.nn.silu(q_h).reshape(bsz, t, HG_HEADS, HG_DK),
                         log_f.reshape(bsz, t, HG_HEADS, HG_DK),
                         k_h.reshape(bsz, t, HG_HEADS, HG_DK),
                         i_h.reshape(bsz, t, HG_HEADS, HG_DV), hg_s.astype(f32))
    o_h = _head_rms(o_h) * p['hgrn_norm'][l].astype(f32).reshape(HG_HEADS, HG_DV)
    o_h = o_h.reshape(bsz, t, HG_W) * jax.nn.silu(g_h)
    br_h = (o_h.astype(h.dtype) @ p['w_hgrn_o'][l]).astype(f32)
    xc, conv_new = _causal_conv(x_l, conv_buf.astype(f32), p['conv_w'][l].astype(f32), p['conv_b'][l].astype(f32))
    hs, h_new = _rg_lru(xc, lru_h.astype(f32), p['lru_w_a'][l].astype(f32), p['lru_b_a'][l].astype(f32),
                        p['lru_w_x'][l].astype(f32), p['lru_b_x'][l].astype(f32), p['lru_lambda'][l].astype(f32))
    br_l = ((hs * jax.nn.gelu(g_l)).astype(h.dtype) @ p['w_lru_o'][l]).astype(f32)
    gs = jax.nn.sigmoid(gate.reshape(bsz, t, N_BRANCH, D_MODEL) + p['merge_bias'][l].astype(f32))
    merged = gs[:, :, 0] * br_r + gs[:, :, 1] * br_h + gs[:, :, 2] * br_l
    out = merged.astype(h.dtype) @ p['w_mix_out'][l]
    return out, (ret_new, hg_new, h_new, conv_new)


def _trunk(x, pos, ret_s, hg_s, lru_h, conv_buf, p):
    news = ([], [], [], [])
    for l in range(DEPTH):
        x = x + 0.5 * _swiglu(_rmsnorm(x, p['ffn1_norm'][l]), p['ffn1_w_gate'][l], p['ffn1_w_up'][l], p['ffn1_w_down'][l])
        m, st = _mixer(_rmsnorm(x, p['mix_norm'][l]), pos, l, ret_s[l], hg_s[l], lru_h[l], conv_buf[l], p)
        x = x + m.astype(x.dtype)
        x = x + 0.5 * _swiglu(_rmsnorm(x, p['ffn2_norm'][l]), p['ffn2_w_gate'][l], p['ffn2_w_up'][l], p['ffn2_w_down'][l])
        for acc, s in zip(news, st):
            acc.append(s)
    y = _rmsnorm(x, p['final_norm'])
    return y, [jnp.stack(a, axis=0) for a in news]


def setup_inputs(seed: int = 0) -> dict:
    key = jax.random.key(seed)
    ks = iter(jax.random.split(key, 48))
    nrm = lambda shape, s: jax.random.normal(next(ks), shape, jnp.float32) * s
    gain = lambda shape: 1.0 + 0.01 * jax.random.normal(next(ks), shape, jnp.float32)
    d, f = D_MODEL, FFN_HIDDEN
    a0 = jax.random.uniform(next(ks), (DEPTH, LRU_W), jnp.float32, 0.9, 0.999)
    inp = {
        'x_prompt': nrm((BATCH, SEQ, d), 1.0),
        'x_sample': nrm((DEC_BATCH, DEC_SEQ, d), 1.0),
        'state_ret': nrm((DEPTH, DEC_BATCH, RET_HEADS, RET_DK, RET_DV), 1.0),
        'state_hgrn': nrm((DEPTH, DEC_BATCH, HG_HEADS, HG_DK, HG_DV), 1.0),
        'state_lru': nrm((DEPTH, DEC_BATCH, LRU_W), 0.5),
        'state_conv': nrm((DEPTH, DEC_BATCH, CONV_W - 1, LRU_W), 1.0),
        'ffn1_norm': gain((DEPTH, d)),
        'ffn1_w_gate': nrm((DEPTH, d, f), d ** -0.5),
        'ffn1_w_up': nrm((DEPTH, d, f), d ** -0.5),
        'ffn1_w_down': nrm((DEPTH, f, d), f ** -0.5),
        'mix_norm': gain((DEPTH, d)),
        'w_in': nrm((DEPTH, d, IN_WIDTH), d ** -0.5),
        'merge_bias': nrm((DEPTH, N_BRANCH, d), 0.01),
        'w_ret_o': nrm((DEPTH, RET_V, d), RET_V ** -0.5),
        'hgrn_lb_logits': nrm((DEPTH, HG_W), 0.5),
        'hgrn_norm': gain((DEPTH, HG_W)),
        'w_hgrn_o': nrm((DEPTH, HG_W, d), HG_W ** -0.5),
        'conv_w': nrm((DEPTH, CONV_W, LRU_W), CONV_W ** -0.5),
        'conv_b': nrm((DEPTH, LRU_W), 0.01),
        'lru_w_a': nrm((DEPTH, LRU_BLOCKS, LRU_BD, LRU_BD), LRU_BD ** -0.5),
        'lru_b_a': nrm((DEPTH, LRU_W), 0.01),
        'lru_w_x': nrm((DEPTH, LRU_BLOCKS, LRU_BD, LRU_BD), LRU_BD ** -0.5),
        'lru_b_x': nrm((DEPTH, LRU_W), 0.01),
        'lru_lambda': jnp.log(a0) - jnp.log1p(-a0),
        'w_lru_o': nrm((DEPTH, LRU_W, d), LRU_W ** -0.5),
        'w_mix_out': nrm((DEPTH, d, d), d ** -0.5),
        'ffn2_norm': gain((DEPTH, d)),
        'ffn2_w_gate': nrm((DEPTH, d, f), d ** -0.5),
        'ffn2_w_up': nrm((DEPTH, d, f), d ** -0.5),
        'ffn2_w_down': nrm((DEPTH, f, d), f ** -0.5),
        'final_norm': gain((d,)),
    }
    return inp


def reference(x_prompt, x_sample, state_ret, state_hgrn, state_lru, state_conv,
              ffn1_norm, ffn1_w_gate, ffn1_w_up, ffn1_w_down, mix_norm, w_in, merge_bias,
              w_ret_o, hgrn_lb_logits, hgrn_norm, w_hgrn_o, conv_w, conv_b,
              lru_w_a, lru_b_a, lru_w_x, lru_b_x, lru_lambda, w_lru_o, w_mix_out,
              ffn2_norm, ffn2_w_gate, ffn2_w_up, ffn2_w_down, final_norm):
    p = dict(ffn1_norm=ffn1_norm, ffn1_w_gate=ffn1_w_gate, ffn1_w_up=ffn1_w_up, ffn1_w_down=ffn1_w_down,
             mix_norm=mix_norm, w_in=w_in, merge_bias=merge_bias, w_ret_o=w_ret_o,
             hgrn_lb_logits=hgrn_lb_logits, hgrn_norm=hgrn_norm, w_hgrn_o=w_hgrn_o,
             conv_w=conv_w, conv_b=conv_b, lru_w_a=lru_w_a, lru_b_a=lru_b_a,
             lru_w_x=lru_w_x, lru_b_x=lru_b_x, lru_lambda=lru_lambda, w_lru_o=w_lru_o,
             w_mix_out=w_mix_out, ffn2_norm=ffn2_norm, ffn2_w_gate=ffn2_w_gate,
             ffn2_w_up=ffn2_w_up, ffn2_w_down=ffn2_w_down, final_norm=final_norm)
    f32 = jnp.float32
    bp = x_prompt.shape[0]
    pos_p = jnp.arange(x_prompt.shape[1])
    z_ret = jnp.zeros((DEPTH, bp, RET_HEADS, RET_DK, RET_DV), f32)
    z_hg = jnp.zeros((DEPTH, bp, HG_HEADS, HG_DK, HG_DV), f32)
    z_h = jnp.zeros((DEPTH, bp, LRU_W), f32)
    z_cv = jnp.zeros((DEPTH, bp, CONV_W - 1, LRU_W), f32)
    y_prompt, st_p = _trunk(x_prompt, pos_p, z_ret, z_hg, z_h, z_cv, p)
    pos_s = PAST_LEN + jnp.arange(x_sample.shape[1])
    y_sample, st_s = _trunk(x_sample, pos_s, state_ret, state_hgrn, state_lru, state_conv, p)
    dt = x_prompt.dtype
    return (y_prompt, y_sample,
            st_p[0].astype(dt), st_s[0].astype(state_ret.dtype),
            st_p[1].astype(dt), st_s[1].astype(state_hgrn.dtype),
            st_p[2].astype(dt), st_s[2].astype(state_lru.dtype),
            st_p[3].astype(dt), st_s[3].astype(state_conv.dtype))
```

```python
import functools

import jax
import jax.numpy as jnp
import numpy as np
from jax import lax
from jax.experimental import pallas as pl
from jax.experimental.pallas import tpu as pltpu

f32 = jnp.float32
bf16 = jnp.bfloat16

D_MODEL = 1024
FFN_HIDDEN = 2816
RET_HEADS, RET_DK, RET_DV = 4, 128, 256
HG_HEADS, HG_DK, HG_DV = 8, 128, 128
LRU_W, LRU_BLOCKS, LRU_BD = 1024, 8, 128
CONV_W = 4
LRU_C = 8.0
N_BRANCH = 3
ROPE_BASE = 10000.0
PAST_LEN = 16384
EPS = 1e-6
RET_COLS = 2 * RET_HEADS * RET_DK + 2 * RET_HEADS * RET_DV
HG_COLS = 4 * HG_HEADS * HG_DK
LRU_COLS = 2 * LRU_W + N_BRANCH * D_MODEL

V7X_VMEM_BYTES = 64 * 1024 * 1024
VMEM_LIMIT = 56 * 1024 * 1024
SUBLANES = 8

TOKEN_TILE = 512
FFN_CHUNK = 256
RET_CHUNK = 256
HG_CHUNK = 128
HG_LEVELS = ((128, 64), (64, 32), (32, 16), (16, 8))
LRU_TILE = 256
SAMPLE_BB = 8


def _cparams(sem):
    return pltpu.CompilerParams(dimension_semantics=sem, vmem_limit_bytes=VMEM_LIMIT)


def _resident(shape):
    nd = len(shape)
    return pl.BlockSpec(shape, lambda *_: (0,) * nd, pipeline_mode=pl.Buffered(1))


def _rmsnorm(x, g):
    return x * lax.rsqrt(jnp.mean(x * x, axis=-1, keepdims=True) + EPS) * g


def _dot(a, b):
    return jnp.dot(a, b, preferred_element_type=f32)


def _dot_tb(a, b):
    return lax.dot_general(a, b, (((1,), (1,)), ((), ())), preferred_element_type=f32)


def _dot_ta(a, b):
    return lax.dot_general(a, b, (((0,), (0,)), ((), ())), preferred_element_type=f32)


def _silu(x):
    return x * jax.nn.sigmoid(x)


def _log_sigmoid(x):
    return jnp.minimum(x, 0.0) - jnp.log1p(jnp.exp(-jnp.abs(x)))


def _softplus(x):
    return jnp.maximum(x, 0.0) + jnp.log1p(jnp.exp(-jnp.abs(x)))


def _expm1(x):
    u = jnp.exp(x)
    return jnp.where(u == 1.0, x, jnp.where(x < -1.0, u - 1.0, (u - 1.0) * x / jnp.log(u)))


def _ffn_body(x_ref, g_ref, wg_ref, wu_ref, wd_ref, fg_ref, o_ref, *, final):
    x = x_ref[...]
    xn = _rmsnorm(x, g_ref[...]).astype(bf16)
    acc = jnp.zeros(x.shape, f32)
    for c in range(FFN_HIDDEN // FFN_CHUNK):
        sl = slice(c * FFN_CHUNK, (c + 1) * FFN_CHUNK)
        g = _dot(xn, wg_ref[:, sl])
        u = _dot(xn, wu_ref[:, sl])
        h = (_silu(g) * u).astype(bf16)
        acc = acc + _dot(h, wd_ref[sl, :])
    y = x + 0.5 * acc
    if final:
        y = _rmsnorm(y, fg_ref[...])
    o_ref[...] = y


def _ffn(x, g, wg, wu, wd, final_g=None):
    n = x.shape[0]
    tm = min(TOKEN_TILE, n)
    final = final_g is not None
    fg = final_g if final else g
    row = pl.BlockSpec((tm, D_MODEL), lambda i: (i, 0))
    return pl.pallas_call(
        functools.partial(_ffn_body, final=final),
        grid=(n // tm,),
        in_specs=[row, _resident((1, D_MODEL)), _resident(wg.shape), _resident(wu.shape),
                  _resident(wd.shape), _resident((1, D_MODEL))],
        out_specs=row,
        out_shape=jax.ShapeDtypeStruct((n, D_MODEL), f32),
        compiler_params=_cparams(("parallel",)),
        name="ffn",
    )(x, g.reshape(1, -1), wg, wu, wd, fg.reshape(1, -1))


def _proj_ret_body(x_ref, g_ref, w_ref, cos_ref, sin_ref, q_ref, k_ref, v_ref, gr_ref):
    xn = _rmsnorm(x_ref[...], g_ref[...]).astype(bf16)
    cos = cos_ref[...]
    sin = sin_ref[...]
    nqk = RET_HEADS * RET_DK

    def rot(z):
        return z * cos + pltpu.roll(z, RET_DK // 2, 1) * sin

    for c in range(nqk // 256):
        zq = _dot(xn, w_ref[:, c * 256:(c + 1) * 256])
        zk = _dot(xn, w_ref[:, nqk + c * 256:nqk + (c + 1) * 256])
        for j in range(2):
            sl = slice(c * 256 + j * 128, c * 256 + (j + 1) * 128)
            q_ref[:, sl] = rot(zq[:, j * 128:(j + 1) * 128]).astype(bf16)
            k_ref[:, sl] = (rot(zk[:, j * 128:(j + 1) * 128]) * (RET_DK ** -0.5)).astype(bf16)
    nv = RET_HEADS * RET_DV
    for c in range(nv // 256):
        sl = slice(c * 256, (c + 1) * 256)
        v_ref[:, sl] = _dot(xn, w_ref[:, 2 * nqk + c * 256:2 * nqk + (c + 1) * 256]).astype(bf16)
        zg = _dot(xn, w_ref[:, 2 * nqk + nv + c * 256:2 * nqk + nv + (c + 1) * 256])
        gr_ref[:, sl] = _silu(zg).astype(bf16)


def _proj_ret(x, g, w, cos, sin):
    n = x.shape[0]
    tm = min(TOKEN_TILE, n)
    row = lambda w_: pl.BlockSpec((tm, w_), lambda i: (i, 0))
    nqk, nv = RET_HEADS * RET_DK, RET_HEADS * RET_DV
    return pl.pallas_call(
        _proj_ret_body,
        grid=(n // tm,),
        in_specs=[row(D_MODEL), _resident((1, D_MODEL)), _resident(w.shape), row(RET_DK), row(RET_DK)],
        out_specs=[row(nqk), row(nqk), row(nv), row(nv)],
        out_shape=[jax.ShapeDtypeStruct((n, nqk), bf16), jax.ShapeDtypeStruct((n, nqk), bf16),
                   jax.ShapeDtypeStruct((n, nv), bf16), jax.ShapeDtypeStruct((n, nv), bf16)],
        compiler_params=_cparams(("parallel",)),
        name="proj_ret",
    )(x, g.reshape(1, -1), w, cos, sin)


def _proj_hgrn_body(x_ref, g_ref, w_ref, lbl_ref, q_ref, lf_ref, i_ref, gh_ref, *, layer):
    xn = _rmsnorm(x_ref[...], g_ref[...]).astype(bf16)
    hw = HG_HEADS * HG_DK
    if layer > 0:
        lg = lbl_ref[...]
        e = jnp.exp(lg - jnp.max(lg, axis=0, keepdims=True))
        sm = e / jnp.sum(e, axis=0, keepdims=True)
        lb = jnp.sum(sm[1:layer + 1], axis=0, keepdims=True)
        log_lb = jnp.log(lb)
        log_1mlb = jnp.log1p(-lb)
    for c in range(hw // 256):
        sl = slice(c * 256, (c + 1) * 256)
        q_ref[:, sl] = _silu(_dot(xn, w_ref[:, sl])).astype(bf16)
        zf = _dot(xn, w_ref[:, hw + c * 256:hw + (c + 1) * 256])
        ls = _log_sigmoid(zf)
        if layer > 0:
            a = log_lb[:, sl]
            b = log_1mlb[:, sl] + ls
            ls = jnp.maximum(a, b) + jnp.log1p(jnp.exp(-jnp.abs(a - b)))
        lf_ref[:, sl] = ls
        i_ref[:, sl] = _dot(xn, w_ref[:, 2 * hw + c * 256:2 * hw + (c + 1) * 256]).astype(bf16)
        gh_ref[:, sl] = _silu(_dot(xn, w_ref[:, 3 * hw + c * 256:3 * hw + (c + 1) * 256])).astype(bf16)


def _proj_hgrn(x, g, w, lb_logits, layer):
    n = x.shape[0]
    tm = min(TOKEN_TILE, n)
    hw = HG_HEADS * HG_DK
    row = lambda w_: pl.BlockSpec((tm, w_), lambda i: (i, 0))
    return pl.pallas_call(
        functools.partial(_proj_hgrn_body, layer=layer),
        grid=(n // tm,),
        in_specs=[row(D_MODEL), _resident((1, D_MODEL)), _resident(w.shape), _resident(lb_logits.shape)],
        out_specs=[row(hw), row(hw), row(hw), row(hw)],
        out_shape=[jax.ShapeDtypeStruct((n, hw), bf16), jax.ShapeDtypeStruct((n, hw), f32),
                   jax.ShapeDtypeStruct((n, hw), bf16), jax.ShapeDtypeStruct((n, hw), bf16)],
        compiler_params=_cparams(("parallel",)),
        name="proj_hgrn",
    )(x, g.reshape(1, -1), w, lb_logits)


def _proj_lru_body(x_ref, g_ref, w_ref, mb_ref, xl_ref, gl_ref, gs_ref):
    xn = _rmsnorm(x_ref[...], g_ref[...]).astype(bf16)
    for c in range(LRU_W // 256):
        sl = slice(c * 256, (c + 1) * 256)
        xl_ref[:, sl] = _dot(xn, w_ref[:, sl])
        zg = _dot(xn, w_ref[:, LRU_W + c * 256:LRU_W + (c + 1) * 256])
        gl_ref[:, sl] = jax.nn.gelu(zg, approximate=True).astype(bf16)
    for c in range(N_BRANCH * D_MODEL // 256):
        sl = slice(c * 256, (c + 1) * 256)
        z = _dot(xn, w_ref[:, 2 * LRU_W + c * 256:2 * LRU_W + (c + 1) * 256])
        gs_ref[:, sl] = jax.nn.sigmoid(z + mb_ref[:, sl]).astype(bf16)


def _proj_lru(x, g, w, merge_bias):
    n = x.shape[0]
    tm = min(TOKEN_TILE, n)
    row = lambda w_: pl.BlockSpec((tm, w_), lambda i: (i, 0))
    ng = N_BRANCH * D_MODEL
    return pl.pallas_call(
        _proj_lru_body,
        grid=(n // tm,),
        in_specs=[row(D_MODEL), _resident((1, D_MODEL)), _resident(w.shape), _resident((1, ng))],
        out_specs=[row(LRU_W), row(LRU_W), row(ng)],
        out_shape=[jax.ShapeDtypeStruct((n, LRU_W), f32), jax.ShapeDtypeStruct((n, LRU_W), bf16),
                   jax.ShapeDtypeStruct((n, ng), bf16)],
        compiler_params=_cparams(("parallel",)),
        name="proj_lru",
    )(x, g.reshape(1, -1), w, merge_bias.reshape(1, -1))


def _mix_out_body(x_ref, or_ref, oh_ref, ol_ref, gs_ref, wr_ref, wh_ref, wl_ref, wo_ref, o_ref):
    d = D_MODEL
    merged = gs_ref[:, 0:d].astype(f32) * _dot(or_ref[...], wr_ref[...])
    merged = merged + gs_ref[:, d:2 * d].astype(f32) * _dot(oh_ref[...], wh_ref[...])
    merged = merged + gs_ref[:, 2 * d:3 * d].astype(f32) * _dot(ol_ref[...], wl_ref[...])
    o_ref[...] = x_ref[...] + _dot(merged.astype(bf16), wo_ref[...])


def _mix_out(x, o_r, o_h, o_l, gs, wr, wh, wl, wo):
    n = x.shape[0]
    tm = min(TOKEN_TILE, n)
    row = lambda w_: pl.BlockSpec((tm, w_), lambda i: (i, 0))
    return pl.pallas_call(
        _mix_out_body,
        grid=(n // tm,),
        in_specs=[row(D_MODEL), row(D_MODEL), row(D_MODEL), row(D_MODEL), row(N_BRANCH * D_MODEL),
                  _resident(wr.shape), _resident(wh.shape), _resident(wl.shape), _resident(wo.shape)],
        out_specs=row(D_MODEL),
        out_shape=jax.ShapeDtypeStruct((n, D_MODEL), f32),
        compiler_params=_cparams(("parallel",)),
        name="mix_out",
    )(x, o_r, o_h, o_l, gs, wr, wh, wl, wo)


def _ret_log_gamma():
    return jnp.log1p(-jnp.exp2(-5.0 - jnp.arange(RET_HEADS, dtype=f32)))


def _ret_tables(c):
    log_g = _ret_log_gamma()
    idx = jnp.arange(c, dtype=f32)
    diff = idx[:, None] - idx[None, :]
    causal = diff >= 0
    d_intra = jnp.where(causal, jnp.exp(log_g[:, None, None] * jnp.where(causal, diff, 0.0)), 0.0)
    q_dec = jnp.exp(log_g[:, None] * (idx + 1.0))[:, :, None]
    k_dec = jnp.exp(log_g[:, None] * (c - 1.0 - idx))[:, :, None]
    s_dec = jnp.exp(log_g * c)
    return d_intra, q_dec, k_dec, s_dec


def _head_rms_gate(o, gate):
    return o * lax.rsqrt(jnp.mean(o * o, axis=-1, keepdims=True) + EPS) * gate


def _ret_prompt_body(q_ref, k_ref, v_ref, g_ref, dm_ref, qd_ref, kd_ref, sd_ref, o_ref, st_ref, s_scr):
    c = pl.program_id(1)

    @pl.when(c == 0)
    def _():
        s_scr[...] = jnp.zeros(s_scr.shape, f32)

    for h in range(RET_HEADS):
        ks = slice(h * RET_DK, (h + 1) * RET_DK)
        vs = slice(h * RET_DV, (h + 1) * RET_DV)
        q = q_ref[:, ks]
        k = k_ref[:, ks]
        v = v_ref[:, vs]
        state = s_scr[h]
        scores = (_dot_tb(q, k) * dm_ref[h]).astype(bf16)
        o = _dot(scores, v) + qd_ref[h] * _dot(q, state.astype(bf16))
        kd = (k.astype(f32) * kd_ref[h]).astype(bf16)
        s_scr[h] = state * sd_ref[h] + _dot_ta(kd, v)
        o_ref[:, vs] = _head_rms_gate(o, g_ref[:, vs].astype(f32)).astype(bf16)

    @pl.when(c == pl.num_programs(1) - 1)
    def _():
        st_ref[0] = s_scr[...]


def _ret_prompt(q, k, v, g, bsz, t):
    c = RET_CHUNK
    nc = t // c
    nqk, nv = RET_HEADS * RET_DK, RET_HEADS * RET_DV
    d_intra, q_dec, k_dec, s_dec = _ret_tables(c)
    sd = jnp.broadcast_to(s_dec[:, None, None], (RET_HEADS, 1, RET_DV))
    row = lambda w_: pl.BlockSpec((c, w_), lambda b, i: (b * nc + i, 0))
    return pl.pallas_call(
        _ret_prompt_body,
        grid=(bsz, nc),
        in_specs=[row(nqk), row(nqk), row(nv), row(nv), _resident(d_intra.shape), _resident(q_dec.shape),
                  _resident(k_dec.shape), _resident(sd.shape)],
        out_specs=[row(nv), pl.BlockSpec((1, RET_HEADS, RET_DK, RET_DV), lambda b, i: (b, 0, 0, 0))],
        out_shape=[jax.ShapeDtypeStruct((bsz * t, nv), bf16),
                   jax.ShapeDtypeStruct((bsz, RET_HEADS, RET_DK, RET_DV), f32)],
        scratch_shapes=[pltpu.VMEM((RET_HEADS, RET_DK, RET_DV), f32)],
        compiler_params=_cparams(("parallel", "arbitrary")),
        name="ret_prompt",
    )(q, k, v, g, d_intra, q_dec, k_dec, sd)


def _pairwise8(q, k, v, weight_fn, nblk):
    dk, dv = q.shape[-1], v.shape[-1]
    q3 = q.reshape(nblk, SUBLANES, dk)
    k3 = k.reshape(nblk, SUBLANES, dk)
    v3 = v.reshape(nblk, SUBLANES, dv)
    out = jnp.zeros((nblk, SUBLANES, dv), f32)
    for s in range(SUBLANES):
        ew, col = weight_fn(s)
        p = q3 * k3[:, s:s + 1, :]
        if ew is not None:
            p = p * ew
        a = jnp.sum(p, axis=-1, keepdims=True) * col
        out = out + a * v3[:, s:s + 1, :]
    return out.reshape(nblk * SUBLANES, dv)


def _hg_level_masks(c):
    t = np.arange(c)[:, None]
    s = np.arange(c)[None, :]
    ms = [((t // blk == s // blk) & (t % blk >= half) & (s % blk < half)) for blk, half in HG_LEVELS]
    return jnp.asarray(np.stack(ms).astype(np.float32))


def _hgrn_prompt_body(q_ref, lf_ref, i_ref, g_ref, nw_ref, tri_ref, mask_ref, o_ref, st_ref, s_scr):
    ci = pl.program_id(1)
    c = HG_CHUNK
    nblk = c // SUBLANES

    @pl.when(ci == 0)
    def _():
        s_scr[...] = jnp.zeros(s_scr.shape, f32)

    lf_all = lf_ref[...]
    hi = lf_all.astype(bf16)
    lo = (lf_all - hi.astype(f32)).astype(bf16)
    tri = tri_ref[...]
    b_all = _dot(tri, hi) + _dot(tri, lo)
    row8 = lax.broadcasted_iota(jnp.int32, (SUBLANES, 1), 0)

    for h in range(HG_HEADS):
        sl = slice(h * HG_DK, (h + 1) * HG_DK)
        b = b_all[:, sl]
        q = q_ref[:, sl].astype(f32)
        kk = 1.0 - jnp.exp(lf_all[:, sl])
        v16 = i_ref[:, sl]
        a_mat = jnp.zeros((c, c), f32)
        for lev, (blk, half) in enumerate(HG_LEVELS):
            bmid = jnp.concatenate(
                [jnp.broadcast_to(b[j * blk + half - 1:j * blk + half, :], (blk, HG_DK)) for j in range(c // blk)],
                axis=0)
            e = jnp.exp(-jnp.abs(b - bmid))
            a_mat = a_mat + _dot_tb((q * e).astype(bf16), (kk * e).astype(bf16)) * mask_ref[lev]
        o = _dot(a_mat.astype(bf16), v16)
        b3 = b.reshape(nblk, SUBLANES, HG_DK)

        def weight(s, b3=b3):
            return jnp.exp(jnp.minimum(b3 - b3[:, s:s + 1, :], 0.0)), (row8 >= s).astype(f32)

        o = o + _pairwise8(q, kk, v16.astype(f32), weight, nblk)
        st = s_scr[h]
        o = o + _dot_tb((q * jnp.exp(b)).astype(bf16), st.astype(bf16))
        bl = b[c - 1:c, :]
        kdec = (kk * jnp.exp(bl - b)).astype(bf16)
        s_scr[h] = st * jnp.exp(bl) + _dot_ta(v16, kdec)
        o_ref[:, sl] = _head_rms_gate(o, nw_ref[:, sl] * g_ref[:, sl].astype(f32)).astype(bf16)

    @pl.when(ci == pl.num_programs(1) - 1)
    def _():
        for h in range(HG_HEADS):
            st_ref[0, h] = s_scr[h].T


def _hgrn_prompt(q, lf, i, g, norm_w, bsz, t):
    c = HG_CHUNK
    nc = t // c
    hw = HG_HEADS * HG_DK
    tri = jnp.asarray(np.tril(np.ones((c, c), np.float32))).astype(bf16)
    masks = _hg_level_masks(c)
    row = lambda: pl.BlockSpec((c, hw), lambda b, j: (b * nc + j, 0))
    return pl.pallas_call(
        _hgrn_prompt_body,
        grid=(bsz, nc),
        in_specs=[row(), row(), row(), row(), _resident((1, hw)), _resident(tri.shape), _resident(masks.shape)],
        out_specs=[row(), pl.BlockSpec((1, HG_HEADS, HG_DK, HG_DV), lambda b, j: (b, 0, 0, 0))],
        out_shape=[jax.ShapeDtypeStruct((bsz * t, hw), bf16),
                   jax.ShapeDtypeStruct((bsz, HG_HEADS, HG_DK, HG_DV), f32)],
        scratch_shapes=[pltpu.VMEM((HG_HEADS, HG_DV, HG_DK), f32)],
        compiler_params=_cparams(("parallel", "arbitrary")),
        name="hgrn_prompt",
    )(q, lf, i, g, norm_w.reshape(1, -1), tri, masks)


def _lru_gates(xc, wa_ref, ba, wx_ref, bx, neg_c_sp):
    xb = xc.astype(bf16)
    a_parts, u_parts = [], []
    for n in range(LRU_BLOCKS):
        sl = slice(n * LRU_BD, (n + 1) * LRU_BD)
        r = jax.nn.sigmoid(_dot(xb[:, sl], wa_ref[n]) + ba[:, sl])
        gi = jax.nn.sigmoid(_dot(xb[:, sl], wx_ref[n]) + bx[:, sl])
        log_a = neg_c_sp[:, sl] * r
        a_parts.append(jnp.exp(log_a))
        u_parts.append(jnp.sqrt(-_expm1(2.0 * log_a)) * (gi * xc[:, sl]))
    return jnp.concatenate(a_parts, axis=1), jnp.concatenate(u_parts, axis=1)


def _lru_prompt_body(x_ref, gl_ref, cw_ref, cb_ref, wa_ref, ba_ref, wx_ref, bx_ref, lam_ref,
                     o_ref, h_ref, cv_ref, xx_scr, h_scr):
    ti = pl.program_id(1)
    tl = LRU_TILE

    @pl.when(ti == 0)
    def _():
        xx_scr[0:SUBLANES, :] = jnp.zeros((SUBLANES, LRU_W), f32)
        h_scr[...] = jnp.zeros(h_scr.shape, f32)

    x = x_ref[...]
    xx_scr[SUBLANES:SUBLANES + tl, :] = x
    cw = cw_ref[...]
    xc = cb_ref[...] + x * cw[CONV_W - 1:CONV_W, :]
    for j in range(CONV_W - 1):
        off = SUBLANES - (CONV_W - 1) + j
        xc = xc + xx_scr[off:off + tl, :] * cw[j:j + 1, :]
    neg_c_sp = -LRU_C * _softplus(-lam_ref[...])
    a, u = _lru_gates(xc, wa_ref, ba_ref[...], wx_ref, bx_ref[...], neg_c_sp)
    row = lax.broadcasted_iota(jnp.int32, (tl, 1), 0)
    u = u + jnp.where(row == 0, a * h_scr[0:1, :], 0.0)
    step = 1
    while step < tl:
        keep = row >= step
        u = a * jnp.where(keep, pltpu.roll(u, step, 0), 0.0) + u
        a = a * jnp.where(keep, pltpu.roll(a, step, 0), 1.0)
        step *= 2
    o_ref[...] = (u * gl_ref[...].astype(f32)).astype(bf16)
    h_scr[...] = jnp.broadcast_to(u[tl - 1:tl, :], h_scr.shape)
    xx_scr[0:SUBLANES, :] = x[tl - SUBLANES:tl, :]

    @pl.when(ti == pl.num_programs(1) - 1)
    def _():
        h_ref[0] = u[tl - 1:tl, :]
        cv_ref[0] = x[tl - (CONV_W - 1):tl, :]


def _lru_prompt(xl, gl, cw, cb, wa, ba, wx, bx, lam, bsz, t):
    tl = LRU_TILE
    nt = t // tl
    row = lambda: pl.BlockSpec((tl, LRU_W), lambda b, j: (b * nt + j, 0))
    vec = _resident((1, LRU_W))
    o, h, cv = pl.pallas_call(
        _lru_prompt_body,
        grid=(bsz, nt),
        in_specs=[row(), row(), _resident((CONV_W, LRU_W)), vec, _resident(wa.shape), vec,
                  _resident(wx.shape), vec, vec],
        out_specs=[row(), pl.BlockSpec((1, 1, LRU_W), lambda b, j: (b, 0, 0)),
                   pl.BlockSpec((1, CONV_W - 1, LRU_W), lambda b, j: (b, 0, 0))],
        out_shape=[jax.ShapeDtypeStruct((bsz * t, LRU_W), bf16), jax.ShapeDtypeStruct((bsz, 1, LRU_W), f32),
                   jax.ShapeDtypeStruct((bsz, CONV_W - 1, LRU_W), f32)],
        scratch_shapes=[pltpu.VMEM((tl + SUBLANES, LRU_W), f32), pltpu.VMEM((SUBLANES, LRU_W), f32)],
        compiler_params=_cparams(("parallel", "arbitrary")),
        name="lru_prompt",
    )(xl, gl, cw, cb.reshape(1, -1), wa, ba.reshape(1, -1), wx, bx.reshape(1, -1), lam.reshape(1, -1))
    return o, h[:, 0], cv


def _ret_sample_tables(t):
    log_g = _ret_log_gamma()
    r = jnp.arange(SUBLANES)
    tt = (r % t).astype(f32)
    same = (r[:, None] // t) == (r[None, :] // t)
    diff = tt[:, None] - tt[None, :]
    ok = same & (diff >= 0)
    dm = jnp.where(ok, jnp.exp(log_g[:, None, None] * jnp.where(ok, diff, 0.0)), 0.0)
    qd = jnp.exp(log_g[:, None] * (tt + 1.0))[:, :, None]
    kd = jnp.exp(log_g[:, None] * (t - 1.0 - tt))[:, :, None]
    sd = jnp.exp(log_g * t)
    return dm, qd, kd, sd


def _ret_sample_body(q_ref, k_ref, v_ref, g_ref, s_ref, dm_ref, qd_ref, kd_ref, sd_ref, o_ref, so_ref, *, t):
    per = SUBLANES // t
    row8 = lax.broadcasted_iota(jnp.int32, (SUBLANES, 1), 0)
    for p in range(SAMPLE_BB // per):
        rows = slice(p * SUBLANES, (p + 1) * SUBLANES)
        for h in range(RET_HEADS):
            ks = slice(h * RET_DK, (h + 1) * RET_DK)
            vs = slice(h * RET_DV, (h + 1) * RET_DV)
            q = q_ref[rows, ks].astype(f32)
            k = k_ref[rows, ks].astype(f32)
            v = v_ref[rows, vs].astype(f32)
            dm = dm_ref[h]
            o = _pairwise8(q, k, v, lambda s, dm=dm: (None, dm[:, s:s + 1]), 1)
            qb = q.astype(bf16)
            kdec = k * kd_ref[h]
            inter = jnp.zeros((SUBLANES, RET_DV), f32)
            for j in range(per):
                bi = p * per + j
                mine = (row8 // t) == j
                state = s_ref[bi, h]
                inter = inter + jnp.where(mine, _dot(qb, state.astype(bf16)), 0.0)
                kj = jnp.where(mine, kdec, 0.0).astype(bf16)
                so_ref[bi, h] = state * sd_ref[h] + _dot_ta(kj, v.astype(bf16))
            o = o + qd_ref[h] * inter
            o_ref[rows, vs] = _head_rms_gate(o, g_ref[rows, vs].astype(f32)).astype(bf16)


def _ret_sample(q, k, v, g, state, layer, bsz, t):
    nqk, nv = RET_HEADS * RET_DK, RET_HEADS * RET_DV
    bb = SAMPLE_BB
    dm, qd, kd, s_dec = _ret_sample_tables(t)
    sd = jnp.broadcast_to(s_dec[:, None, None], (RET_HEADS, 1, RET_DV))
    row = lambda w_: pl.BlockSpec((bb * t, w_), lambda i: (i, 0))
    return pl.pallas_call(
        functools.partial(_ret_sample_body, t=t),
        grid=(bsz // bb,),
        in_specs=[row(nqk), row(nqk), row(nv), row(nv),
                  pl.BlockSpec((None, bb, RET_HEADS, RET_DK, RET_DV), lambda i: (layer, i, 0, 0, 0)),
                  _resident(dm.shape), _resident(qd.shape), _resident(kd.shape), _resident(sd.shape)],
        out_specs=[row(nv), pl.BlockSpec((bb, RET_HEADS, RET_DK, RET_DV), lambda i: (i, 0, 0, 0))],
        out_shape=[jax.ShapeDtypeStruct((bsz * t, nv), bf16),
                   jax.ShapeDtypeStruct((bsz, RET_HEADS, RET_DK, RET_DV), f32)],
        compiler_params=_cparams(("parallel",)),
        name="ret_sample",
    )(q, k, v, g, state, dm, qd, kd, sd)


def _hgrn_sample_body(q_ref, lf_ref, i_ref, g_ref, nw_ref, s_ref, o_ref, so_ref, *, t):
    per = SUBLANES // t
    nrow = SAMPLE_BB * t
    row8 = lax.broadcasted_iota(jnp.int32, (SUBLANES, 1), 0)
    col8 = lax.broadcasted_iota(jnp.int32, (SUBLANES, SUBLANES), 1)
    pair_ok = ((lax.broadcasted_iota(jnp.int32, (SUBLANES, SUBLANES), 0) // t == col8 // t)
               & (lax.broadcasted_iota(jnp.int32, (SUBLANES, SUBLANES), 0) >= col8)).astype(f32)
    lf_all = lf_ref[...]
    rown = lax.broadcasted_iota(jnp.int32, (nrow, 1), 0) % t
    b_all = lf_all
    for d in range(1, t):
        b_all = b_all + jnp.where(rown >= d, pltpu.roll(lf_all, d, 0), 0.0)
    for p in range(SAMPLE_BB // per):
        rows = slice(p * SUBLANES, (p + 1) * SUBLANES)
        for h in range(HG_HEADS):
            sl = slice(h * HG_DK, (h + 1) * HG_DK)
            b = b_all[rows, sl]
            q = q_ref[rows, sl].astype(f32)
            kk = 1.0 - jnp.exp(lf_all[rows, sl])
            v = i_ref[rows, sl].astype(f32)
            b3 = b.reshape(1, SUBLANES, HG_DK)

            def weight(s, b3=b3):
                return jnp.exp(jnp.minimum(b3 - b3[:, s:s + 1, :], 0.0)), pair_ok[:, s:s + 1]

            o = _pairwise8(q, kk, v, weight, 1)
            qb = (q * jnp.exp(b)).astype(bf16)
            for j in range(per):
                bi = p * per + j
                mine = (row8 // t) == j
                state = s_ref[bi, h]
                o = o + jnp.where(mine, _dot(qb, state.astype(bf16)), 0.0)
                bl = b[(j + 1) * t - 1:(j + 1) * t, :]
                kj = jnp.where(mine, kk * jnp.exp(jnp.minimum(bl - b, 0.0)), 0.0).astype(bf16)
                dec_col = jnp.exp(jnp.broadcast_to(bl, (SUBLANES, HG_DK)).T[:, 0:1])
                so_ref[bi, h] = state * dec_col + _dot_ta(kj, v.astype(bf16))
            o_ref[rows, sl] = _head_rms_gate(o, nw_ref[:, sl] * g_ref[rows, sl].astype(f32)).astype(bf16)


def _hgrn_sample(q, lf, i, g, norm_w, state, layer, bsz, t):
    hw = HG_HEADS * HG_DK
    bb = SAMPLE_BB
    row = lambda: pl.BlockSpec((bb * t, hw), lambda j: (j, 0))
    return pl.pallas_call(
        functools.partial(_hgrn_sample_body, t=t),
        grid=(bsz // bb,),
        in_specs=[row(), row(), row(), row(), _resident((1, hw)),
                  pl.BlockSpec((None, bb, HG_HEADS, HG_DK, HG_DV), lambda j: (layer, j, 0, 0, 0))],
        out_specs=[row(), pl.BlockSpec((bb, HG_HEADS, HG_DK, HG_DV), lambda j: (j, 0, 0, 0))],
        out_shape=[jax.ShapeDtypeStruct((bsz * t, hw), bf16),
                   jax.ShapeDtypeStruct((bsz, HG_HEADS, HG_DK, HG_DV), f32)],
        compiler_params=_cparams(("parallel",)),
        name="hgrn_sample",
    )(q, lf, i, g, norm_w.reshape(1, -1), state)


def _lru_sample_body(x_ref, gl_ref, cs_ref, h0_ref, cw_ref, cb_ref, wa_ref, ba_ref, wx_ref, bx_ref, lam_ref,
                     o_ref, h_ref, cv_ref, *, t):
    xx = [cs_ref[j] for j in range(CONV_W - 1)] + [x_ref[j] for j in range(t)]
    cw = cw_ref[...]
    neg_c_sp = -LRU_C * _softplus(-lam_ref[...])
    h = h0_ref[...]
    for j in range(t):
        xc = cb_ref[...]
        for i in range(CONV_W):
            xc = xc + xx[j + i] * cw[i:i + 1, :]
        a, u = _lru_gates(xc, wa_ref, ba_ref[...], wx_ref, bx_ref[...], neg_c_sp)
        h = a * h + u
        o_ref[j] = (h * gl_ref[j].astype(f32)).astype(bf16)
    h_ref[...] = h
    for j in range(CONV_W - 1):
        cv_ref[j] = xx[t + j]


def _lru_sample(xl, gl, cs, h0, cw, cb, wa, ba, wx, bx, lam, bsz, t):
    full = lambda a: pl.BlockSpec(a.shape, lambda i: (0,) * a.ndim)
    args = (xl, gl, cs, h0, cw, cb.reshape(1, -1), wa, ba.reshape(1, -1), wx, bx.reshape(1, -1), lam.reshape(1, -1))
    return pl.pallas_call(
        functools.partial(_lru_sample_body, t=t),
        grid=(1,),
        in_specs=[full(a) for a in args],
        out_specs=[pl.BlockSpec((t, bsz, LRU_W), lambda i: (0, 0, 0)), pl.BlockSpec((bsz, LRU_W), lambda i: (0, 0)),
                   pl.BlockSpec((CONV_W - 1, bsz, LRU_W), lambda i: (0, 0, 0))],
        out_shape=[jax.ShapeDtypeStruct((t, bsz, LRU_W), bf16), jax.ShapeDtypeStruct((bsz, LRU_W), f32),
                   jax.ShapeDtypeStruct((CONV_W - 1, bsz, LRU_W), f32)],
        compiler_params=_cparams(("arbitrary",)),
        name="lru_sample",
    )(*args)


def _rope_tables(pos):
    inv = ROPE_BASE ** (-jnp.arange(0, RET_DK, 2, dtype=f32) / RET_DK)
    ang = pos.astype(f32)[:, None] * inv[None, :]
    cos, sin = jnp.cos(ang), jnp.sin(ang)
    return jnp.concatenate([cos, cos], axis=1), jnp.concatenate([-sin, sin], axis=1)


def _trunk(x, pos, w, states, bsz, t):
    cos, sin = _rope_tables(pos)
    cos = jnp.tile(cos, (bsz, 1))
    sin = jnp.tile(sin, (bsz, 1))
    news = ([], [], [], [])
    depth = w["w_ret"].shape[0]
    for l in range(depth):
        x = _ffn(x, w["ffn1_norm"][l], w["ffn1_wg"][l], w["ffn1_wu"][l], w["ffn1_wd"][l])
        q_r, k_r, v_r, g_r = _proj_ret(x, w["mix_norm"][l], w["w_ret"][l], cos, sin)
        q_h, lf, i_h, g_h = _proj_hgrn(x, w["mix_norm"][l], w["w_hg"][l], w["hgrn_lb_logits"], l)
        x_l, g_l, gs = _proj_lru(x, w["mix_norm"][l], w["w_lru"][l], w["merge_bias"][l])
        lru_w = (w["conv_w"][l], w["conv_b"][l], w["lru_w_a"][l], w["lru_b_a"][l], w["lru_w_x"][l],
                 w["lru_b_x"][l], w["lru_lambda"][l])
        if states is None:
            o_r, ret_new = _ret_prompt(q_r, k_r, v_r, g_r, bsz, t)
            o_h, hg_new = _hgrn_prompt(q_h, lf, i_h, g_h, w["hgrn_norm"][l], bsz, t)
            o_l, h_new, cv_new = _lru_prompt(x_l, g_l, *lru_w, bsz, t)
        else:
            st_ret, st_hg, st_lru, st_conv = states
            o_r, ret_new = _ret_sample(q_r, k_r, v_r, g_r, st_ret, l, bsz, t)
            o_h, hg_new = _hgrn_sample(q_h, lf, i_h, g_h, w["hgrn_norm"][l], st_hg, l, bsz, t)
            tm = lambda a: a.reshape(bsz, t, LRU_W).transpose(1, 0, 2)
            o_lt, h_new, cv_t = _lru_sample(tm(x_l), tm(g_l), st_conv[l].transpose(1, 0, 2), st_lru[l], *lru_w, bsz, t)
            o_l = o_lt.transpose(1, 0, 2).reshape(bsz * t, LRU_W)
            cv_new = cv_t.transpose(1, 0, 2)
        x = _mix_out(x, o_r, o_h, o_l, gs, w["w_ret_o"][l], w["w_hgrn_o"][l], w["w_lru_o"][l], w["w_mix_out"][l])
        last = l == depth - 1
        x = _ffn(x, w["ffn2_norm"][l], w["ffn2_wg"][l], w["ffn2_wu"][l], w["ffn2_wd"][l],
                 final_g=w["final_norm"] if last else None)
        for acc, s in zip(news, (ret_new, hg_new, h_new, cv_new)):
            acc.append(s)
    return x, [jnp.stack(a, axis=0) for a in news]


def kernel(x_prompt, x_sample, state_ret, state_hgrn, state_lru, state_conv, ffn1_norm, ffn1_w_gate, ffn1_w_up, ffn1_w_down, mix_norm, w_in, merge_bias, w_ret_o, hgrn_lb_logits, hgrn_norm, w_hgrn_o, conv_w, conv_b, lru_w_a, lru_b_a, lru_w_x, lru_b_x, lru_lambda, w_lru_o, w_mix_out, ffn2_norm, ffn2_w_gate, ffn2_w_up, ffn2_w_down, final_norm):
    c16 = lambda a: a.astype(bf16)
    w = dict(
        ffn1_norm=ffn1_norm, ffn1_wg=c16(ffn1_w_gate), ffn1_wu=c16(ffn1_w_up), ffn1_wd=c16(ffn1_w_down),
        mix_norm=mix_norm,
        w_ret=c16(w_in[:, :, :RET_COLS]), w_hg=c16(w_in[:, :, RET_COLS:RET_COLS + HG_COLS]),
        w_lru=c16(w_in[:, :, RET_COLS + HG_COLS:]),
        merge_bias=merge_bias, w_ret_o=c16(w_ret_o), hgrn_lb_logits=hgrn_lb_logits, hgrn_norm=hgrn_norm,
        w_hgrn_o=c16(w_hgrn_o), conv_w=conv_w, conv_b=conv_b, lru_w_a=c16(lru_w_a), lru_b_a=lru_b_a,
        lru_w_x=c16(lru_w_x), lru_b_x=lru_b_x, lru_lambda=lru_lambda, w_lru_o=c16(w_lru_o),
        w_mix_out=c16(w_mix_out), ffn2_norm=ffn2_norm, ffn2_wg=c16(ffn2_w_gate), ffn2_wu=c16(ffn2_w_up),
        ffn2_wd=c16(ffn2_w_down), final_norm=final_norm,
    )
    bp, tp, _ = x_prompt.shape
    bs, ts, _ = x_sample.shape
    y_p, st_p = _trunk(x_prompt.reshape(bp * tp, D_MODEL), jnp.arange(tp), w, None, bp, tp)
    y_s, st_s = _trunk(x_sample.reshape(bs * ts, D_MODEL), PAST_LEN + jnp.arange(ts), w,
                       (state_ret, state_hgrn, state_lru, state_conv), bs, ts)
    return (y_p.reshape(bp, tp, D_MODEL), y_s.reshape(bs, ts, D_MODEL),
            st_p[0], st_s[0], st_p[1], st_s[1], st_p[2], st_s[2], st_p[3], st_s[3])
```

```python
import functools

import jax
import jax.numpy as jnp
import numpy as np
from jax import lax
from jax.experimental import pallas as pl
from jax.experimental.pallas import tpu as pltpu

f32 = jnp.float32
bf16 = jnp.bfloat16

D_MODEL = 1024
FFN_HIDDEN = 2816
RET_HEADS, RET_DK, RET_DV = 4, 128, 256
HG_HEADS, HG_DK, HG_DV = 8, 128, 128
LRU_W, LRU_BLOCKS, LRU_BD = 1024, 8, 128
CONV_W = 4
LRU_C = 8.0
N_BRANCH = 3
ROPE_BASE = 10000.0
PAST_LEN = 16384
EPS = 1e-6
RET_COLS = 2 * RET_HEADS * RET_DK + 2 * RET_HEADS * RET_DV
HG_COLS = 4 * HG_HEADS * HG_DK
LRU_COLS = 2 * LRU_W + N_BRANCH * D_MODEL

V7X_VMEM_BYTES = 64 * 1024 * 1024
VMEM_LIMIT = 56 * 1024 * 1024
SUBLANES = 8

TOKEN_TILE = 512
FFN_CHUNK = 256
RET_CHUNK = 256
HG_CHUNK = 128
HG_LEVELS = ((128, 64), (64, 32), (32, 16), (16, 8), (8, 4), (4, 2))
LRU_TILE = 256
SAMPLE_BB = 8


def _cparams(sem):
    return pltpu.CompilerParams(dimension_semantics=sem, vmem_limit_bytes=VMEM_LIMIT)


def _resident(shape):
    nd = len(shape)
    return pl.BlockSpec(shape, lambda *_: (0,) * nd, pipeline_mode=pl.Buffered(1))


def _rmsnorm(x, g):
    return x * lax.rsqrt(jnp.mean(x * x, axis=-1, keepdims=True) + EPS) * g


def _dot(a, b):
    return jnp.dot(a, b, preferred_element_type=f32)


def _dot_tb(a, b):
    return lax.dot_general(a, b, (((1,), (1,)), ((), ())), preferred_element_type=f32)


def _dot_ta(a, b):
    return lax.dot_general(a, b, (((0,), (0,)), ((), ())), preferred_element_type=f32)


def _silu(x):
    return x * jax.nn.sigmoid(x)


def _log_sigmoid(x):
    return jnp.minimum(x, 0.0) - jnp.log1p(jnp.exp(-jnp.abs(x)))


def _softplus(x):
    return jnp.maximum(x, 0.0) + jnp.log1p(jnp.exp(-jnp.abs(x)))


def _block_diag(x, nb, rows_per):
    r, w = x.shape
    rb = lax.broadcasted_iota(jnp.int32, (r, nb * w), 0) // rows_per
    cb = lax.broadcasted_iota(jnp.int32, (r, nb * w), 1) // w
    return jnp.where(rb == cb, jnp.tile(x, (1, nb)), jnp.zeros((), x.dtype))


def _ffn_body(x_ref, g_ref, wg_ref, wu_ref, wd_ref, fg_ref, o_ref, *, final):
    x = x_ref[...]
    xn = _rmsnorm(x, g_ref[...]).astype(bf16)
    acc = jnp.zeros(x.shape, f32)
    for c in range(FFN_HIDDEN // FFN_CHUNK):
        sl = slice(c * FFN_CHUNK, (c + 1) * FFN_CHUNK)
        g = _dot(xn, wg_ref[:, sl])
        u = _dot(xn, wu_ref[:, sl])
        h = (_silu(g) * u).astype(bf16)
        acc = acc + _dot(h, wd_ref[sl, :])
    y = x + 0.5 * acc
    if final:
        y = _rmsnorm(y, fg_ref[...])
    o_ref[...] = y


def _ffn(x, g, wg, wu, wd, final_g=None):
    n = x.shape[0]
    tm = min(TOKEN_TILE, n)
    final = final_g is not None
    fg = final_g if final else g
    row = pl.BlockSpec((tm, D_MODEL), lambda i: (i, 0))
    return pl.pallas_call(
        functools.partial(_ffn_body, final=final),
        grid=(n // tm,),
        in_specs=[row, _resident((1, D_MODEL)), _resident(wg.shape), _resident(wu.shape),
                  _resident(wd.shape), _resident((1, D_MODEL))],
        out_specs=row,
        out_shape=jax.ShapeDtypeStruct((n, D_MODEL), f32),
        compiler_params=_cparams(("parallel",)),
        name="ffn",
    )(x, g.reshape(1, -1), wg, wu, wd, fg.reshape(1, -1))


def _proj_ret_body(x_ref, g_ref, w_ref, cos_ref, sin_ref, q_ref, k_ref, v_ref, gr_ref):
    xn = _rmsnorm(x_ref[...], g_ref[...]).astype(bf16)
    cos = cos_ref[...]
    sin = sin_ref[...]
    nqk = RET_HEADS * RET_DK

    def rot(z):
        return z * cos + pltpu.roll(z, RET_DK // 2, 1) * sin

    for c in range(nqk // 256):
        zq = _dot(xn, w_ref[:, c * 256:(c + 1) * 256])
        zk = _dot(xn, w_ref[:, nqk + c * 256:nqk + (c + 1) * 256])
        for j in range(2):
            sl = slice(c * 256 + j * 128, c * 256 + (j + 1) * 128)
            q_ref[:, sl] = rot(zq[:, j * 128:(j + 1) * 128]).astype(bf16)
            k_ref[:, sl] = (rot(zk[:, j * 128:(j + 1) * 128]) * (RET_DK ** -0.5)).astype(bf16)
    nv = RET_HEADS * RET_DV
    for c in range(nv // 256):
        sl = slice(c * 256, (c + 1) * 256)
        v_ref[:, sl] = _dot(xn, w_ref[:, 2 * nqk + c * 256:2 * nqk + (c + 1) * 256]).astype(bf16)
        zg = _dot(xn, w_ref[:, 2 * nqk + nv + c * 256:2 * nqk + nv + (c + 1) * 256])
        gr_ref[:, sl] = _silu(zg).astype(bf16)


def _proj_ret(x, g, w, cos, sin):
    n = x.shape[0]
    tm = min(TOKEN_TILE, n)
    row = lambda w_: pl.BlockSpec((tm, w_), lambda i: (i, 0))
    nqk, nv = RET_HEADS * RET_DK, RET_HEADS * RET_DV
    return pl.pallas_call(
        _proj_ret_body,
        grid=(n // tm,),
        in_specs=[row(D_MODEL), _resident((1, D_MODEL)), _resident(w.shape), row(RET_DK), row(RET_DK)],
        out_specs=[row(nqk), row(nqk), row(nv), row(nv)],
        out_shape=[jax.ShapeDtypeStruct((n, nqk), bf16), jax.ShapeDtypeStruct((n, nqk), bf16),
                   jax.ShapeDtypeStruct((n, nv), bf16), jax.ShapeDtypeStruct((n, nv), bf16)],
        compiler_params=_cparams(("parallel",)),
        name="proj_ret",
    )(x, g.reshape(1, -1), w, cos, sin)


def _proj_hgrn_body(x_ref, g_ref, w_ref, lbl_ref, q_ref, lf_ref, i_ref, gh_ref, *, layer):
    xn = _rmsnorm(x_ref[...], g_ref[...]).astype(bf16)
    hw = HG_HEADS * HG_DK
    if layer > 0:
        lg = lbl_ref[...]
        e = jnp.exp(lg - jnp.max(lg, axis=0, keepdims=True))
        sm = e / jnp.sum(e, axis=0, keepdims=True)
        lb = jnp.sum(sm[1:layer + 1], axis=0, keepdims=True)
        log_lb = jnp.log(lb)
        log_1mlb = jnp.log1p(-lb)
    for c in range(hw // 256):
        sl = slice(c * 256, (c + 1) * 256)
        q_ref[:, sl] = _silu(_dot(xn, w_ref[:, sl])).astype(bf16)
        zf = _dot(xn, w_ref[:, hw + c * 256:hw + (c + 1) * 256])
        ls = _log_sigmoid(zf)
        if layer > 0:
            a = log_lb[:, sl]
            b = log_1mlb[:, sl] + ls
            ls = jnp.maximum(a, b) + jnp.log1p(jnp.exp(-jnp.abs(a - b)))
        lf_ref[:, sl] = ls
        i_ref[:, sl] = _dot(xn, w_ref[:, 2 * hw + c * 256:2 * hw + (c + 1) * 256]).astype(bf16)
        gh_ref[:, sl] = _silu(_dot(xn, w_ref[:, 3 * hw + c * 256:3 * hw + (c + 1) * 256])).astype(bf16)


def _proj_hgrn(x, g, w, lb_logits, layer):
    n = x.shape[0]
    tm = min(TOKEN_TILE, n)
    hw = HG_HEADS * HG_DK
    row = lambda w_: pl.BlockSpec((tm, w_), lambda i: (i, 0))
    return pl.pallas_call(
        functools.partial(_proj_hgrn_body, layer=layer),
        grid=(n // tm,),
        in_specs=[row(D_MODEL), _resident((1, D_MODEL)), _resident(w.shape), _resident(lb_logits.shape)],
        out_specs=[row(hw), row(hw), row(hw), row(hw)],
        out_shape=[jax.ShapeDtypeStruct((n, hw), bf16), jax.ShapeDtypeStruct((n, hw), f32),
                   jax.ShapeDtypeStruct((n, hw), bf16), jax.ShapeDtypeStruct((n, hw), bf16)],
        compiler_params=_cparams(("parallel",)),
        name="proj_hgrn",
    )(x, g.reshape(1, -1), w, lb_logits)


def _proj_lru_body(x_ref, g_ref, w_ref, mb_ref, xl_ref, gl_ref, gs_ref):
    xn = _rmsnorm(x_ref[...], g_ref[...]).astype(bf16)
    for c in range(LRU_W // 256):
        sl = slice(c * 256, (c + 1) * 256)
        xl_ref[:, sl] = _dot(xn, w_ref[:, sl])
        zg = _dot(xn, w_ref[:, LRU_W + c * 256:LRU_W + (c + 1) * 256])
        gl_ref[:, sl] = jax.nn.gelu(zg, approximate=True).astype(bf16)
    for c in range(N_BRANCH * D_MODEL // 256):
        sl = slice(c * 256, (c + 1) * 256)
        z = _dot(xn, w_ref[:, 2 * LRU_W + c * 256:2 * LRU_W + (c + 1) * 256])
        gs_ref[:, sl] = jax.nn.sigmoid(z + mb_ref[:, sl]).astype(bf16)


def _proj_lru(x, g, w, merge_bias):
    n = x.shape[0]
    tm = min(TOKEN_TILE, n)
    row = lambda w_: pl.BlockSpec((tm, w_), lambda i: (i, 0))
    ng = N_BRANCH * D_MODEL
    return pl.pallas_call(
        _proj_lru_body,
        grid=(n // tm,),
        in_specs=[row(D_MODEL), _resident((1, D_MODEL)), _resident(w.shape), _resident((1, ng))],
        out_specs=[row(LRU_W), row(LRU_W), row(ng)],
        out_shape=[jax.ShapeDtypeStruct((n, LRU_W), f32), jax.ShapeDtypeStruct((n, LRU_W), bf16),
                   jax.ShapeDtypeStruct((n, ng), bf16)],
        compiler_params=_cparams(("parallel",)),
        name="proj_lru",
    )(x, g.reshape(1, -1), w, merge_bias.reshape(1, -1))


def _mix_out_body(x_ref, or_ref, oh_ref, ol_ref, gs_ref, wr_ref, wh_ref, wl_ref, wo_ref, o_ref):
    d = D_MODEL
    merged = gs_ref[:, 0:d].astype(f32) * _dot(or_ref[...], wr_ref[...])
    merged = merged + gs_ref[:, d:2 * d].astype(f32) * _dot(oh_ref[...], wh_ref[...])
    merged = merged + gs_ref[:, 2 * d:3 * d].astype(f32) * _dot(ol_ref[...], wl_ref[...])
    o_ref[...] = x_ref[...] + _dot(merged.astype(bf16), wo_ref[...])


def _mix_out(x, o_r, o_h, o_l, gs, wr, wh, wl, wo):
    n = x.shape[0]
    tm = min(TOKEN_TILE, n)
    row = lambda w_: pl.BlockSpec((tm, w_), lambda i: (i, 0))
    return pl.pallas_call(
        _mix_out_body,
        grid=(n // tm,),
        in_specs=[row(D_MODEL), row(D_MODEL), row(D_MODEL), row(D_MODEL), row(N_BRANCH * D_MODEL),
                  _resident(wr.shape), _resident(wh.shape), _resident(wl.shape), _resident(wo.shape)],
        out_specs=row(D_MODEL),
        out_shape=jax.ShapeDtypeStruct((n, D_MODEL), f32),
        compiler_params=_cparams(("parallel",)),
        name="mix_out",
    )(x, o_r, o_h, o_l, gs, wr, wh, wl, wo)


def _ret_log_gamma():
    return jnp.log1p(-jnp.exp2(-5.0 - jnp.arange(RET_HEADS, dtype=f32)))


def _ret_tables(c):
    log_g = _ret_log_gamma()
    idx = jnp.arange(c, dtype=f32)
    diff = idx[:, None] - idx[None, :]
    causal = diff >= 0
    d_intra = jnp.where(causal, jnp.exp(log_g[:, None, None] * jnp.where(causal, diff, 0.0)), 0.0)
    q_dec = jnp.exp(log_g[:, None] * (idx + 1.0))[:, :, None]
    k_dec = jnp.exp(log_g[:, None] * (c - 1.0 - idx))[:, :, None]
    s_dec = jnp.exp(log_g * c)
    return d_intra, q_dec, k_dec, s_dec


def _head_rms_gate(o, gate):
    return o * lax.rsqrt(jnp.mean(o * o, axis=-1, keepdims=True) + EPS) * gate


def _ret_prompt_body(q_ref, k_ref, v_ref, g_ref, dm_ref, qd_ref, kd_ref, sd_ref, o_ref, st_ref, s_scr):
    c = pl.program_id(1)

    @pl.when(c == 0)
    def _():
        s_scr[...] = jnp.zeros(s_scr.shape, f32)

    for h in range(RET_HEADS):
        ks = slice(h * RET_DK, (h + 1) * RET_DK)
        vs = slice(h * RET_DV, (h + 1) * RET_DV)
        q = q_ref[:, ks]
        k = k_ref[:, ks]
        v = v_ref[:, vs]
        state = s_scr[h]
        scores = (_dot_tb(q, k) * dm_ref[h]).astype(bf16)
        o = _dot(scores, v) + qd_ref[h] * _dot(q, state.astype(bf16))
        kd = (k.astype(f32) * kd_ref[h]).astype(bf16)
        s_scr[h] = state * sd_ref[h] + _dot_ta(kd, v)
        o_ref[:, vs] = _head_rms_gate(o, g_ref[:, vs].astype(f32)).astype(bf16)

    @pl.when(c == pl.num_programs(1) - 1)
    def _():
        st_ref[0] = s_scr[...]


def _ret_prompt(q, k, v, g, bsz, t):
    c = RET_CHUNK
    nc = t // c
    nqk, nv = RET_HEADS * RET_DK, RET_HEADS * RET_DV
    d_intra, q_dec, k_dec, s_dec = _ret_tables(c)
    sd = jnp.broadcast_to(s_dec[:, None, None], (RET_HEADS, 1, RET_DV))
    row = lambda w_: pl.BlockSpec((c, w_), lambda b, i: (b * nc + i, 0))
    return pl.pallas_call(
        _ret_prompt_body,
        grid=(bsz, nc),
        in_specs=[row(nqk), row(nqk), row(nv), row(nv), _resident(d_intra.shape), _resident(q_dec.shape),
                  _resident(k_dec.shape), _resident(sd.shape)],
        out_specs=[row(nv), pl.BlockSpec((1, RET_HEADS, RET_DK, RET_DV), lambda b, i: (b, 0, 0, 0))],
        out_shape=[jax.ShapeDtypeStruct((bsz * t, nv), bf16),
                   jax.ShapeDtypeStruct((bsz, RET_HEADS, RET_DK, RET_DV), f32)],
        scratch_shapes=[pltpu.VMEM((RET_HEADS, RET_DK, RET_DV), f32)],
        compiler_params=_cparams(("parallel", "arbitrary")),
        name="ret_prompt",
    )(q, k, v, g, d_intra, q_dec, k_dec, sd)


def _hg_level_masks(c):
    t = np.arange(c)[:, None]
    s = np.arange(c)[None, :]
    ms = [((t // blk == s // blk) & (t % blk >= half) & (s % blk < half)) for blk, half in HG_LEVELS]
    return jnp.asarray(np.stack(ms).astype(np.float32))


def _boundary_rows(b, blk, half):
    c, w = b.shape
    if half >= SUBLANES:
        return jnp.concatenate(
            [jnp.broadcast_to(b[j * blk + half - 1:j * blk + half, :], (blk, w)) for j in range(c // blk)], axis=0)
    b3 = b.reshape(c // SUBLANES, SUBLANES, w)
    row8 = lax.broadcasted_iota(jnp.int32, (1, SUBLANES, 1), 1)
    out = b3[:, half - 1:half, :]
    for j in range(1, SUBLANES // blk):
        out = jnp.where(row8 >= j * blk, b3[:, j * blk + half - 1:j * blk + half, :], out)
    return jnp.broadcast_to(out, b3.shape).reshape(c, w)


def _hgrn_prompt_body(q_ref, lf_ref, i_ref, g_ref, nw_ref, tri_ref, mask_ref, o_ref, st_ref, s_scr):
    ci = pl.program_id(1)
    c = HG_CHUNK

    @pl.when(ci == 0)
    def _():
        s_scr[...] = jnp.zeros(s_scr.shape, f32)

    lf_all = lf_ref[...]
    hi = lf_all.astype(bf16)
    lo = (lf_all - hi.astype(f32)).astype(bf16)
    tri = tri_ref[...]
    b_all = _dot(tri, hi) + _dot(tri, lo)
    odd = lax.broadcasted_iota(jnp.int32, (c, 1), 0) % 2 == 1

    for h in range(HG_HEADS):
        sl = slice(h * HG_DK, (h + 1) * HG_DK)
        b = b_all[:, sl]
        q = q_ref[:, sl].astype(f32)
        f = jnp.exp(lf_all[:, sl])
        kk = 1.0 - f
        v16 = i_ref[:, sl]
        a_mat = jnp.zeros((c, c), f32)
        for lev, (blk, half) in enumerate(HG_LEVELS):
            e = jnp.exp(-jnp.abs(b - _boundary_rows(b, blk, half)))
            a_mat = a_mat + _dot_tb((q * e).astype(bf16), (kk * e).astype(bf16)) * mask_ref[lev]
        o = _dot(a_mat.astype(bf16), v16)
        vf = v16.astype(f32)
        a0 = jnp.sum(q * kk, axis=-1, keepdims=True)
        a1 = jnp.where(odd, jnp.sum(q * f * pltpu.roll(kk, 1, 0), axis=-1, keepdims=True), 0.0)
        o = o + a0 * vf + a1 * pltpu.roll(vf, 1, 0)
        st = s_scr[h]
        o = o + _dot_tb((q * jnp.exp(b)).astype(bf16), st.astype(bf16))
        bl = b[c - 1:c, :]
        kdec = (kk * jnp.exp(bl - b)).astype(bf16)
        s_scr[h] = st * jnp.exp(bl) + _dot_ta(v16, kdec)
        o_ref[:, sl] = _head_rms_gate(o, nw_ref[:, sl] * g_ref[:, sl].astype(f32)).astype(bf16)

    @pl.when(ci == pl.num_programs(1) - 1)
    def _():
        for h in range(HG_HEADS):
            st_ref[0, h] = s_scr[h].T


def _hgrn_prompt(q, lf, i, g, norm_w, bsz, t):
    c = HG_CHUNK
    nc = t // c
    hw = HG_HEADS * HG_DK
    tri = jnp.asarray(np.tril(np.ones((c, c), np.float32))).astype(bf16)
    masks = _hg_level_masks(c)
    row = lambda: pl.BlockSpec((c, hw), lambda b, j: (b * nc + j, 0))
    return pl.pallas_call(
        _hgrn_prompt_body,
        grid=(bsz, nc),
        in_specs=[row(), row(), row(), row(), _resident((1, hw)), _resident(tri.shape), _resident(masks.shape)],
        out_specs=[row(), pl.BlockSpec((1, HG_HEADS, HG_DK, HG_DV), lambda b, j: (b, 0, 0, 0))],
        out_shape=[jax.ShapeDtypeStruct((bsz * t, hw), bf16),
                   jax.ShapeDtypeStruct((bsz, HG_HEADS, HG_DK, HG_DV), f32)],
        scratch_shapes=[pltpu.VMEM((HG_HEADS, HG_DV, HG_DK), f32)],
        compiler_params=_cparams(("parallel", "arbitrary")),
        name="hgrn_prompt",
    )(q, lf, i, g, norm_w.reshape(1, -1), tri, masks)


def _lru_gates(xc, wa_ref, ba, wx_ref, bx, neg_c_sp):
    xb = xc.astype(bf16)
    a_parts, u_parts = [], []
    for n in range(LRU_BLOCKS):
        sl = slice(n * LRU_BD, (n + 1) * LRU_BD)
        r = jax.nn.sigmoid(_dot(xb[:, sl], wa_ref[n]) + ba[:, sl])
        gi = jax.nn.sigmoid(_dot(xb[:, sl], wx_ref[n]) + bx[:, sl])
        log_a = neg_c_sp[:, sl] * r
        a = jnp.exp(log_a)
        a_parts.append(a)
        u_parts.append(jnp.sqrt(-jnp.tanh(log_a) * (1.0 + a * a)) * (gi * xc[:, sl]))
    return jnp.concatenate(a_parts, axis=1), jnp.concatenate(u_parts, axis=1)


def _lru_prompt_body(x_ref, gl_ref, cw_ref, cb_ref, wa_ref, ba_ref, wx_ref, bx_ref, lam_ref,
                     o_ref, h_ref, cv_ref, xx_scr, h_scr):
    ti = pl.program_id(1)
    tl = LRU_TILE

    @pl.when(ti == 0)
    def _():
        xx_scr[0:SUBLANES, :] = jnp.zeros((SUBLANES, LRU_W), f32)
        h_scr[...] = jnp.zeros(h_scr.shape, f32)

    x = x_ref[...]
    xx_scr[SUBLANES:SUBLANES + tl, :] = x
    cw = cw_ref[...]
    xc = cb_ref[...] + x * cw[CONV_W - 1:CONV_W, :]
    for j in range(CONV_W - 1):
        off = SUBLANES - (CONV_W - 1) + j
        xc = xc + xx_scr[off:off + tl, :] * cw[j:j + 1, :]
    neg_c_sp = -LRU_C * _softplus(-lam_ref[...])
    a, u = _lru_gates(xc, wa_ref, ba_ref[...], wx_ref, bx_ref[...], neg_c_sp)
    ng = tl // SUBLANES
    a3 = a.reshape(ng, SUBLANES, LRU_W)
    u3 = u.reshape(ng, SUBLANES, LRU_W)
    row8 = lax.broadcasted_iota(jnp.int32, (1, SUBLANES, 1), 1)
    step = 1
    while step < SUBLANES:
        keep = row8 >= step
        u3 = a3 * jnp.where(keep, pltpu.roll(u3, step, 1), 0.0) + u3
        a3 = a3 * jnp.where(keep, pltpu.roll(a3, step, 1), 1.0)
        step *= 2
    h = h_scr[0:1, :]
    groups = []
    for j in range(ng):
        hj = a3[j] * h + u3[j]
        groups.append(hj)
        h = hj[SUBLANES - 1:SUBLANES, :]
    hs = jnp.concatenate(groups, axis=0)
    o_ref[...] = (hs * gl_ref[...].astype(f32)).astype(bf16)
    h_scr[...] = jnp.broadcast_to(h, h_scr.shape)
    xx_scr[0:SUBLANES, :] = x[tl - SUBLANES:tl, :]

    @pl.when(ti == pl.num_programs(1) - 1)
    def _():
        h_ref[0] = h
        cv_ref[0] = x[tl - (CONV_W - 1):tl, :]


def _lru_prompt(xl, gl, cw, cb, wa, ba, wx, bx, lam, bsz, t):
    tl = LRU_TILE
    nt = t // tl
    row = lambda: pl.BlockSpec((tl, LRU_W), lambda b, j: (b * nt + j, 0))
    vec = _resident((1, LRU_W))
    o, h, cv = pl.pallas_call(
        _lru_prompt_body,
        grid=(bsz, nt),
        in_specs=[row(), row(), _resident((CONV_W, LRU_W)), vec, _resident(wa.shape), vec,
                  _resident(wx.shape), vec, vec],
        out_specs=[row(), pl.BlockSpec((1, 1, LRU_W), lambda b, j: (b, 0, 0)),
                   pl.BlockSpec((1, CONV_W - 1, LRU_W), lambda b, j: (b, 0, 0))],
        out_shape=[jax.ShapeDtypeStruct((bsz * t, LRU_W), bf16), jax.ShapeDtypeStruct((bsz, 1, LRU_W), f32),
                   jax.ShapeDtypeStruct((bsz, CONV_W - 1, LRU_W), f32)],
        scratch_shapes=[pltpu.VMEM((tl + SUBLANES, LRU_W), f32), pltpu.VMEM((SUBLANES, LRU_W), f32)],
        compiler_params=_cparams(("parallel", "arbitrary")),
        name="lru_prompt",
    )(xl, gl, cw, cb.reshape(1, -1), wa, ba.reshape(1, -1), wx, bx.reshape(1, -1), lam.reshape(1, -1))
    return o, h[:, 0], cv


def _ret_sample_tables(t):
    log_g = _ret_log_gamma()
    r = jnp.arange(SAMPLE_BB * t)
    tt = (r % t).astype(f32)
    same = (r[:, None] // t) == (r[None, :] // t)
    diff = tt[:, None] - tt[None, :]
    ok = same & (diff >= 0)
    dm = jnp.where(ok, jnp.exp(log_g[:, None, None] * jnp.where(ok, diff, 0.0)), 0.0)
    qd = jnp.exp(log_g[:, None] * (tt + 1.0))[:, :, None]
    kd = jnp.exp(log_g[:, None] * (t - 1.0 - tt))[:, :, None]
    sd = jnp.exp(log_g * t)
    return dm, qd, kd, sd


def _ret_sample_body(q_ref, k_ref, v_ref, g_ref, s_ref, dm_ref, qd_ref, kd_ref, sd_ref, o_ref, so_ref, *, t):
    bb = SAMPLE_BB
    for h in range(RET_HEADS):
        ks = slice(h * RET_DK, (h + 1) * RET_DK)
        vs = slice(h * RET_DV, (h + 1) * RET_DV)
        q = q_ref[:, ks]
        k = k_ref[:, ks]
        v = v_ref[:, vs]
        state = s_ref[:, h]
        scores = (_dot_tb(q, k) * dm_ref[h]).astype(bf16)
        inter = _dot(_block_diag(q, bb, t), state.reshape(bb * RET_DK, RET_DV).astype(bf16))
        o = _dot(scores, v) + qd_ref[h] * inter
        kd = (k.astype(f32) * kd_ref[h]).astype(bf16)
        ds = _dot_ta(kd, _block_diag(v, bb, t))
        for b in range(bb):
            so_ref[b, h] = state[b] * sd_ref[h] + ds[:, b * RET_DV:(b + 1) * RET_DV]
        o_ref[:, vs] = _head_rms_gate(o, g_ref[:, vs].astype(f32)).astype(bf16)


def _ret_sample(q, k, v, g, state, layer, bsz, t):
    nqk, nv = RET_HEADS * RET_DK, RET_HEADS * RET_DV
    bb = SAMPLE_BB
    dm, qd, kd, s_dec = _ret_sample_tables(t)
    sd = jnp.broadcast_to(s_dec[:, None, None], (RET_HEADS, 1, RET_DV))
    row = lambda w_: pl.BlockSpec((bb * t, w_), lambda i: (i, 0))
    return pl.pallas_call(
        functools.partial(_ret_sample_body, t=t),
        grid=(bsz // bb,),
        in_specs=[row(nqk), row(nqk), row(nv), row(nv),
                  pl.BlockSpec((None, bb, RET_HEADS, RET_DK, RET_DV), lambda i: (layer, i, 0, 0, 0)),
                  _resident(dm.shape), _resident(qd.shape), _resident(kd.shape), _resident(sd.shape)],
        out_specs=[row(nv), pl.BlockSpec((bb, RET_HEADS, RET_DK, RET_DV), lambda i: (i, 0, 0, 0))],
        out_shape=[jax.ShapeDtypeStruct((bsz * t, nv), bf16),
                   jax.ShapeDtypeStruct((bsz, RET_HEADS, RET_DK, RET_DV), f32)],
        compiler_params=_cparams(("parallel",)),
        name="ret_sample",
    )(q, k, v, g, state, dm, qd, kd, sd)


def _hgrn_sample_body(q_ref, lf_ref, i_ref, g_ref, nw_ref, s_ref, o_ref, so_ref, *, t):
    assert t >= 2
    bb = SAMPLE_BB
    n = bb * t
    tt = lax.broadcasted_iota(jnp.int32, (n, 1), 0) % t
    f = jnp.exp(lf_ref[...])
    kk = 1.0 - f
    q = q_ref[...].astype(f32)
    v = i_ref[...].astype(f32)
    before = [jnp.where(tt >= d, pltpu.roll(f, d, 0), 1.0) for d in range(1, t)]
    after = [jnp.where(tt < t - d, pltpu.roll(f, n - d, 0), 1.0) for d in range(1, t)]
    dec = [None, f]
    for d in range(2, t):
        dec.append(dec[-1] * before[d - 2])
    pre = dec[t - 1] * before[t - 2]
    suf = after[0]
    for d in range(1, t - 1):
        suf = suf * after[d]
    total = pre * suf
    qp = (q * pre).astype(bf16)
    kp = (kk * suf).astype(bf16)
    k_back = [kk] + [pltpu.roll(kk, d, 0) for d in range(1, t)]
    v_back = [v] + [pltpu.roll(v, d, 0) for d in range(1, t)]
    for h in range(HG_HEADS):
        sl = slice(h * HG_DK, (h + 1) * HG_DK)
        o = jnp.zeros((n, HG_DV), f32)
        for d in range(t):
            p = q[:, sl] * k_back[d][:, sl]
            if d > 0:
                p = p * dec[d][:, sl]
            a = jnp.sum(p, axis=-1, keepdims=True)
            if d > 0:
                a = jnp.where(tt >= d, a, 0.0)
            o = o + a * v_back[d][:, sl]
        state = s_ref[:, h]
        o = o + _dot(_block_diag(qp[:, sl], bb, t), state.reshape(bb * HG_DK, HG_DV).astype(bf16))
        ds = _dot_ta(kp[:, sl], _block_diag(i_ref[:, sl], bb, t))
        total_t = total[:, sl].T
        for b in range(bb):
            so_ref[b, h] = state[b] * total_t[:, b * t:b * t + 1] + ds[:, b * HG_DV:(b + 1) * HG_DV]
        o_ref[:, sl] = _head_rms_gate(o, nw_ref[:, sl] * g_ref[:, sl].astype(f32)).astype(bf16)


def _hgrn_sample(q, lf, i, g, norm_w, state, layer, bsz, t):
    hw = HG_HEADS * HG_DK
    bb = SAMPLE_BB
    row = lambda: pl.BlockSpec((bb * t, hw), lambda j: (j, 0))
    return pl.pallas_call(
        functools.partial(_hgrn_sample_body, t=t),
        grid=(bsz // bb,),
        in_specs=[row(), row(), row(), row(), _resident((1, hw)),
                  pl.BlockSpec((None, bb, HG_HEADS, HG_DK, HG_DV), lambda j: (layer, j, 0, 0, 0))],
        out_specs=[row(), pl.BlockSpec((bb, HG_HEADS, HG_DK, HG_DV), lambda j: (j, 0, 0, 0))],
        out_shape=[jax.ShapeDtypeStruct((bsz * t, hw), bf16),
                   jax.ShapeDtypeStruct((bsz, HG_HEADS, HG_DK, HG_DV), f32)],
        compiler_params=_cparams(("parallel",)),
        name="hgrn_sample",
    )(q, lf, i, g, norm_w.reshape(1, -1), state)


def _lru_sample_body(x_ref, gl_ref, cs_ref, h0_ref, cw_ref, cb_ref, wa_ref, ba_ref, wx_ref, bx_ref, lam_ref,
                     o_ref, h_ref, cv_ref, *, t):
    xx = [cs_ref[j] for j in range(CONV_W - 1)] + [x_ref[j] for j in range(t)]
    cw = cw_ref[...]
    neg_c_sp = -LRU_C * _softplus(-lam_ref[...])
    h = h0_ref[...]
    for j in range(t):
        xc = cb_ref[...]
        for i in range(CONV_W):
            xc = xc + xx[j + i] * cw[i:i + 1, :]
        a, u = _lru_gates(xc, wa_ref, ba_ref[...], wx_ref, bx_ref[...], neg_c_sp)
        h = a * h + u
        o_ref[j] = (h * gl_ref[j].astype(f32)).astype(bf16)
    h_ref[...] = h
    for j in range(CONV_W - 1):
        cv_ref[j] = xx[t + j]


def _lru_sample(xl, gl, cs, h0, cw, cb, wa, ba, wx, bx, lam, bsz, t):
    full = lambda a: pl.BlockSpec(a.shape, lambda i: (0,) * a.ndim)
    args = (xl, gl, cs, h0, cw, cb.reshape(1, -1), wa, ba.reshape(1, -1), wx, bx.reshape(1, -1), lam.reshape(1, -1))
    return pl.pallas_call(
        functools.partial(_lru_sample_body, t=t),
        grid=(1,),
        in_specs=[full(a) for a in args],
        out_specs=[pl.BlockSpec((t, bsz, LRU_W), lambda i: (0, 0, 0)), pl.BlockSpec((bsz, LRU_W), lambda i: (0, 0)),
                   pl.BlockSpec((CONV_W - 1, bsz, LRU_W), lambda i: (0, 0, 0))],
        out_shape=[jax.ShapeDtypeStruct((t, bsz, LRU_W), bf16), jax.ShapeDtypeStruct((bsz, LRU_W), f32),
                   jax.ShapeDtypeStruct((CONV_W - 1, bsz, LRU_W), f32)],
        compiler_params=_cparams(("arbitrary",)),
        name="lru_sample",
    )(*args)


def _rope_tables(pos):
    inv = ROPE_BASE ** (-jnp.arange(0, RET_DK, 2, dtype=f32) / RET_DK)
    ang = pos.astype(f32)[:, None] * inv[None, :]
    cos, sin = jnp.cos(ang), jnp.sin(ang)
    return jnp.concatenate([cos, cos], axis=1), jnp.concatenate([-sin, sin], axis=1)


def _trunk(x, pos, w, states, bsz, t):
    cos, sin = _rope_tables(pos)
    cos = jnp.tile(cos, (bsz, 1))
    sin = jnp.tile(sin, (bsz, 1))
    news = ([], [], [], [])
    depth = w["w_ret"].shape[0]
    for l in range(depth):
        x = _ffn(x, w["ffn1_norm"][l], w["ffn1_wg"][l], w["ffn1_wu"][l], w["ffn1_wd"][l])
        q_r, k_r, v_r, g_r = _proj_ret(x, w["mix_norm"][l], w["w_ret"][l], cos, sin)
        q_h, lf, i_h, g_h = _proj_hgrn(x, w["mix_norm"][l], w["w_hg"][l], w["hgrn_lb_logits"], l)
        x_l, g_l, gs = _proj_lru(x, w["mix_norm"][l], w["w_lru"][l], w["merge_bias"][l])
        lru_w = (w["conv_w"][l], w["conv_b"][l], w["lru_w_a"][l], w["lru_b_a"][l], w["lru_w_x"][l],
                 w["lru_b_x"][l], w["lru_lambda"][l])
        if states is None:
            o_r, ret_new = _ret_prompt(q_r, k_r, v_r, g_r, bsz, t)
            o_h, hg_new = _hgrn_prompt(q_h, lf, i_h, g_h, w["hgrn_norm"][l], bsz, t)
            o_l, h_new, cv_new = _lru_prompt(x_l, g_l, *lru_w, bsz, t)
        else:
            st_ret, st_hg, st_lru, st_conv = states
            o_r, ret_new = _ret_sample(q_r, k_r, v_r, g_r, st_ret, l, bsz, t)
            o_h, hg_new = _hgrn_sample(q_h, lf, i_h, g_h, w["hgrn_norm"][l], st_hg, l, bsz, t)
            tm = lambda a: a.reshape(bsz, t, LRU_W).transpose(1, 0, 2)
            o_lt, h_new, cv_t = _lru_sample(tm(x_l), tm(g_l), st_conv[l].transpose(1, 0, 2), st_lru[l], *lru_w, bsz, t)
            o_l = o_lt.transpose(1, 0, 2).reshape(bsz * t, LRU_W)
            cv_new = cv_t.transpose(1, 0, 2)
        x = _mix_out(x, o_r, o_h, o_l, gs, w["w_ret_o"][l], w["w_hgrn_o"][l], w["w_lru_o"][l], w["w_mix_out"][l])
        last = l == depth - 1
        x = _ffn(x, w["ffn2_norm"][l], w["ffn2_wg"][l], w["ffn2_wu"][l], w["ffn2_wd"][l],
                 final_g=w["final_norm"] if last else None)
        for acc, s in zip(news, (ret_new, hg_new, h_new, cv_new)):
            acc.append(s)
    return x, [jnp.stack(a, axis=0) for a in news]


def kernel(x_prompt, x_sample, state_ret, state_hgrn, state_lru, state_conv, ffn1_norm, ffn1_w_gate, ffn1_w_up, ffn1_w_down, mix_norm, w_in, merge_bias, w_ret_o, hgrn_lb_logits, hgrn_norm, w_hgrn_o, conv_w, conv_b, lru_w_a, lru_b_a, lru_w_x, lru_b_x, lru_lambda, w_lru_o, w_mix_out, ffn2_norm, ffn2_w_gate, ffn2_w_up, ffn2_w_down, final_norm):
    c16 = lambda a: a.astype(bf16)
    w = dict(
        ffn1_norm=ffn1_norm, ffn1_wg=c16(ffn1_w_gate), ffn1_wu=c16(ffn1_w_up), ffn1_wd=c16(ffn1_w_down),
        mix_norm=mix_norm,
        w_ret=c16(w_in[:, :, :RET_COLS]), w_hg=c16(w_in[:, :, RET_COLS:RET_COLS + HG_COLS]),
        w_lru=c16(w_in[:, :, RET_COLS + HG_COLS:]),
        merge_bias=merge_bias, w_ret_o=c16(w_ret_o), hgrn_lb_logits=hgrn_lb_logits, hgrn_norm=hgrn_norm,
        w_hgrn_o=c16(w_hgrn_o), conv_w=conv_w, conv_b=conv_b, lru_w_a=c16(lru_w_a), lru_b_a=lru_b_a,
        lru_w_x=c16(lru_w_x), lru_b_x=lru_b_x, lru_lambda=lru_lambda, w_lru_o=c16(w_lru_o),
        w_mix_out=c16(w_mix_out), ffn2_norm=ffn2_norm, ffn2_wg=c16(ffn2_w_gate), ffn2_wu=c16(ffn2_w_up),
        ffn2_wd=c16(ffn2_w_down), final_norm=final_norm,
    )
    bp, tp, _ = x_prompt.shape
    bs, ts, _ = x_sample.shape
    y_p, st_p = _trunk(x_prompt.reshape(bp * tp, D_MODEL), jnp.arange(tp), w, None, bp, tp)
    y_s, st_s = _trunk(x_sample.reshape(bs * ts, D_MODEL), PAST_LEN + jnp.arange(ts), w,
                       (state_ret, state_hgrn, state_lru, state_conv), bs, ts)
    return (y_p.reshape(bp, tp, D_MODEL), y_s.reshape(bs, ts, D_MODEL),
            st_p[0], st_s[0], st_p[1], st_s[1], st_p[2], st_s[2], st_p[3], st_s[3])
```

```python
import functools

import jax
import jax.numpy as jnp
import numpy as np
from jax import lax
from jax.experimental import pallas as pl
from jax.experimental.pallas import tpu as pltpu

f32 = jnp.float32
bf16 = jnp.bfloat16

D_MODEL = 1024
FFN_HIDDEN = 2816
RET_HEADS, RET_DK, RET_DV = 4, 128, 256
HG_HEADS, HG_DK, HG_DV = 8, 128, 128
LRU_W, LRU_BLOCKS, LRU_BD = 1024, 8, 128
CONV_W = 4
LRU_C = 8.0
N_BRANCH = 3
ROPE_BASE = 10000.0
PAST_LEN = 16384
LOG2_E = 1.4426950408889634
EPS = 1e-6
RET_COLS = 2 * RET_HEADS * RET_DK + 2 * RET_HEADS * RET_DV
HG_COLS = 4 * HG_HEADS * HG_DK
LRU_COLS = 2 * LRU_W + N_BRANCH * D_MODEL

V7X_VMEM_BYTES = 64 * 1024 * 1024
VMEM_LIMIT = 56 * 1024 * 1024
SUBLANES = 8

TOKEN_TILE = 512
FFN_CHUNK = 256
RET_CHUNK = 256
HG_CHUNK = 128
HG_LEVELS = ((128, 64), (64, 32), (32, 16), (16, 8), (8, 4), (4, 2))
LRU_ROWS = 128
SAMPLE_BB = 8


def _cparams(sem):
    return pltpu.CompilerParams(dimension_semantics=sem, vmem_limit_bytes=VMEM_LIMIT)


def _resident(shape):
    nd = len(shape)
    return pl.BlockSpec(shape, lambda *_: (0,) * nd, pipeline_mode=pl.Buffered(1))


def _rmsnorm(x, g):
    return x * lax.rsqrt(jnp.mean(x * x, axis=-1, keepdims=True) + EPS) * g


def _dot(a, b):
    return jnp.dot(a, b, preferred_element_type=f32)


def _dot_tb(a, b):
    return lax.dot_general(a, b, (((1,), (1,)), ((), ())), preferred_element_type=f32)


def _dot_ta(a, b):
    return lax.dot_general(a, b, (((0,), (0,)), ((), ())), preferred_element_type=f32)


def _silu(x):
    hx = 0.5 * x
    return hx + hx * jnp.tanh(hx)


def _softplus(x):
    return jnp.maximum(x, 0.0) + jnp.log1p(jnp.exp(-jnp.abs(x)))


def _block_diag(x, nb, rows_per):
    r, w = x.shape
    rb = lax.broadcasted_iota(jnp.int32, (r, nb * w), 0) // rows_per
    cb = lax.broadcasted_iota(jnp.int32, (r, nb * w), 1) // w
    return jnp.where(rb == cb, jnp.tile(x, (1, nb)), jnp.zeros((), x.dtype))


def _ffn_body(x_ref, g_ref, wg_ref, wu_ref, wd_ref, fg_ref, o_ref, *, final):
    x = x_ref[...]
    xn = _rmsnorm(x, g_ref[...]).astype(bf16)
    acc = jnp.zeros(x.shape, f32)
    for c in range(FFN_HIDDEN // FFN_CHUNK):
        sl = slice(c * FFN_CHUNK, (c + 1) * FFN_CHUNK)
        g = _dot(xn, wg_ref[:, sl])
        u = _dot(xn, wu_ref[:, sl])
        h = (_silu(g) * u).astype(bf16)
        acc = acc + _dot(h, wd_ref[sl, :])
    y = x + 0.5 * acc
    if final:
        y = _rmsnorm(y, fg_ref[...])
    o_ref[...] = y


def _ffn(x, g, wg, wu, wd, final_g=None):
    n = x.shape[0]
    tm = min(TOKEN_TILE, n)
    final = final_g is not None
    fg = final_g if final else g
    row = pl.BlockSpec((tm, D_MODEL), lambda i: (i, 0))
    return pl.pallas_call(
        functools.partial(_ffn_body, final=final),
        grid=(n // tm,),
        in_specs=[row, _resident((1, D_MODEL)), _resident(wg.shape), _resident(wu.shape),
                  _resident(wd.shape), _resident((1, D_MODEL))],
        out_specs=row,
        out_shape=jax.ShapeDtypeStruct((n, D_MODEL), f32),
        compiler_params=_cparams(("parallel",)),
        name="ffn",
    )(x, g.reshape(1, -1), wg, wu, wd, fg.reshape(1, -1))


def _proj_ret_body(x_ref, g_ref, w_ref, cos_ref, sin_ref, q_ref, k_ref, v_ref, gr_ref):
    xn = _rmsnorm(x_ref[...], g_ref[...]).astype(bf16)
    cos = cos_ref[...]
    sin = sin_ref[...]
    nqk = RET_HEADS * RET_DK

    def rot(z):
        return z * cos + pltpu.roll(z, RET_DK // 2, 1) * sin

    for c in range(nqk // 256):
        zq = _dot(xn, w_ref[:, c * 256:(c + 1) * 256])
        zk = _dot(xn, w_ref[:, nqk + c * 256:nqk + (c + 1) * 256])
        for j in range(2):
            sl = slice(c * 256 + j * 128, c * 256 + (j + 1) * 128)
            q_ref[:, sl] = rot(zq[:, j * 128:(j + 1) * 128]).astype(bf16)
            k_ref[:, sl] = (rot(zk[:, j * 128:(j + 1) * 128]) * (RET_DK ** -0.5)).astype(bf16)
    nv = RET_HEADS * RET_DV
    for c in range(nv // 256):
        sl = slice(c * 256, (c + 1) * 256)
        v_ref[:, sl] = _dot(xn, w_ref[:, 2 * nqk + c * 256:2 * nqk + (c + 1) * 256]).astype(bf16)
        zg = _dot(xn, w_ref[:, 2 * nqk + nv + c * 256:2 * nqk + nv + (c + 1) * 256])
        gr_ref[:, sl] = _silu(zg).astype(bf16)


def _proj_ret(x, g, w, cos, sin):
    n = x.shape[0]
    tm = min(TOKEN_TILE, n)
    row = lambda w_: pl.BlockSpec((tm, w_), lambda i: (i, 0))
    nqk, nv = RET_HEADS * RET_DK, RET_HEADS * RET_DV
    return pl.pallas_call(
        _proj_ret_body,
        grid=(n // tm,),
        in_specs=[row(D_MODEL), _resident((1, D_MODEL)), _resident(w.shape), row(RET_DK), row(RET_DK)],
        out_specs=[row(nqk), row(nqk), row(nv), row(nv)],
        out_shape=[jax.ShapeDtypeStruct((n, nqk), bf16), jax.ShapeDtypeStruct((n, nqk), bf16),
                   jax.ShapeDtypeStruct((n, nv), bf16), jax.ShapeDtypeStruct((n, nv), bf16)],
        compiler_params=_cparams(("parallel",)),
        name="proj_ret",
    )(x, g.reshape(1, -1), w, cos, sin)


def _proj_hgrn_body(x_ref, g_ref, w_ref, lbl_ref, q_ref, lf_ref, i_ref, gh_ref, *, layer):
    xn = _rmsnorm(x_ref[...], g_ref[...]).astype(bf16)
    hw = HG_HEADS * HG_DK
    if layer > 0:
        lg = lbl_ref[...]
        e = jnp.exp(lg - jnp.max(lg, axis=0, keepdims=True))
        sm = e / jnp.sum(e, axis=0, keepdims=True)
        lb = jnp.sum(sm[1:layer + 1], axis=0, keepdims=True)
    for c in range(hw // 256):
        sl = slice(c * 256, (c + 1) * 256)
        q_ref[:, sl] = _silu(_dot(xn, w_ref[:, sl])).astype(bf16)
        zf = _dot(xn, w_ref[:, hw + c * 256:hw + (c + 1) * 256])
        t = jnp.exp(-jnp.abs(zf))
        num = jnp.minimum(zf, 0.0)
        if layer > 0:
            lbc = lb[:, sl]
            num = jnp.maximum(jnp.log(jnp.where(zf < 0.0, lbc + t, 1.0 + lbc * t)), num)
        lf_ref[:, sl] = num - jnp.log1p(t)
        i_ref[:, sl] = _dot(xn, w_ref[:, 2 * hw + c * 256:2 * hw + (c + 1) * 256]).astype(bf16)
        gh_ref[:, sl] = _silu(_dot(xn, w_ref[:, 3 * hw + c * 256:3 * hw + (c + 1) * 256])).astype(bf16)


def _proj_hgrn(x, g, w, lb_logits, layer):
    n = x.shape[0]
    tm = min(TOKEN_TILE, n)
    hw = HG_HEADS * HG_DK
    row = lambda w_: pl.BlockSpec((tm, w_), lambda i: (i, 0))
    return pl.pallas_call(
        functools.partial(_proj_hgrn_body, layer=layer),
        grid=(n // tm,),
        in_specs=[row(D_MODEL), _resident((1, D_MODEL)), _resident(w.shape), _resident(lb_logits.shape)],
        out_specs=[row(hw), row(hw), row(hw), row(hw)],
        out_shape=[jax.ShapeDtypeStruct((n, hw), bf16), jax.ShapeDtypeStruct((n, hw), f32),
                   jax.ShapeDtypeStruct((n, hw), bf16), jax.ShapeDtypeStruct((n, hw), bf16)],
        compiler_params=_cparams(("parallel",)),
        name="proj_hgrn",
    )(x, g.reshape(1, -1), w, lb_logits)


def _merge_gates(xn, w_ref, mb_ref, gs_ref, chunks=range(N_BRANCH * D_MODEL // 256)):
    for c in chunks:
        sl = slice(c * 256, (c + 1) * 256)
        z = _dot(xn, w_ref[:, 2 * LRU_W + c * 256:2 * LRU_W + (c + 1) * 256])
        gs_ref[:, sl] = jax.nn.sigmoid(z + mb_ref[:, sl]).astype(bf16)


def _proj_lru_body(x_ref, g_ref, w_ref, mb_ref, xl_ref, gl_ref, gs_ref):
    xn = _rmsnorm(x_ref[...], g_ref[...]).astype(bf16)
    for c in range(LRU_W // 256):
        sl = slice(c * 256, (c + 1) * 256)
        xl_ref[:, sl] = _dot(xn, w_ref[:, sl])
        zg = _dot(xn, w_ref[:, LRU_W + c * 256:LRU_W + (c + 1) * 256])
        gl_ref[:, sl] = jax.nn.gelu(zg, approximate=True).astype(bf16)
    _merge_gates(xn, w_ref, mb_ref, gs_ref)


def _proj_lru(x, g, w, merge_bias):
    n = x.shape[0]
    tm = min(TOKEN_TILE, n)
    row = lambda w_: pl.BlockSpec((tm, w_), lambda i: (i, 0))
    ng = N_BRANCH * D_MODEL
    return pl.pallas_call(
        _proj_lru_body,
        grid=(n // tm,),
        in_specs=[row(D_MODEL), _resident((1, D_MODEL)), _resident(w.shape), _resident((1, ng))],
        out_specs=[row(LRU_W), row(LRU_W), row(ng)],
        out_shape=[jax.ShapeDtypeStruct((n, LRU_W), f32), jax.ShapeDtypeStruct((n, LRU_W), bf16),
                   jax.ShapeDtypeStruct((n, ng), bf16)],
        compiler_params=_cparams(("parallel",)),
        name="proj_lru",
    )(x, g.reshape(1, -1), w, merge_bias.reshape(1, -1))


def _mix_out_body(x_ref, or_ref, oh_ref, ol_ref, gs_ref, wr_ref, wh_ref, wl_ref, wo_ref, o_ref):
    d = D_MODEL
    merged = gs_ref[:, 0:d].astype(f32) * _dot(or_ref[...], wr_ref[...])
    merged = merged + gs_ref[:, d:2 * d].astype(f32) * _dot(oh_ref[...], wh_ref[...])
    merged = merged + gs_ref[:, 2 * d:3 * d].astype(f32) * _dot(ol_ref[...], wl_ref[...])
    o_ref[...] = x_ref[...] + _dot(merged.astype(bf16), wo_ref[...])


def _mix_out(x, o_r, o_h, o_l, gs, wr, wh, wl, wo):
    n = x.shape[0]
    tm = min(TOKEN_TILE, n)
    row = lambda w_: pl.BlockSpec((tm, w_), lambda i: (i, 0))
    return pl.pallas_call(
        _mix_out_body,
        grid=(n // tm,),
        in_specs=[row(D_MODEL), row(D_MODEL), row(D_MODEL), row(D_MODEL), row(N_BRANCH * D_MODEL),
                  _resident(wr.shape), _resident(wh.shape), _resident(wl.shape), _resident(wo.shape)],
        out_specs=row(D_MODEL),
        out_shape=jax.ShapeDtypeStruct((n, D_MODEL), f32),
        compiler_params=_cparams(("parallel",)),
        name="mix_out",
    )(x, o_r, o_h, o_l, gs, wr, wh, wl, wo)


def _ret_log_gamma():
    return jnp.log1p(-jnp.exp2(-5.0 - jnp.arange(RET_HEADS, dtype=f32)))


def _ret_tables(c):
    log_g = _ret_log_gamma()
    idx = jnp.arange(c, dtype=f32)
    diff = idx[:, None] - idx[None, :]
    causal = diff >= 0
    d_intra = jnp.where(causal, jnp.exp(log_g[:, None, None] * jnp.where(causal, diff, 0.0)), 0.0)
    q_dec = jnp.exp(log_g[:, None] * (idx + 1.0))[:, :, None]
    k_dec = jnp.exp(log_g[:, None] * (c - 1.0 - idx))[:, :, None]
    s_dec = jnp.exp(log_g * c)
    return d_intra, q_dec, k_dec, s_dec


def _head_rms_gate(o, gate):
    return o * lax.rsqrt(jnp.mean(o * o, axis=-1, keepdims=True) + EPS) * gate


def _ret_prompt_body(q_ref, k_ref, v_ref, g_ref, dm_ref, qd_ref, kd_ref, sd_ref, o_ref, st_ref, s_scr):
    c = pl.program_id(1)

    @pl.when(c == 0)
    def _():
        s_scr[...] = jnp.zeros(s_scr.shape, f32)

    for h in range(RET_HEADS):
        ks = slice(h * RET_DK, (h + 1) * RET_DK)
        vs = slice(h * RET_DV, (h + 1) * RET_DV)
        q = q_ref[:, ks]
        k = k_ref[:, ks]
        v = v_ref[:, vs]
        state = s_scr[h]
        scores = (_dot_tb(q, k) * dm_ref[h]).astype(bf16)
        o = _dot(scores, v) + qd_ref[h] * _dot(q, state.astype(bf16))
        kd = (k.astype(f32) * kd_ref[h]).astype(bf16)
        s_scr[h] = state * sd_ref[h] + _dot_ta(kd, v)
        o_ref[:, vs] = _head_rms_gate(o, g_ref[:, vs].astype(f32)).astype(bf16)

    @pl.when(c == pl.num_programs(1) - 1)
    def _():
        st_ref[0] = s_scr[...]


def _ret_prompt(q, k, v, g, bsz, t):
    c = RET_CHUNK
    nc = t // c
    nqk, nv = RET_HEADS * RET_DK, RET_HEADS * RET_DV
    d_intra, q_dec, k_dec, s_dec = _ret_tables(c)
    sd = jnp.broadcast_to(s_dec[:, None, None], (RET_HEADS, 1, RET_DV))
    row = lambda w_: pl.BlockSpec((c, w_), lambda b, i: (b * nc + i, 0))
    return pl.pallas_call(
        _ret_prompt_body,
        grid=(bsz, nc),
        in_specs=[row(nqk), row(nqk), row(nv), row(nv), _resident(d_intra.shape), _resident(q_dec.shape),
                  _resident(k_dec.shape), _resident(sd.shape)],
        out_specs=[row(nv), pl.BlockSpec((1, RET_HEADS, RET_DK, RET_DV), lambda b, i: (b, 0, 0, 0))],
        out_shape=[jax.ShapeDtypeStruct((bsz * t, nv), bf16),
                   jax.ShapeDtypeStruct((bsz, RET_HEADS, RET_DK, RET_DV), f32)],
        scratch_shapes=[pltpu.VMEM((RET_HEADS, RET_DK, RET_DV), f32)],
        compiler_params=_cparams(("parallel", "arbitrary")),
        name="ret_prompt",
    )(q, k, v, g, d_intra, q_dec, k_dec, sd)


def _hg_level_masks(c):
    t = np.arange(c)[:, None]
    s = np.arange(c)[None, :]
    ms = [((t // blk == s // blk) & (t % blk >= half) & (s % blk < half)) for blk, half in HG_LEVELS]
    return jnp.asarray(np.stack(ms).astype(np.float32))


def _boundary_rows(b, blk, half):
    c, w = b.shape
    if half >= SUBLANES:
        return jnp.concatenate(
            [jnp.broadcast_to(b[j * blk + half - 1:j * blk + half, :], (blk, w)) for j in range(c // blk)], axis=0)
    b3 = b.reshape(c // SUBLANES, SUBLANES, w)
    row8 = lax.broadcasted_iota(jnp.int32, (1, SUBLANES, 1), 1)
    out = b3[:, half - 1:half, :]
    for j in range(1, SUBLANES // blk):
        out = jnp.where(row8 >= j * blk, b3[:, j * blk + half - 1:j * blk + half, :], out)
    return jnp.broadcast_to(out, b3.shape).reshape(c, w)


def _hgrn_prompt_body(q_ref, lf_ref, i_ref, g_ref, nw_ref, tri_ref, mask_ref, o_ref, st_ref, s_scr):
    ci = pl.program_id(1)
    c = HG_CHUNK

    @pl.when(ci == 0)
    def _():
        s_scr[...] = jnp.zeros(s_scr.shape, f32)

    lf_all = lf_ref[...]
    hi = lf_all.astype(bf16)
    lo = (lf_all - hi.astype(f32)).astype(bf16)
    tri = tri_ref[...]
    b_all = _dot(tri, hi) + _dot(tri, lo)
    odd = lax.broadcasted_iota(jnp.int32, (c, 1), 0) % 2 == 1

    f_all = jnp.exp(lf_all)
    kk_all = 1.0 - f_all
    q_all = q_ref[...].astype(f32)
    a_mats = [jnp.zeros((c, c), f32) for _ in range(HG_HEADS)]
    for lev, (blk, half) in enumerate(HG_LEVELS):
        for h in range(HG_HEADS):
            sl = slice(h * HG_DK, (h + 1) * HG_DK)
            b = b_all[:, sl]
            e = jnp.exp2(jnp.abs(b - _boundary_rows(b, blk, half)) * -LOG2_E)
            p = _dot_tb((q_all[:, sl] * e).astype(bf16), (kk_all[:, sl] * e).astype(bf16))
            a_mats[h] = a_mats[h] + p * mask_ref[lev]
    a_mats = [a.astype(bf16) for a in a_mats]

    for h in range(HG_HEADS):
        sl = slice(h * HG_DK, (h + 1) * HG_DK)
        b = b_all[:, sl]
        q = q_all[:, sl]
        f = f_all[:, sl]
        kk = kk_all[:, sl]
        v16 = i_ref[:, sl]
        o = _dot(a_mats[h], v16)
        vf = v16.astype(f32)
        a0 = jnp.sum(q * kk, axis=-1, keepdims=True)
        a1 = jnp.where(odd, jnp.sum(q * f * pltpu.roll(kk, 1, 0), axis=-1, keepdims=True), 0.0)
        o = o + a0 * vf + a1 * pltpu.roll(vf, 1, 0)
        st = s_scr[h]
        o = o + _dot_tb((q * jnp.exp(b)).astype(bf16), st.astype(bf16))
        bl = b[c - 1:c, :]
        kdec = (kk * jnp.exp(bl - b)).astype(bf16)
        s_scr[h] = st * jnp.exp(bl) + _dot_ta(v16, kdec)
        o_ref[:, sl] = _head_rms_gate(o, nw_ref[:, sl] * g_ref[:, sl].astype(f32)).astype(bf16)

    @pl.when(ci == pl.num_programs(1) - 1)
    def _():
        for h in range(HG_HEADS):
            st_ref[0, h] = s_scr[h].T


def _hgrn_prompt(q, lf, i, g, norm_w, bsz, t):
    c = HG_CHUNK
    nc = t // c
    hw = HG_HEADS * HG_DK
    tri = jnp.asarray(np.tril(np.ones((c, c), np.float32))).astype(bf16)
    masks = _hg_level_masks(c)
    row = lambda: pl.BlockSpec((c, hw), lambda b, j: (b * nc + j, 0))
    return pl.pallas_call(
        _hgrn_prompt_body,
        grid=(bsz, nc),
        in_specs=[row(), row(), row(), row(), _resident((1, hw)), _resident(tri.shape), _resident(masks.shape)],
        out_specs=[row(), pl.BlockSpec((1, HG_HEADS, HG_DK, HG_DV), lambda b, j: (b, 0, 0, 0))],
        out_shape=[jax.ShapeDtypeStruct((bsz * t, hw), bf16),
                   jax.ShapeDtypeStruct((bsz, HG_HEADS, HG_DK, HG_DV), f32)],
        scratch_shapes=[pltpu.VMEM((HG_HEADS, HG_DV, HG_DK), f32)],
        compiler_params=_cparams(("parallel", "arbitrary")),
        name="hgrn_prompt",
    )(q, lf, i, g, norm_w.reshape(1, -1), tri, masks)


def _lru_gate_block(xc, wa, ba, wx, bx, neg_c_sp):
    xb = xc.astype(bf16)
    r = jax.nn.sigmoid(_dot(xb, wa) + ba)
    gi = jax.nn.sigmoid(_dot(xb, wx) + bx)
    log_a = neg_c_sp * r
    a = jnp.exp(log_a)
    return a, jnp.sqrt(-jnp.tanh(log_a) * (1.0 + a * a)) * (gi * xc)


def _lru_gates(xc, wa_ref, ba, wx_ref, bx, neg_c_sp):
    parts = []
    for n in range(LRU_BLOCKS):
        sl = slice(n * LRU_BD, (n + 1) * LRU_BD)
        parts.append(_lru_gate_block(xc[:, sl], wa_ref[n], ba[:, sl], wx_ref[n], bx[:, sl], neg_c_sp[:, sl]))
    return jnp.concatenate([p[0] for p in parts], axis=1), jnp.concatenate([p[1] for p in parts], axis=1)


def _lru_scan(a, u, h):
    rows, w = a.shape
    ng = rows // SUBLANES
    a3 = a.reshape(ng, SUBLANES, w)
    u3 = u.reshape(ng, SUBLANES, w)
    row8 = lax.broadcasted_iota(jnp.int32, (1, SUBLANES, 1), 1)
    step = 1
    while step < SUBLANES:
        keep = row8 >= step
        u3 = a3 * jnp.where(keep, pltpu.roll(u3, step, 1), 0.0) + u3
        a3 = a3 * jnp.where(keep, pltpu.roll(a3, step, 1), 1.0)
        step *= 2
    groups = []
    for j in range(ng):
        hj = a3[j] * h + u3[j]
        groups.append(hj)
        h = hj[SUBLANES - 1:SUBLANES, :]
    return jnp.concatenate(groups, axis=0), h


def _proj_lru_scan_body(x_ref, g_ref, w_ref, mb_ref, cw_ref, cb_ref, wa_ref, ba_ref, wx_ref, bx_ref, lam_ref,
                        o_ref, gs_ref, h_ref, cv_ref, xx_scr, hs_scr, h_scr):
    ti = pl.program_id(1)
    tm = x_ref.shape[0]
    top = SUBLANES

    @pl.when(ti == 0)
    def _():
        xx_scr[0:top, :] = jnp.zeros((top, LRU_W), f32)
        h_scr[...] = jnp.zeros(h_scr.shape, f32)

    xn = _rmsnorm(x_ref[...], g_ref[...]).astype(bf16)
    for c in range(LRU_W // 256):
        sl = slice(c * 256, (c + 1) * 256)
        xx_scr[top:top + tm, sl] = _dot(xn, w_ref[:, sl])
    cw = cw_ref[...]
    cb = cb_ref[...]
    ba = ba_ref[...]
    bx = bx_ref[...]
    neg_c_sp = -LRU_C * _softplus(-lam_ref[...])
    n_gate = N_BRANCH * D_MODEL // 256
    n_pair = LRU_W // 256
    for p in range(n_pair):
        for n in (2 * p, 2 * p + 1):
            sl = slice(n * LRU_BD, (n + 1) * LRU_BD)
            h = h_scr[0:1, sl]
            for r0 in range(0, tm, LRU_ROWS):
                xc = cb[:, sl] + xx_scr[top + r0:top + r0 + LRU_ROWS, sl] * cw[CONV_W - 1:CONV_W, sl]
                for j in range(CONV_W - 1):
                    off = top - (CONV_W - 1) + j + r0
                    xc = xc + xx_scr[off:off + LRU_ROWS, sl] * cw[j:j + 1, sl]
                a, u = _lru_gate_block(xc, wa_ref[n], ba[:, sl], wx_ref[n], bx[:, sl], neg_c_sp[:, sl])
                hs, h = _lru_scan(a, u, h)
                hs_scr[r0:r0 + LRU_ROWS, sl] = hs
            h_scr[:, sl] = jnp.broadcast_to(h, (h_scr.shape[0], LRU_BD))
        sl = slice(p * 256, (p + 1) * 256)
        zg = _dot(xn, w_ref[:, LRU_W + p * 256:LRU_W + (p + 1) * 256])
        o_ref[:, sl] = (hs_scr[:, sl] * jax.nn.gelu(zg, approximate=True)).astype(bf16)
        _merge_gates(xn, w_ref, mb_ref, gs_ref, range(p * n_gate // n_pair, (p + 1) * n_gate // n_pair))
    tail = xx_scr[tm:tm + top, :]
    xx_scr[0:top, :] = tail

    @pl.when(ti == pl.num_programs(1) - 1)
    def _():
        h_ref[0] = h_scr[0:1, :]
        cv_ref[0] = tail[top - (CONV_W - 1):top, :]


def _proj_lru_scan(x, g, w, merge_bias, cw, cb, wa, ba, wx, bx, lam, bsz, t):
    tm = min(TOKEN_TILE, t)
    nt = t // tm
    ng = N_BRANCH * D_MODEL
    row = lambda w_: pl.BlockSpec((tm, w_), lambda b, j: (b * nt + j, 0))
    vec = _resident((1, LRU_W))
    o, gs, h, cv = pl.pallas_call(
        _proj_lru_scan_body,
        grid=(bsz, nt),
        in_specs=[row(D_MODEL), _resident((1, D_MODEL)), _resident(w.shape), _resident((1, ng)),
                  _resident((CONV_W, LRU_W)), vec, _resident(wa.shape), vec, _resident(wx.shape), vec, vec],
        out_specs=[row(LRU_W), row(ng), pl.BlockSpec((1, 1, LRU_W), lambda b, j: (b, 0, 0)),
                   pl.BlockSpec((1, CONV_W - 1, LRU_W), lambda b, j: (b, 0, 0))],
        out_shape=[jax.ShapeDtypeStruct((bsz * t, LRU_W), bf16), jax.ShapeDtypeStruct((bsz * t, ng), bf16),
                   jax.ShapeDtypeStruct((bsz, 1, LRU_W), f32), jax.ShapeDtypeStruct((bsz, CONV_W - 1, LRU_W), f32)],
        scratch_shapes=[pltpu.VMEM((tm + SUBLANES, LRU_W), f32), pltpu.VMEM((tm, LRU_W), f32),
                        pltpu.VMEM((SUBLANES, LRU_W), f32)],
        compiler_params=_cparams(("parallel", "arbitrary")),
        name="proj_lru_scan",
    )(x, g.reshape(1, -1), w, merge_bias.reshape(1, -1), cw, cb.reshape(1, -1), wa, ba.reshape(1, -1), wx,
      bx.reshape(1, -1), lam.reshape(1, -1))
    return o, gs, h[:, 0], cv


def _ret_sample_tables(t):
    log_g = _ret_log_gamma()
    r = jnp.arange(SAMPLE_BB * t)
    tt = (r % t).astype(f32)
    same = (r[:, None] // t) == (r[None, :] // t)
    diff = tt[:, None] - tt[None, :]
    ok = same & (diff >= 0)
    dm = jnp.where(ok, jnp.exp(log_g[:, None, None] * jnp.where(ok, diff, 0.0)), 0.0)
    qd = jnp.exp(log_g[:, None] * (tt + 1.0))[:, :, None]
    kd = jnp.exp(log_g[:, None] * (t - 1.0 - tt))[:, :, None]
    sd = jnp.exp(log_g * t)
    return dm, qd, kd, sd


def _stacked_state(layer, depth, bsz, heads_shape):
    shape = jax.ShapeDtypeStruct((depth, bsz) + heads_shape, f32)
    if layer == 0:
        spec = pl.BlockSpec((depth, SAMPLE_BB) + heads_shape, lambda i: (0, i, 0, 0, 0))

        def put(so_ref, b, h, val):
            so_ref[0, b, h] = val

        def finish(so_ref):
            for l in range(1, depth):
                so_ref[l] = jnp.zeros((SAMPLE_BB,) + heads_shape, f32)
    else:
        spec = pl.BlockSpec((None, SAMPLE_BB) + heads_shape, lambda i: (layer, i, 0, 0, 0))

        def put(so_ref, b, h, val):
            so_ref[b, h] = val

        def finish(so_ref):
            pass
    return shape, spec, put, finish


def _ret_sample_body(q_ref, k_ref, v_ref, g_ref, s_ref, dm_ref, qd_ref, kd_ref, sd_ref, *rest, t, put, finish):
    o_ref, so_ref = rest[-2:]
    bb = SAMPLE_BB
    finish(so_ref)
    for h in range(RET_HEADS):
        ks = slice(h * RET_DK, (h + 1) * RET_DK)
        vs = slice(h * RET_DV, (h + 1) * RET_DV)
        q = q_ref[:, ks]
        k = k_ref[:, ks]
        v = v_ref[:, vs]
        state = s_ref[:, h]
        scores = (_dot_tb(q, k) * dm_ref[h]).astype(bf16)
        inter = _dot(_block_diag(q, bb, t), state.reshape(bb * RET_DK, RET_DV).astype(bf16))
        o = _dot(scores, v) + qd_ref[h] * inter
        kd = (k.astype(f32) * kd_ref[h]).astype(bf16)
        ds = _dot_ta(kd, _block_diag(v, bb, t))
        for b in range(bb):
            put(so_ref, b, h, state[b] * sd_ref[h] + ds[:, b * RET_DV:(b + 1) * RET_DV])
        o_ref[:, vs] = _head_rms_gate(o, g_ref[:, vs].astype(f32)).astype(bf16)


def _ret_sample(q, k, v, g, state, stack, layer, bsz, t):
    nqk, nv = RET_HEADS * RET_DK, RET_HEADS * RET_DV
    bb = SAMPLE_BB
    heads_shape = (RET_HEADS, RET_DK, RET_DV)
    dm, qd, kd, s_dec = _ret_sample_tables(t)
    sd = jnp.broadcast_to(s_dec[:, None, None], (RET_HEADS, 1, RET_DV))
    row = lambda w_: pl.BlockSpec((bb * t, w_), lambda i: (i, 0))
    st_shape, st_spec, put, finish = _stacked_state(layer, state.shape[0], bsz, heads_shape)
    args = [q, k, v, g, state, dm, qd, kd, sd]
    in_specs = [row(nqk), row(nqk), row(nv), row(nv),
                pl.BlockSpec((None, bb) + heads_shape, lambda i: (layer, i, 0, 0, 0)),
                _resident(dm.shape), _resident(qd.shape), _resident(kd.shape), _resident(sd.shape)]
    aliases = {}
    if stack is not None:
        aliases = {len(args): 1}
        args.append(stack)
        in_specs.append(pl.BlockSpec(memory_space=pl.ANY))
    return pl.pallas_call(
        functools.partial(_ret_sample_body, t=t, put=put, finish=finish),
        grid=(bsz // bb,),
        in_specs=in_specs,
        out_specs=[row(nv), st_spec],
        out_shape=[jax.ShapeDtypeStruct((bsz * t, nv), bf16), st_shape],
        input_output_aliases=aliases,
        compiler_params=_cparams(("parallel",)),
        name="ret_sample",
    )(*args)


def _hgrn_sample_body(q_ref, lf_ref, i_ref, g_ref, nw_ref, s_ref, *rest, t, put, finish):
    assert t >= 2
    o_ref, so_ref = rest[-2:]
    finish(so_ref)
    bb = SAMPLE_BB
    n = bb * t
    tt = lax.broadcasted_iota(jnp.int32, (n, 1), 0) % t
    f = jnp.exp(lf_ref[...])
    kk = 1.0 - f
    q = q_ref[...].astype(f32)
    v = i_ref[...].astype(f32)
    before = [jnp.where(tt >= d, pltpu.roll(f, d, 0), 1.0) for d in range(1, t)]
    after = [jnp.where(tt < t - d, pltpu.roll(f, n - d, 0), 1.0) for d in range(1, t)]
    dec = [None, f]
    for d in range(2, t):
        dec.append(dec[-1] * before[d - 2])
    pre = dec[t - 1] * before[t - 2]
    suf = after[0]
    for d in range(1, t - 1):
        suf = suf * after[d]
    total = pre * suf
    qp = (q * pre).astype(bf16)
    kp = (kk * suf).astype(bf16)
    k_back = [kk] + [pltpu.roll(kk, d, 0) for d in range(1, t)]
    v_back = [v] + [pltpu.roll(v, d, 0) for d in range(1, t)]
    for h in range(HG_HEADS):
        sl = slice(h * HG_DK, (h + 1) * HG_DK)
        o = jnp.zeros((n, HG_DV), f32)
        for d in range(t):
            p = q[:, sl] * k_back[d][:, sl]
            if d > 0:
                p = p * dec[d][:, sl]
            a = jnp.sum(p, axis=-1, keepdims=True)
            if d > 0:
                a = jnp.where(tt >= d, a, 0.0)
            o = o + a * v_back[d][:, sl]
        state = s_ref[:, h]
        o = o + _dot(_block_diag(qp[:, sl], bb, t), state.reshape(bb * HG_DK, HG_DV).astype(bf16))
        ds = _dot_ta(kp[:, sl], _block_diag(i_ref[:, sl], bb, t))
        total_t = total[:, sl].T
        for b in range(bb):
            put(so_ref, b, h, state[b] * total_t[:, b * t:b * t + 1] + ds[:, b * HG_DV:(b + 1) * HG_DV])
        o_ref[:, sl] = _head_rms_gate(o, nw_ref[:, sl] * g_ref[:, sl].astype(f32)).astype(bf16)


def _hgrn_sample(q, lf, i, g, norm_w, state, stack, layer, bsz, t):
    hw = HG_HEADS * HG_DK
    bb = SAMPLE_BB
    heads_shape = (HG_HEADS, HG_DK, HG_DV)
    row = lambda: pl.BlockSpec((bb * t, hw), lambda j: (j, 0))
    st_shape, st_spec, put, finish = _stacked_state(layer, state.shape[0], bsz, heads_shape)
    args = [q, lf, i, g, norm_w.reshape(1, -1), state]
    in_specs = [row(), row(), row(), row(), _resident((1, hw)),
                pl.BlockSpec((None, bb) + heads_shape, lambda j: (layer, j, 0, 0, 0))]
    aliases = {}
    if stack is not None:
        aliases = {len(args): 1}
        args.append(stack)
        in_specs.append(pl.BlockSpec(memory_space=pl.ANY))
    return pl.pallas_call(
        functools.partial(_hgrn_sample_body, t=t, put=put, finish=finish),
        grid=(bsz // bb,),
        in_specs=in_specs,
        out_specs=[row(), st_spec],
        out_shape=[jax.ShapeDtypeStruct((bsz * t, hw), bf16), st_shape],
        input_output_aliases=aliases,
        compiler_params=_cparams(("parallel",)),
        name="hgrn_sample",
    )(*args)


def _lru_sample_body(x_ref, gl_ref, cs_ref, h0_ref, cw_ref, cb_ref, wa_ref, ba_ref, wx_ref, bx_ref, lam_ref,
                     o_ref, h_ref, cv_ref, *, t):
    xx = [cs_ref[j] for j in range(CONV_W - 1)] + [x_ref[j] for j in range(t)]
    cw = cw_ref[...]
    neg_c_sp = -LRU_C * _softplus(-lam_ref[...])
    h = h0_ref[...]
    for j in range(t):
        xc = cb_ref[...]
        for i in range(CONV_W):
            xc = xc + xx[j + i] * cw[i:i + 1, :]
        a, u = _lru_gates(xc, wa_ref, ba_ref[...], wx_ref, bx_ref[...], neg_c_sp)
        h = a * h + u
        o_ref[j] = (h * gl_ref[j].astype(f32)).astype(bf16)
    h_ref[...] = h
    for j in range(CONV_W - 1):
        cv_ref[j] = xx[t + j]


def _lru_sample(xl, gl, cs, h0, cw, cb, wa, ba, wx, bx, lam, bsz, t):
    full = lambda a: pl.BlockSpec(a.shape, lambda i: (0,) * a.ndim)
    args = (xl, gl, cs, h0, cw, cb.reshape(1, -1), wa, ba.reshape(1, -1), wx, bx.reshape(1, -1), lam.reshape(1, -1))
    return pl.pallas_call(
        functools.partial(_lru_sample_body, t=t),
        grid=(1,),
        in_specs=[full(a) for a in args],
        out_specs=[pl.BlockSpec((t, bsz, LRU_W), lambda i: (0, 0, 0)), pl.BlockSpec((bsz, LRU_W), lambda i: (0, 0)),
                   pl.BlockSpec((CONV_W - 1, bsz, LRU_W), lambda i: (0, 0, 0))],
        out_shape=[jax.ShapeDtypeStruct((t, bsz, LRU_W), bf16), jax.ShapeDtypeStruct((bsz, LRU_W), f32),
                   jax.ShapeDtypeStruct((CONV_W - 1, bsz, LRU_W), f32)],
        compiler_params=_cparams(("arbitrary",)),
        name="lru_sample",
    )(*args)


def _rope_tables(pos):
    inv = ROPE_BASE ** (-jnp.arange(0, RET_DK, 2, dtype=f32) / RET_DK)
    ang = pos.astype(f32)[:, None] * inv[None, :]
    cos, sin = jnp.cos(ang), jnp.sin(ang)
    return jnp.concatenate([cos, cos], axis=1), jnp.concatenate([-sin, sin], axis=1)


def _trunk(x, pos, w, states, bsz, t):
    cos, sin = _rope_tables(pos)
    cos = jnp.tile(cos, (bsz, 1))
    sin = jnp.tile(sin, (bsz, 1))
    news = ([], [], [], [])
    ret_stack = hg_stack = None
    depth = w["w_ret"].shape[0]
    for l in range(depth):
        x = _ffn(x, w["ffn1_norm"][l], w["ffn1_wg"][l], w["ffn1_wu"][l], w["ffn1_wd"][l])
        q_r, k_r, v_r, g_r = _proj_ret(x, w["mix_norm"][l], w["w_ret"][l], cos, sin)
        q_h, lf, i_h, g_h = _proj_hgrn(x, w["mix_norm"][l], w["w_hg"][l], w["hgrn_lb_logits"], l)
        lru_w = (w["conv_w"][l], w["conv_b"][l], w["lru_w_a"][l], w["lru_b_a"][l], w["lru_w_x"][l],
                 w["lru_b_x"][l], w["lru_lambda"][l])
        if states is None:
            o_r, ret_new = _ret_prompt(q_r, k_r, v_r, g_r, bsz, t)
            o_h, hg_new = _hgrn_prompt(q_h, lf, i_h, g_h, w["hgrn_norm"][l], bsz, t)
            o_l, gs, h_new, cv_new = _proj_lru_scan(x, w["mix_norm"][l], w["w_lru"][l], w["merge_bias"][l],
                                                    *lru_w, bsz, t)
            news[0].append(ret_new)
            news[1].append(hg_new)
        else:
            st_ret, st_hg, st_lru, st_conv = states
            o_r, ret_stack = _ret_sample(q_r, k_r, v_r, g_r, st_ret, ret_stack, l, bsz, t)
            o_h, hg_stack = _hgrn_sample(q_h, lf, i_h, g_h, w["hgrn_norm"][l], st_hg, hg_stack, l, bsz, t)
            x_l, g_l, gs = _proj_lru(x, w["mix_norm"][l], w["w_lru"][l], w["merge_bias"][l])
            tm = lambda a: a.reshape(bsz, t, LRU_W).transpose(1, 0, 2)
            o_lt, h_new, cv_t = _lru_sample(tm(x_l), tm(g_l), st_conv[l].transpose(1, 0, 2), st_lru[l], *lru_w, bsz, t)
            o_l = o_lt.transpose(1, 0, 2).reshape(bsz * t, LRU_W)
            cv_new = cv_t.transpose(1, 0, 2)
        x = _mix_out(x, o_r, o_h, o_l, gs, w["w_ret_o"][l], w["w_hgrn_o"][l], w["w_lru_o"][l], w["w_mix_out"][l])
        last = l == depth - 1
        x = _ffn(x, w["ffn2_norm"][l], w["ffn2_wg"][l], w["ffn2_wu"][l], w["ffn2_wd"][l],
                 final_g=w["final_norm"] if last else None)
        news[2].append(h_new)
        news[3].append(cv_new)
    if states is None:
        return x, [jnp.stack(a, axis=0) for a in news]
    return x, [ret_stack, hg_stack, jnp.stack(news[2], axis=0), jnp.stack(news[3], axis=0)]


def kernel(x_prompt, x_sample, state_ret, state_hgrn, state_lru, state_conv, ffn1_norm, ffn1_w_gate, ffn1_w_up, ffn1_w_down, mix_norm, w_in, merge_bias, w_ret_o, hgrn_lb_logits, hgrn_norm, w_hgrn_o, conv_w, conv_b, lru_w_a, lru_b_a, lru_w_x, lru_b_x, lru_lambda, w_lru_o, w_mix_out, ffn2_norm, ffn2_w_gate, ffn2_w_up, ffn2_w_down, final_norm):
    c16 = lambda a: a.astype(bf16)
    w = dict(
        ffn1_norm=ffn1_norm, ffn1_wg=c16(ffn1_w_gate), ffn1_wu=c16(ffn1_w_up), ffn1_wd=c16(ffn1_w_down),
        mix_norm=mix_norm,
        w_ret=c16(w_in[:, :, :RET_COLS]), w_hg=c16(w_in[:, :, RET_COLS:RET_COLS + HG_COLS]),
        w_lru=c16(w_in[:, :, RET_COLS + HG_COLS:]),
        merge_bias=merge_bias, w_ret_o=c16(w_ret_o), hgrn_lb_logits=hgrn_lb_logits, hgrn_norm=hgrn_norm,
        w_hgrn_o=c16(w_hgrn_o), conv_w=conv_w, conv_b=conv_b, lru_w_a=c16(lru_w_a), lru_b_a=lru_b_a,
        lru_w_x=c16(lru_w_x), lru_b_x=lru_b_x, lru_lambda=lru_lambda, w_lru_o=c16(w_lru_o),
        w_mix_out=c16(w_mix_out), ffn2_norm=ffn2_norm, ffn2_wg=c16(ffn2_w_gate), ffn2_wu=c16(ffn2_w_up),
        ffn2_wd=c16(ffn2_w_down), final_norm=final_norm,
    )
    bp, tp, _ = x_prompt.shape
    bs, ts, _ = x_sample.shape
    y_p, st_p = _trunk(x_prompt.reshape(bp * tp, D_MODEL), jnp.arange(tp), w, None, bp, tp)
    y_s, st_s = _trunk(x_sample.reshape(bs * ts, D_MODEL), PAST_LEN + jnp.arange(ts), w,
                       (state_ret, state_hgrn, state_lru, state_conv), bs, ts)
    return (y_p.reshape(bp, tp, D_MODEL), y_s.reshape(bs, ts, D_MODEL),
            st_p[0], st_s[0], st_p[1], st_s[1], st_p[2], st_s[2], st_p[3], st_s[3])
```

```python
import functools

import jax
import jax.numpy as jnp
import numpy as np
from jax import lax
from jax.experimental import pallas as pl
from jax.experimental.pallas import tpu as pltpu

f32 = jnp.float32
bf16 = jnp.bfloat16

D_MODEL = 1024
FFN_HIDDEN = 2816
RET_HEADS, RET_DK, RET_DV = 4, 128, 256
HG_HEADS, HG_DK, HG_DV = 8, 128, 128
LRU_W, LRU_BLOCKS, LRU_BD = 1024, 8, 128
CONV_W = 4
LRU_C = 8.0
N_BRANCH = 3
ROPE_BASE = 10000.0
PAST_LEN = 16384
LOG2_E = 1.4426950408889634
EPS = 1e-6
RET_COLS = 2 * RET_HEADS * RET_DK + 2 * RET_HEADS * RET_DV
HG_COLS = 4 * HG_HEADS * HG_DK
LRU_COLS = 2 * LRU_W + N_BRANCH * D_MODEL

V7X_VMEM_BYTES = 64 * 1024 * 1024
VMEM_LIMIT = 56 * 1024 * 1024
SUBLANES = 8

TOKEN_TILE = 1024
LRU_TILE = 512
FFN_CHUNK = 256
RET_CHUNK = 256
HG_CHUNK = 128
HG_LEVELS = ((128, 64), (64, 32), (32, 16), (16, 8), (8, 4), (4, 2))
SAMPLE_BB = 8


def _cparams(sem):
    return pltpu.CompilerParams(dimension_semantics=sem, vmem_limit_bytes=VMEM_LIMIT)


def _resident(shape):
    nd = len(shape)
    return pl.BlockSpec(shape, lambda *_: (0,) * nd, pipeline_mode=pl.Buffered(1))


def _rmsnorm(x, g):
    return x * lax.rsqrt(jnp.mean(x * x, axis=-1, keepdims=True) + EPS) * g


def _dot(a, b):
    return jnp.dot(a, b, preferred_element_type=f32)


def _dot_tb(a, b):
    return lax.dot_general(a, b, (((1,), (1,)), ((), ())), preferred_element_type=f32)


def _dot_ta(a, b):
    return lax.dot_general(a, b, (((0,), (0,)), ((), ())), preferred_element_type=f32)


def _silu(x):
    hx = 0.5 * x
    return hx + hx * jnp.tanh(hx)


def _softplus(x):
    return jnp.maximum(x, 0.0) + jnp.log1p(jnp.exp(-jnp.abs(x)))


def _block_diag(x, nb, rows_per):
    r, w = x.shape
    rb = lax.broadcasted_iota(jnp.int32, (r, nb * w), 0) // rows_per
    cb = lax.broadcasted_iota(jnp.int32, (r, nb * w), 1) // w
    return jnp.where(rb == cb, jnp.tile(x, (1, nb)), jnp.zeros((), x.dtype))


def _ffn_body(x_ref, g_ref, wg_ref, wu_ref, wd_ref, fg_ref, o_ref, *, final):
    x = x_ref[...]
    xn = _rmsnorm(x, g_ref[...]).astype(bf16)
    acc = jnp.zeros(x.shape, f32)
    for c in range(FFN_HIDDEN // FFN_CHUNK):
        sl = slice(c * FFN_CHUNK, (c + 1) * FFN_CHUNK)
        g = _dot(xn, wg_ref[:, sl])
        u = _dot(xn, wu_ref[:, sl])
        h = (_silu(g) * u).astype(bf16)
        acc = acc + _dot(h, wd_ref[sl, :])
    y = x + 0.5 * acc
    if final:
        y = _rmsnorm(y, fg_ref[...])
    o_ref[...] = y


def _ffn(x, g, wg, wu, wd, final_g=None):
    n = x.shape[0]
    tm = min(TOKEN_TILE, n)
    final = final_g is not None
    fg = final_g if final else g
    row = pl.BlockSpec((tm, D_MODEL), lambda i: (i, 0))
    return pl.pallas_call(
        functools.partial(_ffn_body, final=final),
        grid=(n // tm,),
        in_specs=[row, _resident((1, D_MODEL)), _resident(wg.shape), _resident(wu.shape),
                  _resident(wd.shape), _resident((1, D_MODEL))],
        out_specs=row,
        out_shape=jax.ShapeDtypeStruct((n, D_MODEL), f32),
        compiler_params=_cparams(("parallel",)),
        name="ffn",
    )(x, g.reshape(1, -1), wg, wu, wd, fg.reshape(1, -1))


def _proj_ret_body(x_ref, g_ref, w_ref, cos_ref, sin_ref, q_ref, k_ref, v_ref, gr_ref):
    xn = _rmsnorm(x_ref[...], g_ref[...]).astype(bf16)
    cos = cos_ref[...]
    sin = sin_ref[...]
    nqk = RET_HEADS * RET_DK

    def rot(z):
        return z * cos + pltpu.roll(z, RET_DK // 2, 1) * sin

    for c in range(nqk // 256):
        zq = _dot(xn, w_ref[:, c * 256:(c + 1) * 256])
        zk = _dot(xn, w_ref[:, nqk + c * 256:nqk + (c + 1) * 256])
        for j in range(2):
            sl = slice(c * 256 + j * 128, c * 256 + (j + 1) * 128)
            q_ref[:, sl] = rot(zq[:, j * 128:(j + 1) * 128]).astype(bf16)
            k_ref[:, sl] = (rot(zk[:, j * 128:(j + 1) * 128]) * (RET_DK ** -0.5)).astype(bf16)
    nv = RET_HEADS * RET_DV
    for c in range(nv // 256):
        sl = slice(c * 256, (c + 1) * 256)
        v_ref[:, sl] = _dot(xn, w_ref[:, 2 * nqk + c * 256:2 * nqk + (c + 1) * 256]).astype(bf16)
        zg = _dot(xn, w_ref[:, 2 * nqk + nv + c * 256:2 * nqk + nv + (c + 1) * 256])
        gr_ref[:, sl] = _silu(zg).astype(bf16)


def _proj_ret(x, g, w, cos, sin):
    n = x.shape[0]
    tm = min(TOKEN_TILE, n)
    row = lambda w_: pl.BlockSpec((tm, w_), lambda i: (i, 0))
    nqk, nv = RET_HEADS * RET_DK, RET_HEADS * RET_DV
    return pl.pallas_call(
        _proj_ret_body,
        grid=(n // tm,),
        in_specs=[row(D_MODEL), _resident((1, D_MODEL)), _resident(w.shape), row(RET_DK), row(RET_DK)],
        out_specs=[row(nqk), row(nqk), row(nv), row(nv)],
        out_shape=[jax.ShapeDtypeStruct((n, nqk), bf16), jax.ShapeDtypeStruct((n, nqk), bf16),
                   jax.ShapeDtypeStruct((n, nv), bf16), jax.ShapeDtypeStruct((n, nv), bf16)],
        compiler_params=_cparams(("parallel",)),
        name="proj_ret",
    )(x, g.reshape(1, -1), w, cos, sin)


def _proj_hgrn_body(x_ref, g_ref, w_ref, lbl_ref, q_ref, lf_ref, i_ref, gh_ref, *, layer):
    xn = _rmsnorm(x_ref[...], g_ref[...]).astype(bf16)
    hw = HG_HEADS * HG_DK
    if layer > 0:
        lg = lbl_ref[...]
        e = jnp.exp(lg - jnp.max(lg, axis=0, keepdims=True))
        sm = e / jnp.sum(e, axis=0, keepdims=True)
        lb = jnp.sum(sm[1:layer + 1], axis=0, keepdims=True)
    for c in range(hw // 256):
        sl = slice(c * 256, (c + 1) * 256)
        q_ref[:, sl] = _silu(_dot(xn, w_ref[:, sl])).astype(bf16)
        zf = _dot(xn, w_ref[:, hw + c * 256:hw + (c + 1) * 256])
        t = jnp.exp2(jnp.abs(zf) * -LOG2_E)
        num = jnp.minimum(zf, 0.0)
        if layer > 0:
            lbc = lb[:, sl]
            num = jnp.maximum(jnp.log(jnp.where(zf < 0.0, lbc + t, 1.0 + lbc * t)), num)
        lf_ref[:, sl] = num - jnp.log(1.0 + t)
        i_ref[:, sl] = _dot(xn, w_ref[:, 2 * hw + c * 256:2 * hw + (c + 1) * 256]).astype(bf16)
        gh_ref[:, sl] = _silu(_dot(xn, w_ref[:, 3 * hw + c * 256:3 * hw + (c + 1) * 256])).astype(bf16)


def _proj_hgrn(x, g, w, lb_logits, layer):
    n = x.shape[0]
    tm = min(TOKEN_TILE, n)
    hw = HG_HEADS * HG_DK
    row = lambda w_: pl.BlockSpec((tm, w_), lambda i: (i, 0))
    return pl.pallas_call(
        functools.partial(_proj_hgrn_body, layer=layer),
        grid=(n // tm,),
        in_specs=[row(D_MODEL), _resident((1, D_MODEL)), _resident(w.shape), _resident(lb_logits.shape)],
        out_specs=[row(hw), row(hw), row(hw), row(hw)],
        out_shape=[jax.ShapeDtypeStruct((n, hw), bf16), jax.ShapeDtypeStruct((n, hw), f32),
                   jax.ShapeDtypeStruct((n, hw), bf16), jax.ShapeDtypeStruct((n, hw), bf16)],
        compiler_params=_cparams(("parallel",)),
        name="proj_hgrn",
    )(x, g.reshape(1, -1), w, lb_logits)


def _merge_gates(xn, w_ref, mb_ref, gs_ref, chunks=range(N_BRANCH * D_MODEL // 256)):
    for c in chunks:
        sl = slice(c * 256, (c + 1) * 256)
        z = _dot(xn, w_ref[:, 2 * LRU_W + c * 256:2 * LRU_W + (c + 1) * 256])
        gs_ref[:, sl] = jax.nn.sigmoid(z + mb_ref[:, sl]).astype(bf16)


def _proj_lru_body(x_ref, g_ref, w_ref, mb_ref, xl_ref, gl_ref, gs_ref):
    xn = _rmsnorm(x_ref[...], g_ref[...]).astype(bf16)
    for c in range(LRU_W // 256):
        sl = slice(c * 256, (c + 1) * 256)
        xl_ref[:, sl] = _dot(xn, w_ref[:, sl])
        zg = _dot(xn, w_ref[:, LRU_W + c * 256:LRU_W + (c + 1) * 256])
        gl_ref[:, sl] = jax.nn.gelu(zg, approximate=True).astype(bf16)
    _merge_gates(xn, w_ref, mb_ref, gs_ref)


def _proj_lru(x, g, w, merge_bias):
    n = x.shape[0]
    tm = min(TOKEN_TILE, n)
    row = lambda w_: pl.BlockSpec((tm, w_), lambda i: (i, 0))
    ng = N_BRANCH * D_MODEL
    return pl.pallas_call(
        _proj_lru_body,
        grid=(n // tm,),
        in_specs=[row(D_MODEL), _resident((1, D_MODEL)), _resident(w.shape), _resident((1, ng))],
        out_specs=[row(LRU_W), row(LRU_W), row(ng)],
        out_shape=[jax.ShapeDtypeStruct((n, LRU_W), f32), jax.ShapeDtypeStruct((n, LRU_W), bf16),
                   jax.ShapeDtypeStruct((n, ng), bf16)],
        compiler_params=_cparams(("parallel",)),
        name="proj_lru",
    )(x, g.reshape(1, -1), w, merge_bias.reshape(1, -1))


def _mix_out_body(x_ref, or_ref, oh_ref, ol_ref, gs_ref, wr_ref, wh_ref, wl_ref, wo_ref, o_ref):
    d = D_MODEL
    merged = gs_ref[:, 0:d].astype(f32) * _dot(or_ref[...], wr_ref[...])
    merged = merged + gs_ref[:, d:2 * d].astype(f32) * _dot(oh_ref[...], wh_ref[...])
    merged = merged + gs_ref[:, 2 * d:3 * d].astype(f32) * _dot(ol_ref[...], wl_ref[...])
    o_ref[...] = x_ref[...] + _dot(merged.astype(bf16), wo_ref[...])


def _mix_out(x, o_r, o_h, o_l, gs, wr, wh, wl, wo):
    n = x.shape[0]
    tm = min(TOKEN_TILE, n)
    row = lambda w_: pl.BlockSpec((tm, w_), lambda i: (i, 0))
    return pl.pallas_call(
        _mix_out_body,
        grid=(n // tm,),
        in_specs=[row(D_MODEL), row(D_MODEL), row(D_MODEL), row(D_MODEL), row(N_BRANCH * D_MODEL),
                  _resident(wr.shape), _resident(wh.shape), _resident(wl.shape), _resident(wo.shape)],
        out_specs=row(D_MODEL),
        out_shape=jax.ShapeDtypeStruct((n, D_MODEL), f32),
        compiler_params=_cparams(("parallel",)),
        name="mix_out",
    )(x, o_r, o_h, o_l, gs, wr, wh, wl, wo)


def _ret_log_gamma():
    return jnp.log1p(-jnp.exp2(-5.0 - jnp.arange(RET_HEADS, dtype=f32)))


def _ret_tables(c):
    log_g = _ret_log_gamma()
    idx = jnp.arange(c, dtype=f32)
    diff = idx[:, None] - idx[None, :]
    causal = diff >= 0
    d_intra = jnp.where(causal, jnp.exp(log_g[:, None, None] * jnp.where(causal, diff, 0.0)), 0.0)
    q_dec = jnp.exp(log_g[:, None] * (idx + 1.0))[:, :, None]
    k_dec = jnp.exp(log_g[:, None] * (c - 1.0 - idx))[:, :, None]
    s_dec = jnp.exp(log_g * c)
    return d_intra, q_dec, k_dec, s_dec


def _head_rms_gate(o, gate):
    return o * lax.rsqrt(jnp.mean(o * o, axis=-1, keepdims=True) + EPS) * gate


def _ret_prompt_body(q_ref, k_ref, v_ref, g_ref, dm_ref, qd_ref, kd_ref, sd_ref, o_ref, st_ref, s_scr):
    c = pl.program_id(1)

    @pl.when(c == 0)
    def _():
        s_scr[...] = jnp.zeros(s_scr.shape, f32)

    heads = range(RET_HEADS)
    ks = [slice(h * RET_DK, (h + 1) * RET_DK) for h in heads]
    vs = [slice(h * RET_DV, (h + 1) * RET_DV) for h in heads]
    scores = [(_dot_tb(q_ref[:, ks[h]], k_ref[:, ks[h]]) * dm_ref[h]).astype(bf16) for h in heads]
    inter = [qd_ref[h] * _dot(q_ref[:, ks[h]], s_scr[h].astype(bf16)) for h in heads]
    outs = [_dot(scores[h], v_ref[:, vs[h]]) + inter[h] for h in heads]
    for h in heads:
        kd = (k_ref[:, ks[h]].astype(f32) * kd_ref[h]).astype(bf16)
        s_scr[h] = s_scr[h] * sd_ref[h] + _dot_ta(kd, v_ref[:, vs[h]])
    for h in heads:
        o_ref[:, vs[h]] = _head_rms_gate(outs[h], g_ref[:, vs[h]].astype(f32)).astype(bf16)

    @pl.when(c == pl.num_programs(1) - 1)
    def _():
        st_ref[0] = s_scr[...]


def _ret_prompt(q, k, v, g, bsz, t):
    c = RET_CHUNK
    nc = t // c
    nqk, nv = RET_HEADS * RET_DK, RET_HEADS * RET_DV
    d_intra, q_dec, k_dec, s_dec = _ret_tables(c)
    sd = jnp.broadcast_to(s_dec[:, None, None], (RET_HEADS, 1, RET_DV))
    row = lambda w_: pl.BlockSpec((c, w_), lambda b, i: (b * nc + i, 0))
    return pl.pallas_call(
        _ret_prompt_body,
        grid=(bsz, nc),
        in_specs=[row(nqk), row(nqk), row(nv), row(nv), _resident(d_intra.shape), _resident(q_dec.shape),
                  _resident(k_dec.shape), _resident(sd.shape)],
        out_specs=[row(nv), pl.BlockSpec((1, RET_HEADS, RET_DK, RET_DV), lambda b, i: (b, 0, 0, 0))],
        out_shape=[jax.ShapeDtypeStruct((bsz * t, nv), bf16),
                   jax.ShapeDtypeStruct((bsz, RET_HEADS, RET_DK, RET_DV), f32)],
        scratch_shapes=[pltpu.VMEM((RET_HEADS, RET_DK, RET_DV), f32)],
        compiler_params=_cparams(("parallel", "arbitrary")),
        name="ret_prompt",
    )(q, k, v, g, d_intra, q_dec, k_dec, sd)


def _hg_level_masks(c):
    t = np.arange(c)[:, None]
    s = np.arange(c)[None, :]
    ms = [((t // blk == s // blk) & (t % blk >= half) & (s % blk < half)) for blk, half in HG_LEVELS]
    return jnp.asarray(np.stack(ms).astype(np.float32))


def _boundary_rows(b, blk, half):
    c, w = b.shape
    if half >= SUBLANES:
        return jnp.concatenate(
            [jnp.broadcast_to(b[j * blk + half - 1:j * blk + half, :], (blk, w)) for j in range(c // blk)], axis=0)
    b3 = b.reshape(c // SUBLANES, SUBLANES, w)
    row8 = lax.broadcasted_iota(jnp.int32, (1, SUBLANES, 1), 1)
    out = b3[:, half - 1:half, :]
    for j in range(1, SUBLANES // blk):
        out = jnp.where(row8 >= j * blk, b3[:, j * blk + half - 1:j * blk + half, :], out)
    return jnp.broadcast_to(out, b3.shape).reshape(c, w)


def _hgrn_prompt_body(q_ref, lf_ref, i_ref, g_ref, nw_ref, tri_ref, mask_ref, o_ref, st_ref, s_scr):
    ci = pl.program_id(1)
    c = HG_CHUNK

    @pl.when(ci == 0)
    def _():
        s_scr[...] = jnp.zeros(s_scr.shape, f32)

    lf_all = lf_ref[...]
    hi = lf_all.astype(bf16)
    lo = (lf_all - hi.astype(f32)).astype(bf16)
    tri = tri_ref[...]
    b_all = _dot(tri, hi) + _dot(tri, lo)
    odd = lax.broadcasted_iota(jnp.int32, (c, 1), 0) % 2 == 1

    f_all = jnp.exp(lf_all)
    kk_all = 1.0 - f_all
    q_all = q_ref[...].astype(f32)
    a_mats = [jnp.zeros((c, c), f32) for _ in range(HG_HEADS)]
    for lev, (blk, half) in enumerate(HG_LEVELS):
        for h in range(HG_HEADS):
            sl = slice(h * HG_DK, (h + 1) * HG_DK)
            b = b_all[:, sl]
            e = jnp.exp2(jnp.abs(b - _boundary_rows(b, blk, half)) * -LOG2_E)
            p = _dot_tb((q_all[:, sl] * e).astype(bf16), (kk_all[:, sl] * e).astype(bf16))
            a_mats[h] = a_mats[h] + p * mask_ref[lev]
    a_mats = [a.astype(bf16) for a in a_mats]

    for h in range(HG_HEADS):
        sl = slice(h * HG_DK, (h + 1) * HG_DK)
        b = b_all[:, sl]
        q = q_all[:, sl]
        f = f_all[:, sl]
        kk = kk_all[:, sl]
        v16 = i_ref[:, sl]
        o = _dot(a_mats[h], v16)
        vf = v16.astype(f32)
        a0 = jnp.sum(q * kk, axis=-1, keepdims=True)
        a1 = jnp.where(odd, jnp.sum(q * f * pltpu.roll(kk, 1, 0), axis=-1, keepdims=True), 0.0)
        o = o + a0 * vf + a1 * pltpu.roll(vf, 1, 0)
        st = s_scr[h]
        o = o + _dot_tb((q * jnp.exp(b)).astype(bf16), st.astype(bf16))
        bl = b[c - 1:c, :]
        kdec = (kk * jnp.exp(bl - b)).astype(bf16)
        s_scr[h] = st * jnp.exp(bl) + _dot_ta(v16, kdec)
        o_ref[:, sl] = _head_rms_gate(o, nw_ref[:, sl] * g_ref[:, sl].astype(f32)).astype(bf16)

    @pl.when(ci == pl.num_programs(1) - 1)
    def _():
        for h in range(HG_HEADS):
            st_ref[0, h] = s_scr[h].T


def _hgrn_prompt(q, lf, i, g, norm_w, bsz, t):
    c = HG_CHUNK
    nc = t // c
    hw = HG_HEADS * HG_DK
    tri = jnp.asarray(np.tril(np.ones((c, c), np.float32))).astype(bf16)
    masks = _hg_level_masks(c)
    row = lambda: pl.BlockSpec((c, hw), lambda b, j: (b * nc + j, 0))
    return pl.pallas_call(
        _hgrn_prompt_body,
        grid=(bsz, nc),
        in_specs=[row(), row(), row(), row(), _resident((1, hw)), _resident(tri.shape), _resident(masks.shape)],
        out_specs=[row(), pl.BlockSpec((1, HG_HEADS, HG_DK, HG_DV), lambda b, j: (b, 0, 0, 0))],
        out_shape=[jax.ShapeDtypeStruct((bsz * t, hw), bf16),
                   jax.ShapeDtypeStruct((bsz, HG_HEADS, HG_DK, HG_DV), f32)],
        scratch_shapes=[pltpu.VMEM((HG_HEADS, HG_DV, HG_DK), f32)],
        compiler_params=_cparams(("parallel", "arbitrary")),
        name="hgrn_prompt",
    )(q, lf, i, g, norm_w.reshape(1, -1), tri, masks)


def _lru_gate_math(xc, ya, yx, half_c_sp):
    neg_log_a = half_c_sp + half_c_sp * jnp.tanh(0.5 * ya)
    a = jnp.exp(-neg_log_a)
    y = jnp.tanh(neg_log_a) * (1.0 + a * a)
    scale = jnp.where(y > 0.0, y * lax.rsqrt(y), 0.0)
    hx = 0.5 * xc
    return a, scale * (hx + hx * jnp.tanh(0.5 * yx))


def _lru_gate_block(xc, wa, ba, wx, bx, half_c_sp):
    xb = xc.astype(bf16)
    return _lru_gate_math(xc, _dot(xb, wa) + ba, _dot(xb, wx) + bx, half_c_sp)


def _lru_gates(xc, wa_ref, ba, wx_ref, bx, half_c_sp):
    parts = []
    for n in range(LRU_BLOCKS):
        sl = slice(n * LRU_BD, (n + 1) * LRU_BD)
        parts.append(_lru_gate_block(xc[:, sl], wa_ref[n], ba[:, sl], wx_ref[n], bx[:, sl], half_c_sp[:, sl]))
    return jnp.concatenate([p[0] for p in parts], axis=1), jnp.concatenate([p[1] for p in parts], axis=1)


def _strided_rows(g, rows):
    j_count = rows // SUBLANES
    k, j0 = divmod(SUBLANES * g, j_count)
    return pl.ds(SUBLANES * j0 + k, SUBLANES, stride=SUBLANES)


def _shift_time(xp, tail, d):
    rows = xp.shape[0]
    row8 = lax.broadcasted_iota(jnp.int32, (SUBLANES, 1), 0)
    last = pltpu.roll(xp[rows - SUBLANES * d:rows, :].reshape(d, SUBLANES, xp.shape[1]), 1, 1)
    head = [jnp.where(row8 == 0, tail[SUBLANES - d + j:SUBLANES - d + j + 1, :], last[j]) for j in range(d)]
    return jnp.concatenate(head + [xp[0:rows - SUBLANES * d, :]], axis=0)


def _proj_lru_scan_body(x_ref, g_ref, w_ref, mb_ref, cw_ref, cb_ref, wa_ref, ba_ref, wx_ref, bx_ref, lam_ref,
                        o_ref, gs_ref, h_ref, cv_ref, xp_scr, xc_scr, ya_scr, yx_scr, hp_scr, ap_scr, tail_scr, h_scr):
    ti = pl.program_id(1)
    tm = x_ref.shape[0]
    nj = tm // SUBLANES

    @pl.when(ti == 0)
    def _():
        tail_scr[...] = jnp.zeros(tail_scr.shape, f32)
        h_scr[...] = jnp.zeros(h_scr.shape, f32)

    xn = _rmsnorm(x_ref[...], g_ref[...]).astype(bf16)
    for c in range(LRU_W // 256):
        z = _dot(xn, w_ref[:, c * 256:(c + 1) * 256])
        for half in range(2):
            for g in range(nj):
                xp_scr[2 * c + half, _strided_rows(g, tm), :] = (
                    z[SUBLANES * g:SUBLANES * (g + 1), half * LRU_BD:(half + 1) * LRU_BD])
    cw = cw_ref[...]
    cb = cb_ref[...]
    ba = ba_ref[...]
    bx = bx_ref[...]
    half_c_sp = (0.5 * LRU_C) * _softplus(-lam_ref[...])
    n_gate = N_BRANCH * D_MODEL // 256
    n_pair = LRU_W // 256
    for p in range(n_pair):
        blocks = (2 * p, 2 * p + 1)
        sls = [slice(n * LRU_BD, (n + 1) * LRU_BD) for n in blocks]
        for n, sl in zip(blocks, sls):
            xp = xp_scr[n]
            tail = tail_scr[:, sl]
            xc = cb[:, sl] + xp * cw[CONV_W - 1:CONV_W, sl]
            for d in range(1, CONV_W):
                xc = xc + _shift_time(xp, tail, d) * cw[CONV_W - 1 - d:CONV_W - d, sl]
            xb = xc.astype(bf16)
            xc_scr[n] = xc
            ya_scr[n] = _dot(xb, wa_ref[n]) + ba[:, sl]
            yx_scr[n] = _dot(xb, wx_ref[n]) + bx[:, sl]
        hseg = [jnp.zeros((SUBLANES, LRU_BD), f32) for _ in blocks]
        aseg = [jnp.ones((SUBLANES, LRU_BD), f32) for _ in blocks]
        for j in range(nj):
            rows = slice(SUBLANES * j, SUBLANES * (j + 1))
            for i, (n, sl) in enumerate(zip(blocks, sls)):
                a, u = _lru_gate_math(xc_scr[n, rows, :], ya_scr[n, rows, :], yx_scr[n, rows, :], half_c_sp[:, sl])
                hseg[i] = a * hseg[i] + u
                aseg[i] = a * aseg[i]
                hp_scr[n, rows, :] = hseg[i]
                ap_scr[n, rows, :] = aseg[i]
        for i, (n, sl) in enumerate(zip(blocks, sls)):
            s = h_scr[0:1, sl]
            starts = []
            for k in range(SUBLANES):
                starts.append(s)
                s = aseg[i][k:k + 1, :] * s + hseg[i][k:k + 1, :]
            h_scr[:, sl] = jnp.broadcast_to(s, (h_scr.shape[0], LRU_BD))
            start8 = jnp.concatenate(starts, axis=0)
            hp_scr[n] = (hp_scr[n].reshape(nj, SUBLANES, LRU_BD)
                         + ap_scr[n].reshape(nj, SUBLANES, LRU_BD) * start8).reshape(tm, LRU_BD)
            tail_scr[:, sl] = xp_scr[n, _strided_rows(nj - 1, tm), :]
        zg = jax.nn.gelu(_dot(xn, w_ref[:, LRU_W + p * 256:LRU_W + (p + 1) * 256]), approximate=True)
        for half in range(2):
            n = 2 * p + half
            hs = jnp.concatenate([hp_scr[n, _strided_rows(g, tm), :] for g in range(nj)], axis=0)
            o_ref[:, n * LRU_BD:(n + 1) * LRU_BD] = (hs * zg[:, half * LRU_BD:(half + 1) * LRU_BD]).astype(bf16)
        _merge_gates(xn, w_ref, mb_ref, gs_ref, range(p * n_gate // n_pair, (p + 1) * n_gate // n_pair))

    @pl.when(ti == pl.num_programs(1) - 1)
    def _():
        h_ref[0] = h_scr[0:1, :]
        cv_ref[0] = tail_scr[SUBLANES - (CONV_W - 1):SUBLANES, :]


def _proj_lru_scan(x, g, w, merge_bias, cw, cb, wa, ba, wx, bx, lam, bsz, t):
    tm = min(LRU_TILE, t)
    assert tm % (SUBLANES * SUBLANES) == 0
    nt = t // tm
    ng = N_BRANCH * D_MODEL
    row = lambda w_: pl.BlockSpec((tm, w_), lambda b, j: (b * nt + j, 0))
    vec = _resident((1, LRU_W))
    o, gs, h, cv = pl.pallas_call(
        _proj_lru_scan_body,
        grid=(bsz, nt),
        in_specs=[row(D_MODEL), _resident((1, D_MODEL)), _resident(w.shape), _resident((1, ng)),
                  _resident((CONV_W, LRU_W)), vec, _resident(wa.shape), vec, _resident(wx.shape), vec, vec],
        out_specs=[row(LRU_W), row(ng), pl.BlockSpec((1, 1, LRU_W), lambda b, j: (b, 0, 0)),
                   pl.BlockSpec((1, CONV_W - 1, LRU_W), lambda b, j: (b, 0, 0))],
        out_shape=[jax.ShapeDtypeStruct((bsz * t, LRU_W), bf16), jax.ShapeDtypeStruct((bsz * t, ng), bf16),
                   jax.ShapeDtypeStruct((bsz, 1, LRU_W), f32), jax.ShapeDtypeStruct((bsz, CONV_W - 1, LRU_W), f32)],
        scratch_shapes=[pltpu.VMEM((LRU_BLOCKS, tm, LRU_BD), f32)] * 6 + [pltpu.VMEM((SUBLANES, LRU_W), f32)] * 2,
        compiler_params=_cparams(("parallel", "arbitrary")),
        name="proj_lru_scan",
    )(x, g.reshape(1, -1), w, merge_bias.reshape(1, -1), cw, cb.reshape(1, -1), wa, ba.reshape(1, -1), wx,
      bx.reshape(1, -1), lam.reshape(1, -1))
    return o, gs, h[:, 0], cv


def _ret_sample_tables(t):
    log_g = _ret_log_gamma()
    r = jnp.arange(SAMPLE_BB * t)
    tt = (r % t).astype(f32)
    same = (r[:, None] // t) == (r[None, :] // t)
    diff = tt[:, None] - tt[None, :]
    ok = same & (diff >= 0)
    dm = jnp.where(ok, jnp.exp(log_g[:, None, None] * jnp.where(ok, diff, 0.0)), 0.0)
    qd = jnp.exp(log_g[:, None] * (tt + 1.0))[:, :, None]
    kd = jnp.exp(log_g[:, None] * (t - 1.0 - tt))[:, :, None]
    sd = jnp.exp(log_g * t)
    return dm, qd, kd, sd


def _stacked_state(layer, depth, bsz, heads_shape):
    shape = jax.ShapeDtypeStruct((depth, bsz) + heads_shape, f32)
    if layer == 0:
        spec = pl.BlockSpec((depth, SAMPLE_BB) + heads_shape, lambda i: (0, i, 0, 0, 0))

        def put(so_ref, b, h, val):
            so_ref[0, b, h] = val

        def finish(so_ref):
            for l in range(1, depth):
                so_ref[l] = jnp.zeros((SAMPLE_BB,) + heads_shape, f32)
    else:
        spec = pl.BlockSpec((None, SAMPLE_BB) + heads_shape, lambda i: (layer, i, 0, 0, 0))

        def put(so_ref, b, h, val):
            so_ref[b, h] = val

        def finish(so_ref):
            pass
    return shape, spec, put, finish


def _ret_sample_body(q_ref, k_ref, v_ref, g_ref, s_ref, dm_ref, qd_ref, kd_ref, sd_ref, *rest, t, put, finish):
    o_ref, so_ref = rest[-2:]
    bb = SAMPLE_BB
    finish(so_ref)
    for h in range(RET_HEADS):
        ks = slice(h * RET_DK, (h + 1) * RET_DK)
        vs = slice(h * RET_DV, (h + 1) * RET_DV)
        q = q_ref[:, ks]
        k = k_ref[:, ks]
        v = v_ref[:, vs]
        state = s_ref[:, h]
        scores = (_dot_tb(q, k) * dm_ref[h]).astype(bf16)
        inter = _dot(_block_diag(q, bb, t), state.reshape(bb * RET_DK, RET_DV).astype(bf16))
        o = _dot(scores, v) + qd_ref[h] * inter
        kd = (k.astype(f32) * kd_ref[h]).astype(bf16)
        ds = _dot_ta(kd, _block_diag(v, bb, t))
        for b in range(bb):
            put(so_ref, b, h, state[b] * sd_ref[h] + ds[:, b * RET_DV:(b + 1) * RET_DV])
        o_ref[:, vs] = _head_rms_gate(o, g_ref[:, vs].astype(f32)).astype(bf16)


def _ret_sample(q, k, v, g, state, stack, layer, bsz, t):
    nqk, nv = RET_HEADS * RET_DK, RET_HEADS * RET_DV
    bb = SAMPLE_BB
    heads_shape = (RET_HEADS, RET_DK, RET_DV)
    dm, qd, kd, s_dec = _ret_sample_tables(t)
    sd = jnp.broadcast_to(s_dec[:, None, None], (RET_HEADS, 1, RET_DV))
    row = lambda w_: pl.BlockSpec((bb * t, w_), lambda i: (i, 0))
    st_shape, st_spec, put, finish = _stacked_state(layer, state.shape[0], bsz, heads_shape)
    args = [q, k, v, g, state, dm, qd, kd, sd]
    in_specs = [row(nqk), row(nqk), row(nv), row(nv),
                pl.BlockSpec((None, bb) + heads_shape, lambda i: (layer, i, 0, 0, 0)),
                _resident(dm.shape), _resident(qd.shape), _resident(kd.shape), _resident(sd.shape)]
    aliases = {}
    if stack is not None:
        aliases = {len(args): 1}
        args.append(stack)
        in_specs.append(pl.BlockSpec(memory_space=pl.ANY))
    return pl.pallas_call(
        functools.partial(_ret_sample_body, t=t, put=put, finish=finish),
        grid=(bsz // bb,),
        in_specs=in_specs,
        out_specs=[row(nv), st_spec],
        out_shape=[jax.ShapeDtypeStruct((bsz * t, nv), bf16), st_shape],
        input_output_aliases=aliases,
        compiler_params=_cparams(("parallel",)),
        name="ret_sample",
    )(*args)


def _hgrn_sample_body(q_ref, lf_ref, i_ref, g_ref, nw_ref, s_ref, *rest, t, put, finish):
    assert t >= 2
    o_ref, so_ref = rest[-2:]
    finish(so_ref)
    bb = SAMPLE_BB
    n = bb * t
    tt = lax.broadcasted_iota(jnp.int32, (n, 1), 0) % t
    f = jnp.exp(lf_ref[...])
    kk = 1.0 - f
    q = q_ref[...].astype(f32)
    v = i_ref[...].astype(f32)
    before = [jnp.where(tt >= d, pltpu.roll(f, d, 0), 1.0) for d in range(1, t)]
    after = [jnp.where(tt < t - d, pltpu.roll(f, n - d, 0), 1.0) for d in range(1, t)]
    dec = [None, f]
    for d in range(2, t):
        dec.append(dec[-1] * before[d - 2])
    pre = dec[t - 1] * before[t - 2]
    suf = after[0]
    for d in range(1, t - 1):
        suf = suf * after[d]
    total = pre * suf
    qp = (q * pre).astype(bf16)
    kp = (kk * suf).astype(bf16)
    k_back = [kk] + [pltpu.roll(kk, d, 0) for d in range(1, t)]
    v_back = [v] + [pltpu.roll(v, d, 0) for d in range(1, t)]
    for h in range(HG_HEADS):
        sl = slice(h * HG_DK, (h + 1) * HG_DK)
        o = jnp.zeros((n, HG_DV), f32)
        for d in range(t):
            p = q[:, sl] * k_back[d][:, sl]
            if d > 0:
                p = p * dec[d][:, sl]
            a = jnp.sum(p, axis=-1, keepdims=True)
            if d > 0:
                a = jnp.where(tt >= d, a, 0.0)
            o = o + a * v_back[d][:, sl]
        state = s_ref[:, h]
        o = o + _dot(_block_diag(qp[:, sl], bb, t), state.reshape(bb * HG_DK, HG_DV).astype(bf16))
        ds = _dot_ta(kp[:, sl], _block_diag(i_ref[:, sl], bb, t))
        total_t = total[:, sl].T
        for b in range(bb):
            put(so_ref, b, h, state[b] * total_t[:, b * t:b * t + 1] + ds[:, b * HG_DV:(b + 1) * HG_DV])
        o_ref[:, sl] = _head_rms_gate(o, nw_ref[:, sl] * g_ref[:, sl].astype(f32)).astype(bf16)


def _hgrn_sample(q, lf, i, g, norm_w, state, stack, layer, bsz, t):
    hw = HG_HEADS * HG_DK
    bb = SAMPLE_BB
    heads_shape = (HG_HEADS, HG_DK, HG_DV)
    row = lambda: pl.BlockSpec((bb * t, hw), lambda j: (j, 0))
    st_shape, st_spec, put, finish = _stacked_state(layer, state.shape[0], bsz, heads_shape)
    args = [q, lf, i, g, norm_w.reshape(1, -1), state]
    in_specs = [row(), row(), row(), row(), _resident((1, hw)),
                pl.BlockSpec((None, bb) + heads_shape, lambda j: (layer, j, 0, 0, 0))]
    aliases = {}
    if stack is not None:
        aliases = {len(args): 1}
        args.append(stack)
        in_specs.append(pl.BlockSpec(memory_space=pl.ANY))
    return pl.pallas_call(
        functools.partial(_hgrn_sample_body, t=t, put=put, finish=finish),
        grid=(bsz // bb,),
        in_specs=in_specs,
        out_specs=[row(), st_spec],
        out_shape=[jax.ShapeDtypeStruct((bsz * t, hw), bf16), st_shape],
        input_output_aliases=aliases,
        compiler_params=_cparams(("parallel",)),
        name="hgrn_sample",
    )(*args)


def _lru_sample_body(x_ref, gl_ref, cs_ref, h0_ref, cw_ref, cb_ref, wa_ref, ba_ref, wx_ref, bx_ref, lam_ref,
                     o_ref, h_ref, cv_ref, *, t):
    xx = [cs_ref[j] for j in range(CONV_W - 1)] + [x_ref[j] for j in range(t)]
    cw = cw_ref[...]
    half_c_sp = (0.5 * LRU_C) * _softplus(-lam_ref[...])
    h = h0_ref[...]
    for j in range(t):
        xc = cb_ref[...]
        for i in range(CONV_W):
            xc = xc + xx[j + i] * cw[i:i + 1, :]
        a, u = _lru_gates(xc, wa_ref, ba_ref[...], wx_ref, bx_ref[...], half_c_sp)
        h = a * h + u
        o_ref[j] = (h * gl_ref[j].astype(f32)).astype(bf16)
    h_ref[...] = h
    for j in range(CONV_W - 1):
        cv_ref[j] = xx[t + j]


def _lru_sample(xl, gl, cs, h0, cw, cb, wa, ba, wx, bx, lam, bsz, t):
    full = lambda a: pl.BlockSpec(a.shape, lambda i: (0,) * a.ndim)
    args = (xl, gl, cs, h0, cw, cb.reshape(1, -1), wa, ba.reshape(1, -1), wx, bx.reshape(1, -1), lam.reshape(1, -1))
    return pl.pallas_call(
        functools.partial(_lru_sample_body, t=t),
        grid=(1,),
        in_specs=[full(a) for a in args],
        out_specs=[pl.BlockSpec((t, bsz, LRU_W), lambda i: (0, 0, 0)), pl.BlockSpec((bsz, LRU_W), lambda i: (0, 0)),
                   pl.BlockSpec((CONV_W - 1, bsz, LRU_W), lambda i: (0, 0, 0))],
        out_shape=[jax.ShapeDtypeStruct((t, bsz, LRU_W), bf16), jax.ShapeDtypeStruct((bsz, LRU_W), f32),
                   jax.ShapeDtypeStruct((CONV_W - 1, bsz, LRU_W), f32)],
        compiler_params=_cparams(("arbitrary",)),
        name="lru_sample",
    )(*args)


def _rope_tables(pos):
    inv = ROPE_BASE ** (-jnp.arange(0, RET_DK, 2, dtype=f32) / RET_DK)
    ang = pos.astype(f32)[:, None] * inv[None, :]
    cos, sin = jnp.cos(ang), jnp.sin(ang)
    return jnp.concatenate([cos, cos], axis=1), jnp.concatenate([-sin, sin], axis=1)


def _trunk(x, pos, w, states, bsz, t):
    cos, sin = _rope_tables(pos)
    cos = jnp.tile(cos, (bsz, 1))
    sin = jnp.tile(sin, (bsz, 1))
    news = ([], [], [], [])
    ret_stack = hg_stack = None
    depth = w["w_ret"].shape[0]
    for l in range(depth):
        x = _ffn(x, w["ffn1_norm"][l], w["ffn1_wg"][l], w["ffn1_wu"][l], w["ffn1_wd"][l])
        q_r, k_r, v_r, g_r = _proj_ret(x, w["mix_norm"][l], w["w_ret"][l], cos, sin)
        q_h, lf, i_h, g_h = _proj_hgrn(x, w["mix_norm"][l], w["w_hg"][l], w["hgrn_lb_logits"], l)
        lru_w = (w["conv_w"][l], w["conv_b"][l], w["lru_w_a"][l], w["lru_b_a"][l], w["lru_w_x"][l],
                 w["lru_b_x"][l], w["lru_lambda"][l])
        if states is None:
            o_r, ret_new = _ret_prompt(q_r, k_r, v_r, g_r, bsz, t)
            o_h, hg_new = _hgrn_prompt(q_h, lf, i_h, g_h, w["hgrn_norm"][l], bsz, t)
            o_l, gs, h_new, cv_new = _proj_lru_scan(x, w["mix_norm"][l], w["w_lru"][l], w["merge_bias"][l],
                                                    *lru_w, bsz, t)
            news[0].append(ret_new)
            news[1].append(hg_new)
        else:
            st_ret, st_hg, st_lru, st_conv = states
            o_r, ret_stack = _ret_sample(q_r, k_r, v_r, g_r, st_ret, ret_stack, l, bsz, t)
            o_h, hg_stack = _hgrn_sample(q_h, lf, i_h, g_h, w["hgrn_norm"][l], st_hg, hg_stack, l, bsz, t)
            x_l, g_l, gs = _proj_lru(x, w["mix_norm"][l], w["w_lru"][l], w["merge_bias"][l])
            tm = lambda a: a.reshape(bsz, t, LRU_W).transpose(1, 0, 2)
            o_lt, h_new, cv_t = _lru_sample(tm(x_l), tm(g_l), st_conv[l].transpose(1, 0, 2), st_lru[l], *lru_w, bsz, t)
            o_l = o_lt.transpose(1, 0, 2).reshape(bsz * t, LRU_W)
            cv_new = cv_t.transpose(1, 0, 2)
        x = _mix_out(x, o_r, o_h, o_l, gs, w["w_ret_o"][l], w["w_hgrn_o"][l], w["w_lru_o"][l], w["w_mix_out"][l])
        last = l == depth - 1
        x = _ffn(x, w["ffn2_norm"][l], w["ffn2_wg"][l], w["ffn2_wu"][l], w["ffn2_wd"][l],
                 final_g=w["final_norm"] if last else None)
        news[2].append(h_new)
        news[3].append(cv_new)
    if states is None:
        return x, [jnp.stack(a, axis=0) for a in news]
    return x, [ret_stack, hg_stack, jnp.stack(news[2], axis=0), jnp.stack(news[3], axis=0)]


def kernel(x_prompt, x_sample, state_ret, state_hgrn, state_lru, state_conv, ffn1_norm, ffn1_w_gate, ffn1_w_up, ffn1_w_down, mix_norm, w_in, merge_bias, w_ret_o, hgrn_lb_logits, hgrn_norm, w_hgrn_o, conv_w, conv_b, lru_w_a, lru_b_a, lru_w_x, lru_b_x, lru_lambda, w_lru_o, w_mix_out, ffn2_norm, ffn2_w_gate, ffn2_w_up, ffn2_w_down, final_norm):
    c16 = lambda a: a.astype(bf16)
    w = dict(
        ffn1_norm=ffn1_norm, ffn1_wg=c16(ffn1_w_gate), ffn1_wu=c16(ffn1_w_up), ffn1_wd=c16(ffn1_w_down),
        mix_norm=mix_norm,
        w_ret=c16(w_in[:, :, :RET_COLS]), w_hg=c16(w_in[:, :, RET_COLS:RET_COLS + HG_COLS]),
        w_lru=c16(w_in[:, :, RET_COLS + HG_COLS:]),
        merge_bias=merge_bias, w_ret_o=c16(w_ret_o), hgrn_lb_logits=hgrn_lb_logits, hgrn_norm=hgrn_norm,
        w_hgrn_o=c16(w_hgrn_o), conv_w=conv_w, conv_b=conv_b, lru_w_a=c16(lru_w_a), lru_b_a=lru_b_a,
        lru_w_x=c16(lru_w_x), lru_b_x=lru_b_x, lru_lambda=lru_lambda, w_lru_o=c16(w_lru_o),
        w_mix_out=c16(w_mix_out), ffn2_norm=ffn2_norm, ffn2_wg=c16(ffn2_w_gate), ffn2_wu=c16(ffn2_w_up),
        ffn2_wd=c16(ffn2_w_down), final_norm=final_norm,
    )
    bp, tp, _ = x_prompt.shape
    bs, ts, _ = x_sample.shape
    y_p, st_p = _trunk(x_prompt.reshape(bp * tp, D_MODEL), jnp.arange(tp), w, None, bp, tp)
    y_s, st_s = _trunk(x_sample.reshape(bs * ts, D_MODEL), PAST_LEN + jnp.arange(ts), w,
                       (state_ret, state_hgrn, state_lru, state_conv), bs, ts)
    return (y_p.reshape(bp, tp, D_MODEL), y_s.reshape(bs, ts, D_MODEL),
            st_p[0], st_s[0], st_p[1], st_s[1], st_p[2], st_s[2], st_p[3], st_s[3])
```

```python
import functools

import jax
import jax.numpy as jnp
import numpy as np
from jax import lax
from jax.experimental import pallas as pl
from jax.experimental.pallas import tpu as pltpu

f32 = jnp.float32
bf16 = jnp.bfloat16

D_MODEL = 1024
FFN_HIDDEN = 2816
RET_HEADS, RET_DK, RET_DV = 4, 128, 256
HG_HEADS, HG_DK, HG_DV = 8, 128, 128
LRU_W, LRU_BLOCKS, LRU_BD = 1024, 8, 128
CONV_W = 4
LRU_C = 8.0
N_BRANCH = 3
ROPE_BASE = 10000.0
PAST_LEN = 16384
LOG2_E = 1.4426950408889634
EPS = 1e-6
RET_COLS = 2 * RET_HEADS * RET_DK + 2 * RET_HEADS * RET_DV
HG_COLS = 4 * HG_HEADS * HG_DK
LRU_COLS = 2 * LRU_W + N_BRANCH * D_MODEL

V7X_VMEM_BYTES = 64 * 1024 * 1024
VMEM_LIMIT = 56 * 1024 * 1024
SUBLANES = 8

TOKEN_TILE = 1024
LRU_TILE = 512
FFN_CHUNK = 256
RET_CHUNK = 256
HG_CHUNK = 128
HG_STEP_CHUNKS = 4
RET_STEP_CHUNKS = 4
HG_LEVELS = ((128, 64), (64, 32), (32, 16), (16, 8), (8, 4), (4, 2))
SAMPLE_BB = 8


def _cparams(sem):
    return pltpu.CompilerParams(dimension_semantics=sem, vmem_limit_bytes=VMEM_LIMIT)


def _resident(shape):
    nd = len(shape)
    return pl.BlockSpec(shape, lambda *_: (0,) * nd, pipeline_mode=pl.Buffered(1))


def _rmsnorm(x, g):
    return x * lax.rsqrt(jnp.mean(x * x, axis=-1, keepdims=True) + EPS) * g


def _dot(a, b):
    return jnp.dot(a, b, preferred_element_type=f32)


def _dot_tb(a, b):
    return lax.dot_general(a, b, (((1,), (1,)), ((), ())), preferred_element_type=f32)


def _dot_ta(a, b):
    return lax.dot_general(a, b, (((0,), (0,)), ((), ())), preferred_element_type=f32)


def _silu(x):
    hx = 0.5 * x
    return hx + hx * jnp.tanh(hx)


def _softplus(x):
    return jnp.maximum(x, 0.0) + jnp.log1p(jnp.exp(-jnp.abs(x)))


def _block_diag(x, nb, rows_per):
    r, w = x.shape
    rb = lax.broadcasted_iota(jnp.int32, (r, nb * w), 0) // rows_per
    cb = lax.broadcasted_iota(jnp.int32, (r, nb * w), 1) // w
    return jnp.where(rb == cb, jnp.tile(x, (1, nb)), jnp.zeros((), x.dtype))


def _ffn_body(x_ref, g_ref, wg_ref, wu_ref, wd_ref, fg_ref, o_ref, *, final):
    x = x_ref[...]
    xn = _rmsnorm(x, g_ref[...]).astype(bf16)
    acc = jnp.zeros(x.shape, f32)
    for c in range(FFN_HIDDEN // FFN_CHUNK):
        sl = slice(c * FFN_CHUNK, (c + 1) * FFN_CHUNK)
        g = _dot(xn, wg_ref[:, sl])
        u = _dot(xn, wu_ref[:, sl])
        h = (_silu(g) * u).astype(bf16)
        acc = acc + _dot(h, wd_ref[sl, :])
    y = x + 0.5 * acc
    if final:
        y = _rmsnorm(y, fg_ref[...])
    o_ref[...] = y


def _ffn(x, g, wg, wu, wd, final_g=None):
    n = x.shape[0]
    tm = min(TOKEN_TILE, n)
    final = final_g is not None
    fg = final_g if final else g
    row = pl.BlockSpec((tm, D_MODEL), lambda i: (i, 0))
    return pl.pallas_call(
        functools.partial(_ffn_body, final=final),
        grid=(n // tm,),
        in_specs=[row, _resident((1, D_MODEL)), _resident(wg.shape), _resident(wu.shape),
                  _resident(wd.shape), _resident((1, D_MODEL))],
        out_specs=row,
        out_shape=jax.ShapeDtypeStruct((n, D_MODEL), f32),
        compiler_params=_cparams(("parallel",)),
        name="ffn",
    )(x, g.reshape(1, -1), wg, wu, wd, fg.reshape(1, -1))


def _proj_ret_body(x_ref, g_ref, w_ref, cos_ref, sin_ref, q_ref, k_ref, v_ref, gr_ref, xn_ref):
    xn = _rmsnorm(x_ref[...], g_ref[...]).astype(bf16)
    xn_ref[...] = xn
    cos = cos_ref[...]
    sin = sin_ref[...]
    nqk = RET_HEADS * RET_DK

    def rot(z):
        return z * cos + pltpu.roll(z, RET_DK // 2, 1) * sin

    for c in range(nqk // 256):
        zq = _dot(xn, w_ref[:, c * 256:(c + 1) * 256])
        zk = _dot(xn, w_ref[:, nqk + c * 256:nqk + (c + 1) * 256])
        for j in range(2):
            sl = slice(c * 256 + j * 128, c * 256 + (j + 1) * 128)
            q_ref[:, sl] = rot(zq[:, j * 128:(j + 1) * 128]).astype(bf16)
            k_ref[:, sl] = (rot(zk[:, j * 128:(j + 1) * 128]) * (RET_DK ** -0.5)).astype(bf16)
    nv = RET_HEADS * RET_DV
    for c in range(nv // 256):
        sl = slice(c * 256, (c + 1) * 256)
        v_ref[:, sl] = _dot(xn, w_ref[:, 2 * nqk + c * 256:2 * nqk + (c + 1) * 256]).astype(bf16)
        zg = _dot(xn, w_ref[:, 2 * nqk + nv + c * 256:2 * nqk + nv + (c + 1) * 256])
        gr_ref[:, sl] = _silu(zg).astype(bf16)


def _proj_ret(x, g, w, cos, sin):
    n = x.shape[0]
    tm = min(TOKEN_TILE, n)
    row = lambda w_: pl.BlockSpec((tm, w_), lambda i: (i, 0))
    nqk, nv = RET_HEADS * RET_DK, RET_HEADS * RET_DV
    assert cos.shape[0] % tm == 0
    period = cos.shape[0] // tm
    rope = pl.BlockSpec((tm, RET_DK), lambda i: (i % period, 0))
    return pl.pallas_call(
        _proj_ret_body,
        grid=(n // tm,),
        in_specs=[row(D_MODEL), _resident((1, D_MODEL)), _resident(w.shape), rope, rope],
        out_specs=[row(nqk), row(nqk), row(nv), row(nv), row(D_MODEL)],
        out_shape=[jax.ShapeDtypeStruct((n, nqk), bf16), jax.ShapeDtypeStruct((n, nqk), bf16),
                   jax.ShapeDtypeStruct((n, nv), bf16), jax.ShapeDtypeStruct((n, nv), bf16),
                   jax.ShapeDtypeStruct((n, D_MODEL), bf16)],
        compiler_params=_cparams(("parallel",)),
        name="proj_ret",
    )(x, g.reshape(1, -1), w, cos, sin)


def _proj_hgrn_body(xn_ref, w_ref, lbl_ref, q_ref, lf_ref, i_ref, gh_ref, *, layer):
    xn = xn_ref[...]
    hw = HG_HEADS * HG_DK
    if layer > 0:
        lg = lbl_ref[...]
        e = jnp.exp(lg - jnp.max(lg, axis=0, keepdims=True))
        sm = e / jnp.sum(e, axis=0, keepdims=True)
        lb = jnp.sum(sm[1:layer + 1], axis=0, keepdims=True)
    for c in range(hw // 256):
        sl = slice(c * 256, (c + 1) * 256)
        q_ref[:, sl] = _silu(_dot(xn, w_ref[:, sl])).astype(bf16)
        zf = _dot(xn, w_ref[:, hw + c * 256:hw + (c + 1) * 256])
        t = jnp.exp2(jnp.abs(zf) * -LOG2_E)
        num = jnp.minimum(zf, 0.0)
        if layer > 0:
            lbc = lb[:, sl]
            num = jnp.maximum(jnp.log(jnp.where(zf < 0.0, lbc + t, 1.0 + lbc * t)), num)
        lf_ref[:, sl] = num - jnp.log(1.0 + t)
        i_ref[:, sl] = _dot(xn, w_ref[:, 2 * hw + c * 256:2 * hw + (c + 1) * 256]).astype(bf16)
        gh_ref[:, sl] = _silu(_dot(xn, w_ref[:, 3 * hw + c * 256:3 * hw + (c + 1) * 256])).astype(bf16)


def _proj_hgrn(xn, w, lb_logits, layer):
    n = xn.shape[0]
    tm = min(TOKEN_TILE, n)
    hw = HG_HEADS * HG_DK
    row = lambda w_: pl.BlockSpec((tm, w_), lambda i: (i, 0))
    return pl.pallas_call(
        functools.partial(_proj_hgrn_body, layer=layer),
        grid=(n // tm,),
        in_specs=[row(D_MODEL), _resident(w.shape), _resident(lb_logits.shape)],
        out_specs=[row(hw), row(hw), row(hw), row(hw)],
        out_shape=[jax.ShapeDtypeStruct((n, hw), bf16), jax.ShapeDtypeStruct((n, hw), f32),
                   jax.ShapeDtypeStruct((n, hw), bf16), jax.ShapeDtypeStruct((n, hw), bf16)],
        compiler_params=_cparams(("parallel",)),
        name="proj_hgrn",
    )(xn, w, lb_logits)


def _merge_gates(xn, w_ref, mb_ref, gs_ref, chunks=range(N_BRANCH * D_MODEL // 256)):
    for c in chunks:
        sl = slice(c * 256, (c + 1) * 256)
        z = _dot(xn, w_ref[:, 2 * LRU_W + c * 256:2 * LRU_W + (c + 1) * 256])
        gs_ref[:, sl] = jax.nn.sigmoid(z + mb_ref[:, sl]).astype(bf16)


def _proj_lru_body(xn_ref, w_ref, mb_ref, xl_ref, gl_ref, gs_ref):
    xn = xn_ref[...]
    for c in range(LRU_W // 256):
        sl = slice(c * 256, (c + 1) * 256)
        xl_ref[:, sl] = _dot(xn, w_ref[:, sl])
        zg = _dot(xn, w_ref[:, LRU_W + c * 256:LRU_W + (c + 1) * 256])
        gl_ref[:, sl] = jax.nn.gelu(zg, approximate=True).astype(bf16)
    _merge_gates(xn, w_ref, mb_ref, gs_ref)


def _proj_lru(xn, w, merge_bias):
    n = xn.shape[0]
    tm = min(TOKEN_TILE, n)
    row = lambda w_: pl.BlockSpec((tm, w_), lambda i: (i, 0))
    ng = N_BRANCH * D_MODEL
    return pl.pallas_call(
        _proj_lru_body,
        grid=(n // tm,),
        in_specs=[row(D_MODEL), _resident(w.shape), _resident((1, ng))],
        out_specs=[row(LRU_W), row(LRU_W), row(ng)],
        out_shape=[jax.ShapeDtypeStruct((n, LRU_W), f32), jax.ShapeDtypeStruct((n, LRU_W), bf16),
                   jax.ShapeDtypeStruct((n, ng), bf16)],
        compiler_params=_cparams(("parallel",)),
        name="proj_lru",
    )(xn, w, merge_bias.reshape(1, -1))


def _mix_out_body(x_ref, or_ref, oh_ref, ol_ref, gs_ref, wr_ref, wh_ref, wl_ref, wo_ref, o_ref):
    d = D_MODEL
    merged = gs_ref[:, 0:d].astype(f32) * _dot(or_ref[...], wr_ref[...])
    merged = merged + gs_ref[:, d:2 * d].astype(f32) * _dot(oh_ref[...], wh_ref[...])
    merged = merged + gs_ref[:, 2 * d:3 * d].astype(f32) * _dot(ol_ref[...], wl_ref[...])
    o_ref[...] = x_ref[...] + _dot(merged.astype(bf16), wo_ref[...])


def _mix_out(x, o_r, o_h, o_l, gs, wr, wh, wl, wo):
    n = x.shape[0]
    tm = min(TOKEN_TILE, n)
    row = lambda w_: pl.BlockSpec((tm, w_), lambda i: (i, 0))
    return pl.pallas_call(
        _mix_out_body,
        grid=(n // tm,),
        in_specs=[row(D_MODEL), row(D_MODEL), row(D_MODEL), row(D_MODEL), row(N_BRANCH * D_MODEL),
                  _resident(wr.shape), _resident(wh.shape), _resident(wl.shape), _resident(wo.shape)],
        out_specs=row(D_MODEL),
        out_shape=jax.ShapeDtypeStruct((n, D_MODEL), f32),
        compiler_params=_cparams(("parallel",)),
        name="mix_out",
    )(x, o_r, o_h, o_l, gs, wr, wh, wl, wo)


def _ret_log_gamma():
    return jnp.log1p(-jnp.exp2(-5.0 - jnp.arange(RET_HEADS, dtype=f32)))


def _ret_tables(c):
    log_g = _ret_log_gamma()
    idx = jnp.arange(c, dtype=f32)
    diff = idx[:, None] - idx[None, :]
    causal = diff >= 0
    d_intra = jnp.where(causal, jnp.exp(log_g[:, None, None] * jnp.where(causal, diff, 0.0)), 0.0)
    q_dec = jnp.exp(log_g[:, None] * (idx + 1.0))[:, :, None]
    k_dec = jnp.exp(log_g[:, None] * (c - 1.0 - idx))[:, :, None]
    s_dec = jnp.exp(log_g * c)
    return d_intra, q_dec, k_dec, s_dec


def _head_rms_gate(o, gate):
    return o * lax.rsqrt(jnp.mean(o * o, axis=-1, keepdims=True) + EPS) * gate


def _ret_prompt_body(q_ref, k_ref, v_ref, g_ref, dm_ref, qd_ref, kd_ref, sd_ref, o_ref, st_ref, s_scr):
    c = pl.program_id(1)

    @pl.when(c == 0)
    def _():
        s_scr[...] = jnp.zeros(s_scr.shape, f32)

    heads = range(RET_HEADS)
    ks = [slice(h * RET_DK, (h + 1) * RET_DK) for h in heads]
    vs = [slice(h * RET_DV, (h + 1) * RET_DV) for h in heads]
    for r0 in range(0, q_ref.shape[0], RET_CHUNK):
        rows = slice(r0, r0 + RET_CHUNK)
        scores = [(_dot_tb(q_ref[rows, ks[h]], k_ref[rows, ks[h]]) * dm_ref[h]).astype(bf16) for h in heads]
        inter = [qd_ref[h] * _dot(q_ref[rows, ks[h]], s_scr[h].astype(bf16)) for h in heads]
        outs = [_dot(scores[h], v_ref[rows, vs[h]]) + inter[h] for h in heads]
        for h in heads:
            kd = (k_ref[rows, ks[h]].astype(f32) * kd_ref[h]).astype(bf16)
            s_scr[h] = s_scr[h] * sd_ref[h] + _dot_ta(kd, v_ref[rows, vs[h]])
        for h in heads:
            o_ref[rows, vs[h]] = _head_rms_gate(outs[h], g_ref[rows, vs[h]].astype(f32)).astype(bf16)

    @pl.when(c == pl.num_programs(1) - 1)
    def _():
        st_ref[0] = s_scr[...]


def _ret_prompt(q, k, v, g, bsz, t):
    c = RET_CHUNK
    rows = RET_STEP_CHUNKS * c
    nc = t // rows
    nqk, nv = RET_HEADS * RET_DK, RET_HEADS * RET_DV
    d_intra, q_dec, k_dec, s_dec = _ret_tables(c)
    sd = jnp.broadcast_to(s_dec[:, None, None], (RET_HEADS, 1, RET_DV))
    row = lambda w_: pl.BlockSpec((rows, w_), lambda b, i: (b * nc + i, 0))
    return pl.pallas_call(
        _ret_prompt_body,
        grid=(bsz, nc),
        in_specs=[row(nqk), row(nqk), row(nv), row(nv), _resident(d_intra.shape), _resident(q_dec.shape),
                  _resident(k_dec.shape), _resident(sd.shape)],
        out_specs=[row(nv), pl.BlockSpec((1, RET_HEADS, RET_DK, RET_DV), lambda b, i: (b, 0, 0, 0))],
        out_shape=[jax.ShapeDtypeStruct((bsz * t, nv), bf16),
                   jax.ShapeDtypeStruct((bsz, RET_HEADS, RET_DK, RET_DV), f32)],
        scratch_shapes=[pltpu.VMEM((RET_HEADS, RET_DK, RET_DV), f32)],
        compiler_params=_cparams(("parallel", "arbitrary")),
        name="ret_prompt",
    )(q, k, v, g, d_intra, q_dec, k_dec, sd)


def _hg_level_masks(c):
    t = np.arange(c)[:, None]
    s = np.arange(c)[None, :]
    ms = [((t // blk == s // blk) & (t % blk >= half) & (s % blk < half)) for blk, half in HG_LEVELS]
    return jnp.asarray(np.stack(ms).astype(np.float32))


def _boundary_rows(b, blk, half):
    c, w = b.shape
    if half >= SUBLANES:
        return jnp.concatenate(
            [jnp.broadcast_to(b[j * blk + half - 1:j * blk + half, :], (blk, w)) for j in range(c // blk)], axis=0)
    b3 = b.reshape(c // SUBLANES, SUBLANES, w)
    row8 = lax.broadcasted_iota(jnp.int32, (1, SUBLANES, 1), 1)
    out = b3[:, half - 1:half, :]
    for j in range(1, SUBLANES // blk):
        out = jnp.where(row8 >= j * blk, b3[:, j * blk + half - 1:j * blk + half, :], out)
    return jnp.broadcast_to(out, b3.shape).reshape(c, w)


def _upper_lower(up, low, blk, half):
    c = up.shape[0]
    if half >= SUBLANES:
        parts = []
        for j in range(c // blk):
            parts += [low[j * blk:j * blk + half, :], up[j * blk + half:(j + 1) * blk, :]]
        return jnp.concatenate(parts, axis=0)
    upper = lax.broadcasted_iota(jnp.int32, (c, 1), 0) % blk >= half
    return jnp.where(upper, up, low)


def _hgrn_chunk(q_ref, lf_ref, i_ref, g_ref, nw_ref, tri_ref, mask_ref, o_ref, s_scr, rows):
    c = HG_CHUNK
    lf_all = lf_ref[rows, :]
    hi = lf_all.astype(bf16)
    lo = (lf_all - hi.astype(f32)).astype(bf16)
    tri = tri_ref[...]
    b2_all = (_dot(tri, hi) + _dot(tri, lo)) * LOG2_E
    odd = lax.broadcasted_iota(jnp.int32, (c, 1), 0) % 2 == 1

    f_all = jnp.exp(lf_all)
    kk_all = 1.0 - f_all
    q_all = q_ref[rows, :].astype(f32)
    a_mats = [None] * HG_HEADS
    for lev, (blk, half) in enumerate(HG_LEVELS):
        keep = mask_ref[lev] > 0.5
        for h in range(HG_HEADS):
            sl = slice(h * HG_DK, (h + 1) * HG_DK)
            b2 = b2_all[:, sl]
            bm = _boundary_rows(b2, blk, half)
            e = jnp.exp2(_upper_lower(b2 - bm, bm - b2, blk, half))
            m = (_upper_lower(q_all[:, sl], kk_all[:, sl], blk, half) * e).astype(bf16)
            p = _dot_tb(m, m)
            a_mats[h] = jnp.where(keep, p, 0.0) if lev == 0 else jnp.where(keep, p, a_mats[h])
    a_mats = [a.astype(bf16) for a in a_mats]

    for h in range(HG_HEADS):
        sl = slice(h * HG_DK, (h + 1) * HG_DK)
        b2 = b2_all[:, sl]
        q = q_all[:, sl]
        f = f_all[:, sl]
        kk = kk_all[:, sl]
        v16 = i_ref[rows, sl]
        o = _dot(a_mats[h], v16)
        vf = v16.astype(f32)
        a0 = jnp.sum(q * kk, axis=-1, keepdims=True)
        a1 = jnp.where(odd, jnp.sum(q * f * pltpu.roll(kk, 1, 0), axis=-1, keepdims=True), 0.0)
        o = o + a0 * vf + a1 * pltpu.roll(vf, 1, 0)
        st = s_scr[h]
        o = o + _dot_tb((q * jnp.exp2(b2)).astype(bf16), st.astype(bf16))
        bl2 = b2[c - 1:c, :]
        kdec = (kk * jnp.exp2(bl2 - b2)).astype(bf16)
        s_scr[h] = st * jnp.exp2(bl2) + _dot_ta(v16, kdec)
        o_ref[rows, sl] = _head_rms_gate(o, nw_ref[:, sl] * g_ref[rows, sl].astype(f32)).astype(bf16)


def _hgrn_prompt_body(q_ref, lf_ref, i_ref, g_ref, nw_ref, tri_ref, mask_ref, o_ref, st_ref, s_scr):
    ci = pl.program_id(1)
    c = HG_CHUNK

    @pl.when(ci == 0)
    def _():
        s_scr[...] = jnp.zeros(s_scr.shape, f32)

    for r0 in range(0, q_ref.shape[0], c):
        _hgrn_chunk(q_ref, lf_ref, i_ref, g_ref, nw_ref, tri_ref, mask_ref, o_ref, s_scr, slice(r0, r0 + c))

    @pl.when(ci == pl.num_programs(1) - 1)
    def _():
        for h in range(HG_HEADS):
            st_ref[0, h] = s_scr[h].T


def _hgrn_prompt(q, lf, i, g, norm_w, bsz, t):
    c = HG_CHUNK
    rows = HG_STEP_CHUNKS * c
    nc = t // rows
    hw = HG_HEADS * HG_DK
    tri = jnp.asarray(np.tril(np.ones((c, c), np.float32))).astype(bf16)
    masks = _hg_level_masks(c)
    row = lambda: pl.BlockSpec((rows, hw), lambda b, j: (b * nc + j, 0))
    return pl.pallas_call(
        _hgrn_prompt_body,
        grid=(bsz, nc),
        in_specs=[row(), row(), row(), row(), _resident((1, hw)), _resident(tri.shape), _resident(masks.shape)],
        out_specs=[row(), pl.BlockSpec((1, HG_HEADS, HG_DK, HG_DV), lambda b, j: (b, 0, 0, 0))],
        out_shape=[jax.ShapeDtypeStruct((bsz * t, hw), bf16),
                   jax.ShapeDtypeStruct((bsz, HG_HEADS, HG_DK, HG_DV), f32)],
        scratch_shapes=[pltpu.VMEM((HG_HEADS, HG_DV, HG_DK), f32)],
        compiler_params=_cparams(("parallel", "arbitrary")),
        name="hgrn_prompt",
    )(q, lf, i, g, norm_w.reshape(1, -1), tri, masks)


def _lru_gate_math(xc, ya, yx, half_c_sp):
    neg_log_a = half_c_sp + half_c_sp * jnp.tanh(0.5 * ya)
    a = jnp.exp(-neg_log_a)
    y = jnp.tanh(neg_log_a) * (1.0 + a * a)
    scale = jnp.where(y > 0.0, y * lax.rsqrt(y), 0.0)
    hx = 0.5 * xc
    return a, scale * (hx + hx * jnp.tanh(0.5 * yx))


def _lru_gate_block(xc, wa, ba, wx, bx, half_c_sp):
    xb = xc.astype(bf16)
    return _lru_gate_math(xc, _dot(xb, wa) + ba, _dot(xb, wx) + bx, half_c_sp)


def _lru_gates(xc, wa_ref, ba, wx_ref, bx, half_c_sp):
    parts = []
    for n in range(LRU_BLOCKS):
        sl = slice(n * LRU_BD, (n + 1) * LRU_BD)
        parts.append(_lru_gate_block(xc[:, sl], wa_ref[n], ba[:, sl], wx_ref[n], bx[:, sl], half_c_sp[:, sl]))
    return jnp.concatenate([p[0] for p in parts], axis=1), jnp.concatenate([p[1] for p in parts], axis=1)


def _strided_rows(g, rows):
    j_count = rows // SUBLANES
    k, j0 = divmod(SUBLANES * g, j_count)
    return pl.ds(SUBLANES * j0 + k, SUBLANES, stride=SUBLANES)


def _shift_time(xp, tail, d):
    rows = xp.shape[0]
    row8 = lax.broadcasted_iota(jnp.int32, (SUBLANES, 1), 0)
    last = pltpu.roll(xp[rows - SUBLANES * d:rows, :].reshape(d, SUBLANES, xp.shape[1]), 1, 1)
    head = [jnp.where(row8 == 0, tail[SUBLANES - d + j:SUBLANES - d + j + 1, :], last[j]) for j in range(d)]
    return jnp.concatenate(head + [xp[0:rows - SUBLANES * d, :]], axis=0)


def _proj_lru_scan_body(xn_ref, w_ref, mb_ref, cw_ref, cb_ref, wa_ref, ba_ref, wx_ref, bx_ref, lam_ref,
                        o_ref, gs_ref, h_ref, cv_ref, xp_scr, hp_scr, ap_scr, tail_scr, h_scr):
    ti = pl.program_id(1)
    tm = xn_ref.shape[0]
    nj = tm // SUBLANES

    @pl.when(ti == 0)
    def _():
        tail_scr[...] = jnp.zeros(tail_scr.shape, f32)
        h_scr[...] = jnp.zeros(h_scr.shape, f32)

    xn = xn_ref[...]
    for c in range(LRU_W // 256):
        z = _dot(xn, w_ref[:, c * 256:(c + 1) * 256])
        for half in range(2):
            for g in range(nj):
                xp_scr[2 * c + half, _strided_rows(g, tm), :] = (
                    z[SUBLANES * g:SUBLANES * (g + 1), half * LRU_BD:(half + 1) * LRU_BD])
    cw = cw_ref[...]
    cb = cb_ref[...]
    ba = ba_ref[...]
    bx = bx_ref[...]
    half_c_sp = (0.5 * LRU_C) * _softplus(-lam_ref[...])
    n_gate = N_BRANCH * D_MODEL // 256
    n_pair = LRU_W // 256
    for p in range(n_pair):
        blocks = (2 * p, 2 * p + 1)
        sls = [slice(n * LRU_BD, (n + 1) * LRU_BD) for n in blocks]
        xcs, yas, yxs = [], [], []
        for n, sl in zip(blocks, sls):
            xp = xp_scr[n]
            tail = tail_scr[:, sl]
            xc = cb[:, sl] + xp * cw[CONV_W - 1:CONV_W, sl]
            for d in range(1, CONV_W):
                xc = xc + _shift_time(xp, tail, d) * cw[CONV_W - 1 - d:CONV_W - d, sl]
            xb = xc.astype(bf16)
            xcs.append(xc)
            yas.append(_dot(xb, wa_ref[n]) + ba[:, sl])
            yxs.append(_dot(xb, wx_ref[n]) + bx[:, sl])
        hseg = [jnp.zeros((SUBLANES, LRU_BD), f32) for _ in blocks]
        aseg = [jnp.ones((SUBLANES, LRU_BD), f32) for _ in blocks]
        for j in range(nj):
            rows = slice(SUBLANES * j, SUBLANES * (j + 1))
            for i, (n, sl) in enumerate(zip(blocks, sls)):
                a, u = _lru_gate_math(xcs[i][rows, :], yas[i][rows, :], yxs[i][rows, :], half_c_sp[:, sl])
                hseg[i] = a * hseg[i] + u
                aseg[i] = a * aseg[i]
                hp_scr[n, rows, :] = hseg[i]
                ap_scr[n, rows, :] = aseg[i]
        for i, (n, sl) in enumerate(zip(blocks, sls)):
            s = h_scr[0:1, sl]
            starts = []
            for k in range(SUBLANES):
                starts.append(s)
                s = aseg[i][k:k + 1, :] * s + hseg[i][k:k + 1, :]
            h_scr[:, sl] = jnp.broadcast_to(s, (h_scr.shape[0], LRU_BD))
            start8 = jnp.concatenate(starts, axis=0)
            hp_scr[n] = (hp_scr[n].reshape(nj, SUBLANES, LRU_BD)
                         + ap_scr[n].reshape(nj, SUBLANES, LRU_BD) * start8).reshape(tm, LRU_BD)
            tail_scr[:, sl] = xp_scr[n, _strided_rows(nj - 1, tm), :]
        zg = jax.nn.gelu(_dot(xn, w_ref[:, LRU_W + p * 256:LRU_W + (p + 1) * 256]), approximate=True)
        for half in range(2):
            n = 2 * p + half
            hs = jnp.concatenate([hp_scr[n, _strided_rows(g, tm), :] for g in range(nj)], axis=0)
            o_ref[:, n * LRU_BD:(n + 1) * LRU_BD] = (hs * zg[:, half * LRU_BD:(half + 1) * LRU_BD]).astype(bf16)
        _merge_gates(xn, w_ref, mb_ref, gs_ref, range(p * n_gate // n_pair, (p + 1) * n_gate // n_pair))

    @pl.when(ti == pl.num_programs(1) - 1)
    def _():
        h_ref[0] = h_scr[0:1, :]
        cv_ref[0] = tail_scr[SUBLANES - (CONV_W - 1):SUBLANES, :]


def _proj_lru_scan(xn, w, merge_bias, cw, cb, wa, ba, wx, bx, lam, bsz, t):
    tm = min(LRU_TILE, t)
    assert tm % (SUBLANES * SUBLANES) == 0
    nt = t // tm
    ng = N_BRANCH * D_MODEL
    row = lambda w_: pl.BlockSpec((tm, w_), lambda b, j: (b * nt + j, 0))
    vec = _resident((1, LRU_W))
    o, gs, h, cv = pl.pallas_call(
        _proj_lru_scan_body,
        grid=(bsz, nt),
        in_specs=[row(D_MODEL), _resident(w.shape), _resident((1, ng)),
                  _resident((CONV_W, LRU_W)), vec, _resident(wa.shape), vec, _resident(wx.shape), vec, vec],
        out_specs=[row(LRU_W), row(ng), pl.BlockSpec((1, 1, LRU_W), lambda b, j: (b, 0, 0)),
                   pl.BlockSpec((1, CONV_W - 1, LRU_W), lambda b, j: (b, 0, 0))],
        out_shape=[jax.ShapeDtypeStruct((bsz * t, LRU_W), bf16), jax.ShapeDtypeStruct((bsz * t, ng), bf16),
                   jax.ShapeDtypeStruct((bsz, 1, LRU_W), f32), jax.ShapeDtypeStruct((bsz, CONV_W - 1, LRU_W), f32)],
        scratch_shapes=[pltpu.VMEM((LRU_BLOCKS, tm, LRU_BD), f32)] * 3 + [pltpu.VMEM((SUBLANES, LRU_W), f32)] * 2,
        compiler_params=_cparams(("parallel", "arbitrary")),
        name="proj_lru_scan",
    )(xn, w, merge_bias.reshape(1, -1), cw, cb.reshape(1, -1), wa, ba.reshape(1, -1), wx,
      bx.reshape(1, -1), lam.reshape(1, -1))
    return o, gs, h[:, 0], cv


def _ret_sample_tables(t):
    log_g = _ret_log_gamma()
    r = jnp.arange(SAMPLE_BB * t)
    tt = (r % t).astype(f32)
    same = (r[:, None] // t) == (r[None, :] // t)
    diff = tt[:, None] - tt[None, :]
    ok = same & (diff >= 0)
    dm = jnp.where(ok, jnp.exp(log_g[:, None, None] * jnp.where(ok, diff, 0.0)), 0.0)
    qd = jnp.exp(log_g[:, None] * (tt + 1.0))[:, :, None]
    kd = jnp.exp(log_g[:, None] * (t - 1.0 - tt))[:, :, None]
    sd = jnp.exp(log_g * t)
    return dm, qd, kd, sd


def _stacked_state(layer, depth, bsz, heads_shape):
    shape = jax.ShapeDtypeStruct((depth, bsz) + heads_shape, f32)
    if layer == 0:
        spec = pl.BlockSpec((depth, SAMPLE_BB) + heads_shape, lambda i: (0, i, 0, 0, 0))

        def put(so_ref, b, h, val):
            so_ref[0, b, h] = val

        def finish(so_ref):
            for l in range(1, depth):
                so_ref[l] = jnp.zeros((SAMPLE_BB,) + heads_shape, f32)
    else:
        spec = pl.BlockSpec((None, SAMPLE_BB) + heads_shape, lambda i: (layer, i, 0, 0, 0))

        def put(so_ref, b, h, val):
            so_ref[b, h] = val

        def finish(so_ref):
            pass
    return shape, spec, put, finish


def _ret_sample_body(q_ref, k_ref, v_ref, g_ref, s_ref, dm_ref, qd_ref, kd_ref, sd_ref, *rest, t, put, finish):
    o_ref, so_ref = rest[-2:]
    bb = SAMPLE_BB
    finish(so_ref)
    for h in range(RET_HEADS):
        ks = slice(h * RET_DK, (h + 1) * RET_DK)
        vs = slice(h * RET_DV, (h + 1) * RET_DV)
        q = q_ref[:, ks]
        k = k_ref[:, ks]
        v = v_ref[:, vs]
        state = s_ref[:, h]
        scores = (_dot_tb(q, k) * dm_ref[h]).astype(bf16)
        inter = _dot(_block_diag(q, bb, t), state.reshape(bb * RET_DK, RET_DV).astype(bf16))
        o = _dot(scores, v) + qd_ref[h] * inter
        kd = (k.astype(f32) * kd_ref[h]).astype(bf16)
        ds = _dot_ta(kd, _block_diag(v, bb, t))
        for b in range(bb):
            put(so_ref, b, h, state[b] * sd_ref[h] + ds[:, b * RET_DV:(b + 1) * RET_DV])
        o_ref[:, vs] = _head_rms_gate(o, g_ref[:, vs].astype(f32)).astype(bf16)


def _ret_sample(q, k, v, g, state, stack, layer, bsz, t):
    nqk, nv = RET_HEADS * RET_DK, RET_HEADS * RET_DV
    bb = SAMPLE_BB
    heads_shape = (RET_HEADS, RET_DK, RET_DV)
    dm, qd, kd, s_dec = _ret_sample_tables(t)
    sd = jnp.broadcast_to(s_dec[:, None, None], (RET_HEADS, 1, RET_DV))
    row = lambda w_: pl.BlockSpec((bb * t, w_), lambda i: (i, 0))
    st_shape, st_spec, put, finish = _stacked_state(layer, state.shape[0], bsz, heads_shape)
    args = [q, k, v, g, state, dm, qd, kd, sd]
    in_specs = [row(nqk), row(nqk), row(nv), row(nv),
                pl.BlockSpec((None, bb) + heads_shape, lambda i: (layer, i, 0, 0, 0)),
                _resident(dm.shape), _resident(qd.shape), _resident(kd.shape), _resident(sd.shape)]
    aliases = {}
    if stack is not None:
        aliases = {len(args): 1}
        args.append(stack)
        in_specs.append(pl.BlockSpec(memory_space=pl.ANY))
    return pl.pallas_call(
        functools.partial(_ret_sample_body, t=t, put=put, finish=finish),
        grid=(bsz // bb,),
        in_specs=in_specs,
        out_specs=[row(nv), st_spec],
        out_shape=[jax.ShapeDtypeStruct((bsz * t, nv), bf16), st_shape],
        input_output_aliases=aliases,
        compiler_params=_cparams(("parallel",)),
        name="ret_sample",
    )(*args)


def _hgrn_sample_body(q_ref, lf_ref, i_ref, g_ref, nw_ref, s_ref, *rest, t, put, finish):
    assert t >= 2
    o_ref, so_ref = rest[-2:]
    finish(so_ref)
    bb = SAMPLE_BB
    n = bb * t
    tt = lax.broadcasted_iota(jnp.int32, (n, 1), 0) % t
    f = jnp.exp(lf_ref[...])
    kk = 1.0 - f
    q = q_ref[...].astype(f32)
    v = i_ref[...].astype(f32)
    before = [jnp.where(tt >= d, pltpu.roll(f, d, 0), 1.0) for d in range(1, t)]
    after = [jnp.where(tt < t - d, pltpu.roll(f, n - d, 0), 1.0) for d in range(1, t)]
    dec = [None, f]
    for d in range(2, t):
        dec.append(dec[-1] * before[d - 2])
    pre = dec[t - 1] * before[t - 2]
    suf = after[0]
    for d in range(1, t - 1):
        suf = suf * after[d]
    total = pre * suf
    qp = (q * pre).astype(bf16)
    kp = (kk * suf).astype(bf16)
    k_back = [kk] + [pltpu.roll(kk, d, 0) for d in range(1, t)]
    v_back = [v] + [pltpu.roll(v, d, 0) for d in range(1, t)]
    for h in range(HG_HEADS):
        sl = slice(h * HG_DK, (h + 1) * HG_DK)
        o = jnp.zeros((n, HG_DV), f32)
        for d in range(t):
            p = q[:, sl] * k_back[d][:, sl]
            if d > 0:
                p = p * dec[d][:, sl]
            a = jnp.sum(p, axis=-1, keepdims=True)
            if d > 0:
                a = jnp.where(tt >= d, a, 0.0)
            o = o + a * v_back[d][:, sl]
        state = s_ref[:, h]
        o = o + _dot(_block_diag(qp[:, sl], bb, t), state.reshape(bb * HG_DK, HG_DV).astype(bf16))
        ds = _dot_ta(kp[:, sl], _block_diag(i_ref[:, sl], bb, t))
        total_t = total[:, sl].T
        for b in range(bb):
            put(so_ref, b, h, state[b] * total_t[:, b * t:b * t + 1] + ds[:, b * HG_DV:(b + 1) * HG_DV])
        o_ref[:, sl] = _head_rms_gate(o, nw_ref[:, sl] * g_ref[:, sl].astype(f32)).astype(bf16)


def _hgrn_sample(q, lf, i, g, norm_w, state, stack, layer, bsz, t):
    hw = HG_HEADS * HG_DK
    bb = SAMPLE_BB
    heads_shape = (HG_HEADS, HG_DK, HG_DV)
    row = lambda: pl.BlockSpec((bb * t, hw), lambda j: (j, 0))
    st_shape, st_spec, put, finish = _stacked_state(layer, state.shape[0], bsz, heads_shape)
    args = [q, lf, i, g, norm_w.reshape(1, -1), state]
    in_specs = [row(), row(), row(), row(), _resident((1, hw)),
                pl.BlockSpec((None, bb) + heads_shape, lambda j: (layer, j, 0, 0, 0))]
    aliases = {}
    if stack is not None:
        aliases = {len(args): 1}
        args.append(stack)
        in_specs.append(pl.BlockSpec(memory_space=pl.ANY))
    return pl.pallas_call(
        functools.partial(_hgrn_sample_body, t=t, put=put, finish=finish),
        grid=(bsz // bb,),
        in_specs=in_specs,
        out_specs=[row(), st_spec],
        out_shape=[jax.ShapeDtypeStruct((bsz * t, hw), bf16), st_shape],
        input_output_aliases=aliases,
        compiler_params=_cparams(("parallel",)),
        name="hgrn_sample",
    )(*args)


def _lru_sample_body(x_ref, gl_ref, cs_ref, h0_ref, cw_ref, cb_ref, wa_ref, ba_ref, wx_ref, bx_ref, lam_ref,
                     o_ref, h_ref, cv_ref, *, t):
    xx = [cs_ref[j] for j in range(CONV_W - 1)] + [x_ref[j] for j in range(t)]
    cw = cw_ref[...]
    half_c_sp = (0.5 * LRU_C) * _softplus(-lam_ref[...])
    h = h0_ref[...]
    for j in range(t):
        xc = cb_ref[...]
        for i in range(CONV_W):
            xc = xc + xx[j + i] * cw[i:i + 1, :]
        a, u = _lru_gates(xc, wa_ref, ba_ref[...], wx_ref, bx_ref[...], half_c_sp)
        h = a * h + u
        o_ref[j] = (h * gl_ref[j].astype(f32)).astype(bf16)
    h_ref[...] = h
    for j in range(CONV_W - 1):
        cv_ref[j] = xx[t + j]


def _lru_sample(xl, gl, cs, h0, cw, cb, wa, ba, wx, bx, lam, bsz, t):
    full = lambda a: pl.BlockSpec(a.shape, lambda i: (0,) * a.ndim)
    args = (xl, gl, cs, h0, cw, cb.reshape(1, -1), wa, ba.reshape(1, -1), wx, bx.reshape(1, -1), lam.reshape(1, -1))
    return pl.pallas_call(
        functools.partial(_lru_sample_body, t=t),
        grid=(1,),
        in_specs=[full(a) for a in args],
        out_specs=[pl.BlockSpec((t, bsz, LRU_W), lambda i: (0, 0, 0)), pl.BlockSpec((bsz, LRU_W), lambda i: (0, 0)),
                   pl.BlockSpec((CONV_W - 1, bsz, LRU_W), lambda i: (0, 0, 0))],
        out_shape=[jax.ShapeDtypeStruct((t, bsz, LRU_W), bf16), jax.ShapeDtypeStruct((bsz, LRU_W), f32),
                   jax.ShapeDtypeStruct((CONV_W - 1, bsz, LRU_W), f32)],
        compiler_params=_cparams(("arbitrary",)),
        name="lru_sample",
    )(*args)


def _rope_tables(pos):
    inv = ROPE_BASE ** (-jnp.arange(0, RET_DK, 2, dtype=f32) / RET_DK)
    ang = pos.astype(f32)[:, None] * inv[None, :]
    cos, sin = jnp.cos(ang), jnp.sin(ang)
    return jnp.concatenate([cos, cos], axis=1), jnp.concatenate([-sin, sin], axis=1)


def _trunk(x, pos, w, states, bsz, t):
    cos, sin = _rope_tables(pos)
    tile_rows = min(TOKEN_TILE, bsz * t)
    if t < tile_rows:
        cos = jnp.tile(cos, (tile_rows // t, 1))
        sin = jnp.tile(sin, (tile_rows // t, 1))
    news = ([], [], [], [])
    ret_stack = hg_stack = None
    depth = w["w_ret"].shape[0]
    for l in range(depth):
        x = _ffn(x, w["ffn1_norm"][l], w["ffn1_wg"][l], w["ffn1_wu"][l], w["ffn1_wd"][l])
        q_r, k_r, v_r, g_r, xn = _proj_ret(x, w["mix_norm"][l], w["w_ret"][l], cos, sin)
        q_h, lf, i_h, g_h = _proj_hgrn(xn, w["w_hg"][l], w["hgrn_lb_logits"], l)
        lru_w = (w["conv_w"][l], w["conv_b"][l], w["lru_w_a"][l], w["lru_b_a"][l], w["lru_w_x"][l],
                 w["lru_b_x"][l], w["lru_lambda"][l])
        if states is None:
            o_r, ret_new = _ret_prompt(q_r, k_r, v_r, g_r, bsz, t)
            o_h, hg_new = _hgrn_prompt(q_h, lf, i_h, g_h, w["hgrn_norm"][l], bsz, t)
            o_l, gs, h_new, cv_new = _proj_lru_scan(xn, w["w_lru"][l], w["merge_bias"][l], *lru_w, bsz, t)
            news[0].append(ret_new)
            news[1].append(hg_new)
        else:
            st_ret, st_hg, st_lru, st_conv = states
            o_r, ret_stack = _ret_sample(q_r, k_r, v_r, g_r, st_ret, ret_stack, l, bsz, t)
            o_h, hg_stack = _hgrn_sample(q_h, lf, i_h, g_h, w["hgrn_norm"][l], st_hg, hg_stack, l, bsz, t)
            x_l, g_l, gs = _proj_lru(xn, w["w_lru"][l], w["merge_bias"][l])
            tm = lambda a: a.reshape(bsz, t, LRU_W).transpose(1, 0, 2)
            o_lt, h_new, cv_t = _lru_sample(tm(x_l), tm(g_l), st_conv[l].transpose(1, 0, 2), st_lru[l], *lru_w, bsz, t)
            o_l = o_lt.transpose(1, 0, 2).reshape(bsz * t, LRU_W)
            cv_new = cv_t.transpose(1, 0, 2)
        x = _mix_out(x, o_r, o_h, o_l, gs, w["w_ret_o"][l], w["w_hgrn_o"][l], w["w_lru_o"][l], w["w_mix_out"][l])
        last = l == depth - 1
        x = _ffn(x, w["ffn2_norm"][l], w["ffn2_wg"][l], w["ffn2_wu"][l], w["ffn2_wd"][l],
                 final_g=w["final_norm"] if last else None)
        news[2].append(h_new)
        news[3].append(cv_new)
    if states is None:
        return x, [jnp.stack(a, axis=0) for a in news]
    return x, [ret_stack, hg_stack, jnp.stack(news[2], axis=0), jnp.stack(news[3], axis=0)]


def kernel(x_prompt, x_sample, state_ret, state_hgrn, state_lru, state_conv, ffn1_norm, ffn1_w_gate, ffn1_w_up, ffn1_w_down, mix_norm, w_in, merge_bias, w_ret_o, hgrn_lb_logits, hgrn_norm, w_hgrn_o, conv_w, conv_b, lru_w_a, lru_b_a, lru_w_x, lru_b_x, lru_lambda, w_lru_o, w_mix_out, ffn2_norm, ffn2_w_gate, ffn2_w_up, ffn2_w_down, final_norm):
    c16 = lambda a: a.astype(bf16)
    w = dict(
        ffn1_norm=ffn1_norm, ffn1_wg=c16(ffn1_w_gate), ffn1_wu=c16(ffn1_w_up), ffn1_wd=c16(ffn1_w_down),
        mix_norm=mix_norm,
        w_ret=c16(w_in[:, :, :RET_COLS]), w_hg=c16(w_in[:, :, RET_COLS:RET_COLS + HG_COLS]),
        w_lru=c16(w_in[:, :, RET_COLS + HG_COLS:]),
        merge_bias=merge_bias, w_ret_o=c16(w_ret_o), hgrn_lb_logits=hgrn_lb_logits, hgrn_norm=hgrn_norm,
        w_hgrn_o=c16(w_hgrn_o), conv_w=conv_w, conv_b=conv_b, lru_w_a=c16(lru_w_a), lru_b_a=lru_b_a,
        lru_w_x=c16(lru_w_x), lru_b_x=lru_b_x, lru_lambda=lru_lambda, w_lru_o=c16(w_lru_o),
        w_mix_out=c16(w_mix_out), ffn2_norm=ffn2_norm, ffn2_wg=c16(ffn2_w_gate), ffn2_wu=c16(ffn2_w_up),
        ffn2_wd=c16(ffn2_w_down), final_norm=final_norm,
    )
    bp, tp, _ = x_prompt.shape
    bs, ts, _ = x_sample.shape
    y_p, st_p = _trunk(x_prompt.reshape(bp * tp, D_MODEL), jnp.arange(tp), w, None, bp, tp)
    y_s, st_s = _trunk(x_sample.reshape(bs * ts, D_MODEL), PAST_LEN + jnp.arange(ts), w,
                       (state_ret, state_hgrn, state_lru, state_conv), bs, ts)
    return (y_p.reshape(bp, tp, D_MODEL), y_s.reshape(bs, ts, D_MODEL),
            st_p[0], st_s[0], st_p[1], st_s[1], st_p[2], st_s[2], st_p[3], st_s[3])
```

```python
import functools

import jax
import jax.numpy as jnp
import numpy as np
from jax import lax
from jax.experimental import pallas as pl
from jax.experimental.pallas import tpu as pltpu

f32 = jnp.float32
bf16 = jnp.bfloat16

D_MODEL = 1024
FFN_HIDDEN = 2816
RET_HEADS, RET_DK, RET_DV = 4, 128, 256
HG_HEADS, HG_DK, HG_DV = 8, 128, 128
LRU_W, LRU_BLOCKS, LRU_BD = 1024, 8, 128
CONV_W = 4
LRU_C = 8.0
N_BRANCH = 3
ROPE_BASE = 10000.0
PAST_LEN = 16384
LOG2_E = 1.4426950408889634
EPS = 1e-6
RET_COLS = 2 * RET_HEADS * RET_DK + 2 * RET_HEADS * RET_DV
HG_COLS = 4 * HG_HEADS * HG_DK
LRU_COLS = 2 * LRU_W + N_BRANCH * D_MODEL

V7X_VMEM_BYTES = 64 * 1024 * 1024
VMEM_LIMIT = 56 * 1024 * 1024
SUBLANES = 8

TOKEN_TILE = 1024
LRU_TILE = 512
FFN_CHUNK = 256
RET_CHUNK = 256
HG_CHUNK = 128
HG_STEP_CHUNKS = 4
RET_STEP_CHUNKS = 4
HG_LEVELS = ((128, 64), (64, 32), (32, 16), (16, 8), (8, 4), (4, 2))
SAMPLE_BB = 8


def _cparams(sem):
    return pltpu.CompilerParams(dimension_semantics=sem, vmem_limit_bytes=VMEM_LIMIT)


def _resident(shape):
    nd = len(shape)
    return pl.BlockSpec(shape, lambda *_: (0,) * nd, pipeline_mode=pl.Buffered(1))


def _rmsnorm(x, g):
    return x * lax.rsqrt(jnp.mean(x * x, axis=-1, keepdims=True) + EPS) * g


def _dot(a, b):
    return jnp.dot(a, b, preferred_element_type=f32)


def _dot_tb(a, b):
    return lax.dot_general(a, b, (((1,), (1,)), ((), ())), preferred_element_type=f32)


def _dot_ta(a, b):
    return lax.dot_general(a, b, (((0,), (0,)), ((), ())), preferred_element_type=f32)


def _silu_of_half(hx):
    return hx + hx * jnp.tanh(hx)


def _gelu_of_half(hx):
    c = float(np.float32(np.sqrt(2.0 / np.pi)))
    return hx + hx * jnp.tanh(hx * (2.0 * c + (8.0 * c * 0.044715) * (hx * hx)))


def _softplus(x):
    return jnp.maximum(x, 0.0) + jnp.log1p(jnp.exp(-jnp.abs(x)))


def _block_diag(x, nb, rows_per):
    r, w = x.shape
    rb = lax.broadcasted_iota(jnp.int32, (r, nb * w), 0) // rows_per
    cb = lax.broadcasted_iota(jnp.int32, (r, nb * w), 1) // w
    return jnp.where(rb == cb, jnp.tile(x, (1, nb)), jnp.zeros((), x.dtype))


def _ffn_body(x_ref, g_ref, wg_ref, wu_ref, wd_ref, fg_ref, o_ref, *, final):
    x = x_ref[...]
    xn = _rmsnorm(x, g_ref[...]).astype(bf16)
    acc = jnp.zeros(x.shape, f32)
    for c in range(FFN_HIDDEN // FFN_CHUNK):
        sl = slice(c * FFN_CHUNK, (c + 1) * FFN_CHUNK)
        g = _dot(xn, wg_ref[:, sl])
        u = _dot(xn, wu_ref[:, sl])
        h = (_silu_of_half(g) * u).astype(bf16)
        acc = acc + _dot(h, wd_ref[sl, :])
    y = x + 0.5 * acc
    if final:
        y = _rmsnorm(y, fg_ref[...])
    o_ref[...] = y


def _ffn(x, g, wg, wu, wd, final_g=None):
    n = x.shape[0]
    tm = min(TOKEN_TILE, n)
    final = final_g is not None
    fg = final_g if final else g
    row = pl.BlockSpec((tm, D_MODEL), lambda i: (i, 0))
    return pl.pallas_call(
        functools.partial(_ffn_body, final=final),
        grid=(n // tm,),
        in_specs=[row, _resident((1, D_MODEL)), _resident(wg.shape), _resident(wu.shape),
                  _resident(wd.shape), _resident((1, D_MODEL))],
        out_specs=row,
        out_shape=jax.ShapeDtypeStruct((n, D_MODEL), f32),
        compiler_params=_cparams(("parallel",)),
        name="ffn",
    )(x, g.reshape(1, -1), wg, wu, wd, fg.reshape(1, -1))


def _proj_ret_body(x_ref, g_ref, w_ref, cos_ref, sin_ref, q_ref, k_ref, v_ref, gr_ref, xn_ref):
    xn = _rmsnorm(x_ref[...], g_ref[...]).astype(bf16)
    xn_ref[...] = xn
    cos = cos_ref[...]
    sin = sin_ref[...]
    nqk = RET_HEADS * RET_DK

    def rot(z):
        return z * cos + pltpu.roll(z, RET_DK // 2, 1) * sin

    for c in range(nqk // 256):
        zq = _dot(xn, w_ref[:, c * 256:(c + 1) * 256])
        zk = _dot(xn, w_ref[:, nqk + c * 256:nqk + (c + 1) * 256])
        for j in range(2):
            sl = slice(c * 256 + j * 128, c * 256 + (j + 1) * 128)
            q_ref[:, sl] = rot(zq[:, j * 128:(j + 1) * 128]).astype(bf16)
            k_ref[:, sl] = (rot(zk[:, j * 128:(j + 1) * 128]) * (RET_DK ** -0.5)).astype(bf16)
    nv = RET_HEADS * RET_DV
    for c in range(nv // 256):
        sl = slice(c * 256, (c + 1) * 256)
        v_ref[:, sl] = _dot(xn, w_ref[:, 2 * nqk + c * 256:2 * nqk + (c + 1) * 256]).astype(bf16)
        zg = _dot(xn, w_ref[:, 2 * nqk + nv + c * 256:2 * nqk + nv + (c + 1) * 256])
        gr_ref[:, sl] = _silu_of_half(zg).astype(bf16)


def _proj_ret(x, g, w, cos, sin):
    n = x.shape[0]
    tm = min(TOKEN_TILE, n)
    row = lambda w_: pl.BlockSpec((tm, w_), lambda i: (i, 0))
    nqk, nv = RET_HEADS * RET_DK, RET_HEADS * RET_DV
    assert cos.shape[0] % tm == 0
    period = cos.shape[0] // tm
    rope = pl.BlockSpec((tm, RET_DK), lambda i: (i % period, 0))
    return pl.pallas_call(
        _proj_ret_body,
        grid=(n // tm,),
        in_specs=[row(D_MODEL), _resident((1, D_MODEL)), _resident(w.shape), rope, rope],
        out_specs=[row(nqk), row(nqk), row(nv), row(nv), row(D_MODEL)],
        out_shape=[jax.ShapeDtypeStruct((n, nqk), bf16), jax.ShapeDtypeStruct((n, nqk), bf16),
                   jax.ShapeDtypeStruct((n, nv), bf16), jax.ShapeDtypeStruct((n, nv), bf16),
                   jax.ShapeDtypeStruct((n, D_MODEL), bf16)],
        compiler_params=_cparams(("parallel",)),
        name="proj_ret",
    )(x, g.reshape(1, -1), w, cos, sin)


def _proj_hgrn_body(xn_ref, w_ref, lbl_ref, q_ref, lf_ref, i_ref, gh_ref, *, layer):
    xn = xn_ref[...]
    hw = HG_HEADS * HG_DK
    if layer > 0:
        lg = lbl_ref[...]
        e = jnp.exp(lg - jnp.max(lg, axis=0, keepdims=True))
        sm = e / jnp.sum(e, axis=0, keepdims=True)
        lb = jnp.sum(sm[1:layer + 1], axis=0, keepdims=True)
    for c in range(hw // 256):
        sl = slice(c * 256, (c + 1) * 256)
        q_ref[:, sl] = _silu_of_half(_dot(xn, w_ref[:, sl])).astype(bf16)
        zf = _dot(xn, w_ref[:, hw + c * 256:hw + (c + 1) * 256])
        t = jnp.exp2(jnp.abs(zf) * -LOG2_E)
        num = jnp.minimum(zf, 0.0)
        if layer > 0:
            lbc = lb[:, sl]
            num = jnp.maximum(jnp.log(jnp.where(zf < 0.0, lbc + t, 1.0 + lbc * t)), num)
        lf_ref[:, sl] = num - jnp.log(1.0 + t)
        i_ref[:, sl] = _dot(xn, w_ref[:, 2 * hw + c * 256:2 * hw + (c + 1) * 256]).astype(bf16)
        gh_ref[:, sl] = _silu_of_half(_dot(xn, w_ref[:, 3 * hw + c * 256:3 * hw + (c + 1) * 256])).astype(bf16)


def _proj_hgrn(xn, w, lb_logits, layer):
    n = xn.shape[0]
    tm = min(TOKEN_TILE, n)
    hw = HG_HEADS * HG_DK
    row = lambda w_: pl.BlockSpec((tm, w_), lambda i: (i, 0))
    return pl.pallas_call(
        functools.partial(_proj_hgrn_body, layer=layer),
        grid=(n // tm,),
        in_specs=[row(D_MODEL), _resident(w.shape), _resident(lb_logits.shape)],
        out_specs=[row(hw), row(hw), row(hw), row(hw)],
        out_shape=[jax.ShapeDtypeStruct((n, hw), bf16), jax.ShapeDtypeStruct((n, hw), f32),
                   jax.ShapeDtypeStruct((n, hw), bf16), jax.ShapeDtypeStruct((n, hw), bf16)],
        compiler_params=_cparams(("parallel",)),
        name="proj_hgrn",
    )(xn, w, lb_logits)


def _merge_gates(xn, w_ref, mb_ref, gs_ref, chunks=range(N_BRANCH * D_MODEL // 256)):
    for c in chunks:
        sl = slice(c * 256, (c + 1) * 256)
        half_z = _dot(xn, w_ref[:, 2 * LRU_W + c * 256:2 * LRU_W + (c + 1) * 256]) + mb_ref[:, sl]
        gs_ref[:, sl] = (0.5 + 0.5 * jnp.tanh(half_z)).astype(bf16)


def _proj_lru_body(xn_ref, w_ref, mb_ref, xl_ref, gl_ref, gs_ref):
    xn = xn_ref[...]
    for c in range(LRU_W // 256):
        sl = slice(c * 256, (c + 1) * 256)
        xl_ref[:, sl] = _dot(xn, w_ref[:, sl])
        zg = _dot(xn, w_ref[:, LRU_W + c * 256:LRU_W + (c + 1) * 256])
        gl_ref[:, sl] = _gelu_of_half(zg).astype(bf16)
    _merge_gates(xn, w_ref, mb_ref, gs_ref)


def _proj_lru(xn, w, merge_bias):
    n = xn.shape[0]
    tm = min(TOKEN_TILE, n)
    row = lambda w_: pl.BlockSpec((tm, w_), lambda i: (i, 0))
    ng = N_BRANCH * D_MODEL
    return pl.pallas_call(
        _proj_lru_body,
        grid=(n // tm,),
        in_specs=[row(D_MODEL), _resident(w.shape), _resident((1, ng))],
        out_specs=[row(LRU_W), row(LRU_W), row(ng)],
        out_shape=[jax.ShapeDtypeStruct((n, LRU_W), f32), jax.ShapeDtypeStruct((n, LRU_W), bf16),
                   jax.ShapeDtypeStruct((n, ng), bf16)],
        compiler_params=_cparams(("parallel",)),
        name="proj_lru",
    )(xn, w, merge_bias.reshape(1, -1))


def _mix_out_body(x_ref, or_ref, oh_ref, ol_ref, gs_ref, wr_ref, wh_ref, wl_ref, wo_ref, o_ref):
    d = D_MODEL
    merged = gs_ref[:, 0:d].astype(f32) * _dot(or_ref[...], wr_ref[...])
    merged = merged + gs_ref[:, d:2 * d].astype(f32) * _dot(oh_ref[...], wh_ref[...])
    merged = merged + gs_ref[:, 2 * d:3 * d].astype(f32) * _dot(ol_ref[...], wl_ref[...])
    o_ref[...] = x_ref[...] + _dot(merged.astype(bf16), wo_ref[...])


def _mix_out(x, o_r, o_h, o_l, gs, wr, wh, wl, wo):
    n = x.shape[0]
    tm = min(TOKEN_TILE, n)
    row = lambda w_: pl.BlockSpec((tm, w_), lambda i: (i, 0))
    return pl.pallas_call(
        _mix_out_body,
        grid=(n // tm,),
        in_specs=[row(D_MODEL), row(D_MODEL), row(D_MODEL), row(D_MODEL), row(N_BRANCH * D_MODEL),
                  _resident(wr.shape), _resident(wh.shape), _resident(wl.shape), _resident(wo.shape)],
        out_specs=row(D_MODEL),
        out_shape=jax.ShapeDtypeStruct((n, D_MODEL), f32),
        compiler_params=_cparams(("parallel",)),
        name="mix_out",
    )(x, o_r, o_h, o_l, gs, wr, wh, wl, wo)


def _ret_log_gamma():
    return jnp.log1p(-jnp.exp2(-5.0 - jnp.arange(RET_HEADS, dtype=f32)))


def _ret_tables(c):
    log_g = _ret_log_gamma()
    idx = jnp.arange(c, dtype=f32)
    diff = idx[:, None] - idx[None, :]
    causal = diff >= 0
    d_intra = jnp.where(causal, jnp.exp(log_g[:, None, None] * jnp.where(causal, diff, 0.0)), 0.0)
    q_dec = jnp.exp(log_g[:, None] * (idx + 1.0))[:, :, None]
    k_dec = jnp.exp(log_g[:, None] * (c - 1.0 - idx))[:, :, None]
    s_dec = jnp.exp(log_g * c)
    return d_intra, q_dec, k_dec, s_dec


def _head_rms_gate(o, gate):
    return o * lax.rsqrt(jnp.mean(o * o, axis=-1, keepdims=True) + EPS) * gate


def _ret_prompt_body(q_ref, k_ref, v_ref, g_ref, dm_ref, qd_ref, kd_ref, sd_ref, o_ref, st_ref, s_scr):
    c = pl.program_id(1)

    @pl.when(c == 0)
    def _():
        s_scr[...] = jnp.zeros(s_scr.shape, f32)

    heads = range(RET_HEADS)
    ks = [slice(h * RET_DK, (h + 1) * RET_DK) for h in heads]
    vs = [slice(h * RET_DV, (h + 1) * RET_DV) for h in heads]
    for r0 in range(0, q_ref.shape[0], RET_CHUNK):
        rows = slice(r0, r0 + RET_CHUNK)
        scores = [(_dot_tb(q_ref[rows, ks[h]], k_ref[rows, ks[h]]) * dm_ref[h]).astype(bf16) for h in heads]
        inter = [qd_ref[h] * _dot(q_ref[rows, ks[h]], s_scr[h].astype(bf16)) for h in heads]
        outs = [_dot(scores[h], v_ref[rows, vs[h]]) + inter[h] for h in heads]
        for h in heads:
            kd = (k_ref[rows, ks[h]].astype(f32) * kd_ref[h]).astype(bf16)
            s_scr[h] = s_scr[h] * sd_ref[h] + _dot_ta(kd, v_ref[rows, vs[h]])
        for h in heads:
            o_ref[rows, vs[h]] = _head_rms_gate(outs[h], g_ref[rows, vs[h]].astype(f32)).astype(bf16)

    @pl.when(c == pl.num_programs(1) - 1)
    def _():
        st_ref[0] = s_scr[...]


def _ret_prompt(q, k, v, g, bsz, t):
    c = RET_CHUNK
    rows = RET_STEP_CHUNKS * c
    nc = t // rows
    nqk, nv = RET_HEADS * RET_DK, RET_HEADS * RET_DV
    d_intra, q_dec, k_dec, s_dec = _ret_tables(c)
    sd = jnp.broadcast_to(s_dec[:, None, None], (RET_HEADS, 1, RET_DV))
    row = lambda w_: pl.BlockSpec((rows, w_), lambda b, i: (b * nc + i, 0))
    return pl.pallas_call(
        _ret_prompt_body,
        grid=(bsz, nc),
        in_specs=[row(nqk), row(nqk), row(nv), row(nv), _resident(d_intra.shape), _resident(q_dec.shape),
                  _resident(k_dec.shape), _resident(sd.shape)],
        out_specs=[row(nv), pl.BlockSpec((1, RET_HEADS, RET_DK, RET_DV), lambda b, i: (b, 0, 0, 0))],
        out_shape=[jax.ShapeDtypeStruct((bsz * t, nv), bf16),
                   jax.ShapeDtypeStruct((bsz, RET_HEADS, RET_DK, RET_DV), f32)],
        scratch_shapes=[pltpu.VMEM((RET_HEADS, RET_DK, RET_DV), f32)],
        compiler_params=_cparams(("parallel", "arbitrary")),
        name="ret_prompt",
    )(q, k, v, g, d_intra, q_dec, k_dec, sd)


def _hg_level_masks(c):
    t = np.arange(c)[:, None]
    s = np.arange(c)[None, :]
    ms = [((t // blk == s // blk) & (t % blk >= half) & (s % blk < half)) for blk, half in HG_LEVELS]
    return jnp.asarray(np.stack(ms).astype(np.float32))


def _boundary_rows(b, blk, half):
    c, w = b.shape
    if half >= SUBLANES:
        return jnp.concatenate(
            [jnp.broadcast_to(b[j * blk + half - 1:j * blk + half, :], (blk, w)) for j in range(c // blk)], axis=0)
    b3 = b.reshape(c // SUBLANES, SUBLANES, w)
    row8 = lax.broadcasted_iota(jnp.int32, (1, SUBLANES, 1), 1)
    out = b3[:, half - 1:half, :]
    for j in range(1, SUBLANES // blk):
        out = jnp.where(row8 >= j * blk, b3[:, j * blk + half - 1:j * blk + half, :], out)
    return jnp.broadcast_to(out, b3.shape).reshape(c, w)


def _upper_lower(up, low, blk, half):
    c = up.shape[0]
    if half >= SUBLANES:
        parts = []
        for j in range(c // blk):
            parts += [low[j * blk:j * blk + half, :], up[j * blk + half:(j + 1) * blk, :]]
        return jnp.concatenate(parts, axis=0)
    upper = lax.broadcasted_iota(jnp.int32, (c, 1), 0) % blk >= half
    return jnp.where(upper, up, low)


def _hgrn_chunk(q_ref, lf_ref, i_ref, g_ref, nw_ref, tri_ref, mask_ref, o_ref, s_scr, rows):
    c = HG_CHUNK
    lf_all = lf_ref[rows, :]
    hi = lf_all.astype(bf16)
    lo = (lf_all - hi.astype(f32)).astype(bf16)
    tri = tri_ref[...]
    b2_all = (_dot(tri, hi) + _dot(tri, lo)) * LOG2_E
    odd = lax.broadcasted_iota(jnp.int32, (c, 1), 0) % 2 == 1

    f_all = jnp.exp(lf_all)
    kk_all = 1.0 - f_all
    q_all = q_ref[rows, :].astype(f32)
    a_mats = [None] * HG_HEADS
    for lev, (blk, half) in enumerate(HG_LEVELS):
        keep = mask_ref[lev] > 0.5
        for h in range(HG_HEADS):
            sl = slice(h * HG_DK, (h + 1) * HG_DK)
            b2 = b2_all[:, sl]
            bm = _boundary_rows(b2, blk, half)
            e = jnp.exp2(_upper_lower(b2 - bm, bm - b2, blk, half))
            m = (_upper_lower(q_all[:, sl], kk_all[:, sl], blk, half) * e).astype(bf16)
            p = _dot_tb(m, m)
            a_mats[h] = jnp.where(keep, p, 0.0) if lev == 0 else jnp.where(keep, p, a_mats[h])
    a_mats = [a.astype(bf16) for a in a_mats]

    for h in range(HG_HEADS):
        sl = slice(h * HG_DK, (h + 1) * HG_DK)
        b2 = b2_all[:, sl]
        q = q_all[:, sl]
        f = f_all[:, sl]
        kk = kk_all[:, sl]
        v16 = i_ref[rows, sl]
        o = _dot(a_mats[h], v16)
        vf = v16.astype(f32)
        a0 = jnp.sum(q * kk, axis=-1, keepdims=True)
        a1 = jnp.where(odd, jnp.sum(q * f * pltpu.roll(kk, 1, 0), axis=-1, keepdims=True), 0.0)
        o = o + a0 * vf + a1 * pltpu.roll(vf, 1, 0)
        st = s_scr[h]
        o = o + _dot_tb((q * jnp.exp2(b2)).astype(bf16), st.astype(bf16))
        bl2 = b2[c - 1:c, :]
        kdec = (kk * jnp.exp2(bl2 - b2)).astype(bf16)
        s_scr[h] = st * jnp.exp2(bl2) + _dot_ta(v16, kdec)
        o_ref[rows, sl] = _head_rms_gate(o, nw_ref[:, sl] * g_ref[rows, sl].astype(f32)).astype(bf16)


def _hgrn_prompt_body(q_ref, lf_ref, i_ref, g_ref, nw_ref, tri_ref, mask_ref, o_ref, st_ref, s_scr):
    ci = pl.program_id(1)
    c = HG_CHUNK

    @pl.when(ci == 0)
    def _():
        s_scr[...] = jnp.zeros(s_scr.shape, f32)

    for r0 in range(0, q_ref.shape[0], c):
        _hgrn_chunk(q_ref, lf_ref, i_ref, g_ref, nw_ref, tri_ref, mask_ref, o_ref, s_scr, slice(r0, r0 + c))

    @pl.when(ci == pl.num_programs(1) - 1)
    def _():
        for h in range(HG_HEADS):
            st_ref[0, h] = s_scr[h].T


def _hgrn_prompt(q, lf, i, g, norm_w, bsz, t):
    c = HG_CHUNK
    rows = HG_STEP_CHUNKS * c
    nc = t // rows
    hw = HG_HEADS * HG_DK
    tri = jnp.asarray(np.tril(np.ones((c, c), np.float32))).astype(bf16)
    masks = _hg_level_masks(c)
    row = lambda: pl.BlockSpec((rows, hw), lambda b, j: (b * nc + j, 0))
    return pl.pallas_call(
        _hgrn_prompt_body,
        grid=(bsz, nc),
        in_specs=[row(), row(), row(), row(), _resident((1, hw)), _resident(tri.shape), _resident(masks.shape)],
        out_specs=[row(), pl.BlockSpec((1, HG_HEADS, HG_DK, HG_DV), lambda b, j: (b, 0, 0, 0))],
        out_shape=[jax.ShapeDtypeStruct((bsz * t, hw), bf16),
                   jax.ShapeDtypeStruct((bsz, HG_HEADS, HG_DK, HG_DV), f32)],
        scratch_shapes=[pltpu.VMEM((HG_HEADS, HG_DV, HG_DK), f32)],
        compiler_params=_cparams(("parallel", "arbitrary")),
        name="hgrn_prompt",
    )(q, lf, i, g, norm_w.reshape(1, -1), tri, masks)


def _lru_gate_math(xc, half_ya, half_yx, half_c_sp):
    neg_log_a = half_c_sp + half_c_sp * jnp.tanh(half_ya)
    a = jnp.exp(-neg_log_a)
    y = jnp.tanh(neg_log_a) * (1.0 + a * a)
    scale = jnp.where(y > 0.0, y * lax.rsqrt(y), 0.0)
    hx = 0.5 * xc
    return a, scale * (hx + hx * jnp.tanh(half_yx))


def _lru_gate_block(xc, wa, ba, wx, bx, half_c_sp):
    xb = xc.astype(bf16)
    return _lru_gate_math(xc, _dot(xb, wa) + ba, _dot(xb, wx) + bx, half_c_sp)


def _lru_gates(xc, wa_ref, ba, wx_ref, bx, half_c_sp):
    parts = []
    for n in range(LRU_BLOCKS):
        sl = slice(n * LRU_BD, (n + 1) * LRU_BD)
        parts.append(_lru_gate_block(xc[:, sl], wa_ref[n], ba[:, sl], wx_ref[n], bx[:, sl], half_c_sp[:, sl]))
    return jnp.concatenate([p[0] for p in parts], axis=1), jnp.concatenate([p[1] for p in parts], axis=1)


def _strided_rows(g, rows):
    j_count = rows // SUBLANES
    k, j0 = divmod(SUBLANES * g, j_count)
    return pl.ds(SUBLANES * j0 + k, SUBLANES, stride=SUBLANES)


def _shift_time(xp, tail, d):
    rows = xp.shape[0]
    row8 = lax.broadcasted_iota(jnp.int32, (SUBLANES, 1), 0)
    last = pltpu.roll(xp[rows - SUBLANES * d:rows, :].reshape(d, SUBLANES, xp.shape[1]), 1, 1)
    head = [jnp.where(row8 == 0, tail[SUBLANES - d + j:SUBLANES - d + j + 1, :], last[j]) for j in range(d)]
    return jnp.concatenate(head + [xp[0:rows - SUBLANES * d, :]], axis=0)


def _proj_lru_scan_body(xn_ref, w_ref, mb_ref, cw_ref, cb_ref, wa_ref, ba_ref, wx_ref, bx_ref, lam_ref,
                        o_ref, gs_ref, h_ref, cv_ref, xp_scr, hp_scr, ap_scr, tail_scr, h_scr):
    ti = pl.program_id(1)
    tm = xn_ref.shape[0]
    nj = tm // SUBLANES

    @pl.when(ti == 0)
    def _():
        tail_scr[...] = jnp.zeros(tail_scr.shape, f32)
        h_scr[...] = jnp.zeros(h_scr.shape, f32)

    xn = xn_ref[...]
    for c in range(LRU_W // 256):
        z = _dot(xn, w_ref[:, c * 256:(c + 1) * 256])
        for half in range(2):
            for g in range(nj):
                xp_scr[2 * c + half, _strided_rows(g, tm), :] = (
                    z[SUBLANES * g:SUBLANES * (g + 1), half * LRU_BD:(half + 1) * LRU_BD])
    cw = cw_ref[...]
    cb = cb_ref[...]
    ba = ba_ref[...]
    bx = bx_ref[...]
    half_c_sp = (0.5 * LRU_C) * _softplus(-lam_ref[...])
    n_gate = N_BRANCH * D_MODEL // 256
    n_pair = LRU_W // 256
    for p in range(n_pair):
        blocks = (2 * p, 2 * p + 1)
        sls = [slice(n * LRU_BD, (n + 1) * LRU_BD) for n in blocks]
        xcs, yas, yxs = [], [], []
        for n, sl in zip(blocks, sls):
            xp = xp_scr[n]
            tail = tail_scr[:, sl]
            xc = cb[:, sl] + xp * cw[CONV_W - 1:CONV_W, sl]
            for d in range(1, CONV_W):
                xc = xc + _shift_time(xp, tail, d) * cw[CONV_W - 1 - d:CONV_W - d, sl]
            xb = xc.astype(bf16)
            xcs.append(xc)
            yas.append(_dot(xb, wa_ref[n]) + ba[:, sl])
            yxs.append(_dot(xb, wx_ref[n]) + bx[:, sl])
        hseg = [jnp.zeros((SUBLANES, LRU_BD), f32) for _ in blocks]
        aseg = [jnp.ones((SUBLANES, LRU_BD), f32) for _ in blocks]
        for j in range(nj):
            rows = slice(SUBLANES * j, SUBLANES * (j + 1))
            for i, (n, sl) in enumerate(zip(blocks, sls)):
                a, u = _lru_gate_math(xcs[i][rows, :], yas[i][rows, :], yxs[i][rows, :], half_c_sp[:, sl])
                hseg[i] = a * hseg[i] + u
                aseg[i] = a * aseg[i]
                hp_scr[n, rows, :] = hseg[i]
                ap_scr[n, rows, :] = aseg[i]
        for i, (n, sl) in enumerate(zip(blocks, sls)):
            s = h_scr[0:1, sl]
            starts = []
            for k in range(SUBLANES):
                starts.append(s)
                s = aseg[i][k:k + 1, :] * s + hseg[i][k:k + 1, :]
            h_scr[:, sl] = jnp.broadcast_to(s, (h_scr.shape[0], LRU_BD))
            start8 = jnp.concatenate(starts, axis=0)
            hp_scr[n] = (hp_scr[n].reshape(nj, SUBLANES, LRU_BD)
                         + ap_scr[n].reshape(nj, SUBLANES, LRU_BD) * start8).reshape(tm, LRU_BD)
            tail_scr[:, sl] = xp_scr[n, _strided_rows(nj - 1, tm), :]
        zg = _gelu_of_half(_dot(xn, w_ref[:, LRU_W + p * 256:LRU_W + (p + 1) * 256]))
        for half in range(2):
            n = 2 * p + half
            hs = jnp.concatenate([hp_scr[n, _strided_rows(g, tm), :] for g in range(nj)], axis=0)
            o_ref[:, n * LRU_BD:(n + 1) * LRU_BD] = (hs * zg[:, half * LRU_BD:(half + 1) * LRU_BD]).astype(bf16)
        _merge_gates(xn, w_ref, mb_ref, gs_ref, range(p * n_gate // n_pair, (p + 1) * n_gate // n_pair))

    @pl.when(ti == pl.num_programs(1) - 1)
    def _():
        h_ref[0] = h_scr[0:1, :]
        cv_ref[0] = tail_scr[SUBLANES - (CONV_W - 1):SUBLANES, :]


def _proj_lru_scan(xn, w, merge_bias, cw, cb, wa, ba, wx, bx, lam, bsz, t):
    tm = min(LRU_TILE, t)
    assert tm % (SUBLANES * SUBLANES) == 0
    nt = t // tm
    ng = N_BRANCH * D_MODEL
    row = lambda w_: pl.BlockSpec((tm, w_), lambda b, j: (b * nt + j, 0))
    vec = _resident((1, LRU_W))
    o, gs, h, cv = pl.pallas_call(
        _proj_lru_scan_body,
        grid=(bsz, nt),
        in_specs=[row(D_MODEL), _resident(w.shape), _resident((1, ng)),
                  _resident((CONV_W, LRU_W)), vec, _resident(wa.shape), vec, _resident(wx.shape), vec, vec],
        out_specs=[row(LRU_W), row(ng), pl.BlockSpec((1, 1, LRU_W), lambda b, j: (b, 0, 0)),
                   pl.BlockSpec((1, CONV_W - 1, LRU_W), lambda b, j: (b, 0, 0))],
        out_shape=[jax.ShapeDtypeStruct((bsz * t, LRU_W), bf16), jax.ShapeDtypeStruct((bsz * t, ng), bf16),
                   jax.ShapeDtypeStruct((bsz, 1, LRU_W), f32), jax.ShapeDtypeStruct((bsz, CONV_W - 1, LRU_W), f32)],
        scratch_shapes=[pltpu.VMEM((LRU_BLOCKS, tm, LRU_BD), f32)] * 3 + [pltpu.VMEM((SUBLANES, LRU_W), f32)] * 2,
        compiler_params=_cparams(("parallel", "arbitrary")),
        name="proj_lru_scan",
    )(xn, w, merge_bias.reshape(1, -1), cw, cb.reshape(1, -1), wa, ba.reshape(1, -1), wx,
      bx.reshape(1, -1), lam.reshape(1, -1))
    return o, gs, h[:, 0], cv


def _ret_sample_tables(t):
    log_g = _ret_log_gamma()
    r = jnp.arange(SAMPLE_BB * t)
    tt = (r % t).astype(f32)
    same = (r[:, None] // t) == (r[None, :] // t)
    diff = tt[:, None] - tt[None, :]
    ok = same & (diff >= 0)
    dm = jnp.where(ok, jnp.exp(log_g[:, None, None] * jnp.where(ok, diff, 0.0)), 0.0)
    qd = jnp.exp(log_g[:, None] * (tt + 1.0))[:, :, None]
    kd = jnp.exp(log_g[:, None] * (t - 1.0 - tt))[:, :, None]
    sd = jnp.exp(log_g * t)
    return dm, qd, kd, sd


def _stacked_state(layer, depth, bsz, heads_shape):
    shape = jax.ShapeDtypeStruct((depth, bsz) + heads_shape, f32)
    if layer == 0:
        spec = pl.BlockSpec((depth, SAMPLE_BB) + heads_shape, lambda i: (0, i, 0, 0, 0))

        def put(so_ref, b, h, val):
            so_ref[0, b, h] = val

        def finish(so_ref):
            for l in range(1, depth):
                so_ref[l] = jnp.zeros((SAMPLE_BB,) + heads_shape, f32)
    else:
        spec = pl.BlockSpec((None, SAMPLE_BB) + heads_shape, lambda i: (layer, i, 0, 0, 0))

        def put(so_ref, b, h, val):
            so_ref[b, h] = val

        def finish(so_ref):
            pass
    return shape, spec, put, finish


def _ret_sample_body(q_ref, k_ref, v_ref, g_ref, s_ref, dm_ref, qd_ref, kd_ref, sd_ref, *rest, t, put, finish):
    o_ref, so_ref = rest[-2:]
    bb = SAMPLE_BB
    finish(so_ref)
    for h in range(RET_HEADS):
        ks = slice(h * RET_DK, (h + 1) * RET_DK)
        vs = slice(h * RET_DV, (h + 1) * RET_DV)
        q = q_ref[:, ks]
        k = k_ref[:, ks]
        v = v_ref[:, vs]
        state = s_ref[:, h]
        scores = (_dot_tb(q, k) * dm_ref[h]).astype(bf16)
        inter = _dot(_block_diag(q, bb, t), state.reshape(bb * RET_DK, RET_DV).astype(bf16))
        o = _dot(scores, v) + qd_ref[h] * inter
        kd = (k.astype(f32) * kd_ref[h]).astype(bf16)
        ds = _dot_ta(kd, _block_diag(v, bb, t))
        for b in range(bb):
            put(so_ref, b, h, state[b] * sd_ref[h] + ds[:, b * RET_DV:(b + 1) * RET_DV])
        o_ref[:, vs] = _head_rms_gate(o, g_ref[:, vs].astype(f32)).astype(bf16)


def _ret_sample(q, k, v, g, state, stack, layer, bsz, t):
    nqk, nv = RET_HEADS * RET_DK, RET_HEADS * RET_DV
    bb = SAMPLE_BB
    heads_shape = (RET_HEADS, RET_DK, RET_DV)
    dm, qd, kd, s_dec = _ret_sample_tables(t)
    sd = jnp.broadcast_to(s_dec[:, None, None], (RET_HEADS, 1, RET_DV))
    row = lambda w_: pl.BlockSpec((bb * t, w_), lambda i: (i, 0))
    st_shape, st_spec, put, finish = _stacked_state(layer, state.shape[0], bsz, heads_shape)
    args = [q, k, v, g, state, dm, qd, kd, sd]
    in_specs = [row(nqk), row(nqk), row(nv), row(nv),
                pl.BlockSpec((None, bb) + heads_shape, lambda i: (layer, i, 0, 0, 0)),
                _resident(dm.shape), _resident(qd.shape), _resident(kd.shape), _resident(sd.shape)]
    aliases = {}
    if stack is not None:
        aliases = {len(args): 1}
        args.append(stack)
        in_specs.append(pl.BlockSpec(memory_space=pl.ANY))
    return pl.pallas_call(
        functools.partial(_ret_sample_body, t=t, put=put, finish=finish),
        grid=(bsz // bb,),
        in_specs=in_specs,
        out_specs=[row(nv), st_spec],
        out_shape=[jax.ShapeDtypeStruct((bsz * t, nv), bf16), st_shape],
        input_output_aliases=aliases,
        compiler_params=_cparams(("parallel",)),
        name="ret_sample",
    )(*args)


def _hgrn_sample_body(q_ref, lf_ref, i_ref, g_ref, nw_ref, s_ref, *rest, t, put, finish):
    assert t >= 2
    o_ref, so_ref = rest[-2:]
    finish(so_ref)
    bb = SAMPLE_BB
    n = bb * t
    tt = lax.broadcasted_iota(jnp.int32, (n, 1), 0) % t
    f = jnp.exp(lf_ref[...])
    kk = 1.0 - f
    q = q_ref[...].astype(f32)
    v = i_ref[...].astype(f32)
    before = [jnp.where(tt >= d, pltpu.roll(f, d, 0), 1.0) for d in range(1, t)]
    after = [jnp.where(tt < t - d, pltpu.roll(f, n - d, 0), 1.0) for d in range(1, t)]
    dec = [None, f]
    for d in range(2, t):
        dec.append(dec[-1] * before[d - 2])
    pre = dec[t - 1] * before[t - 2]
    suf = after[0]
    for d in range(1, t - 1):
        suf = suf * after[d]
    total = pre * suf
    qp = (q * pre).astype(bf16)
    kp = (kk * suf).astype(bf16)
    k_back = [kk] + [pltpu.roll(kk, d, 0) for d in range(1, t)]
    v_back = [v] + [pltpu.roll(v, d, 0) for d in range(1, t)]
    for h in range(HG_HEADS):
        sl = slice(h * HG_DK, (h + 1) * HG_DK)
        o = jnp.zeros((n, HG_DV), f32)
        for d in range(t):
            p = q[:, sl] * k_back[d][:, sl]
            if d > 0:
                p = p * dec[d][:, sl]
            a = jnp.sum(p, axis=-1, keepdims=True)
            if d > 0:
                a = jnp.where(tt >= d, a, 0.0)
            o = o + a * v_back[d][:, sl]
        state = s_ref[:, h]
        o = o + _dot(_block_diag(qp[:, sl], bb, t), state.reshape(bb * HG_DK, HG_DV).astype(bf16))
        ds = _dot_ta(kp[:, sl], _block_diag(i_ref[:, sl], bb, t))
        total_t = total[:, sl].T
        for b in range(bb):
            put(so_ref, b, h, state[b] * total_t[:, b * t:b * t + 1] + ds[:, b * HG_DV:(b + 1) * HG_DV])
        o_ref[:, sl] = _head_rms_gate(o, nw_ref[:, sl] * g_ref[:, sl].astype(f32)).astype(bf16)


def _hgrn_sample(q, lf, i, g, norm_w, state, stack, layer, bsz, t):
    hw = HG_HEADS * HG_DK
    bb = SAMPLE_BB
    heads_shape = (HG_HEADS, HG_DK, HG_DV)
    row = lambda: pl.BlockSpec((bb * t, hw), lambda j: (j, 0))
    st_shape, st_spec, put, finish = _stacked_state(layer, state.shape[0], bsz, heads_shape)
    args = [q, lf, i, g, norm_w.reshape(1, -1), state]
    in_specs = [row(), row(), row(), row(), _resident((1, hw)),
                pl.BlockSpec((None, bb) + heads_shape, lambda j: (layer, j, 0, 0, 0))]
    aliases = {}
    if stack is not None:
        aliases = {len(args): 1}
        args.append(stack)
        in_specs.append(pl.BlockSpec(memory_space=pl.ANY))
    return pl.pallas_call(
        functools.partial(_hgrn_sample_body, t=t, put=put, finish=finish),
        grid=(bsz // bb,),
        in_specs=in_specs,
        out_specs=[row(), st_spec],
        out_shape=[jax.ShapeDtypeStruct((bsz * t, hw), bf16), st_shape],
        input_output_aliases=aliases,
        compiler_params=_cparams(("parallel",)),
        name="hgrn_sample",
    )(*args)


def _lru_sample_body(x_ref, gl_ref, cs_ref, h0_ref, cw_ref, cb_ref, wa_ref, ba_ref, wx_ref, bx_ref, lam_ref,
                     o_ref, h_ref, cv_ref, *, t):
    xx = [cs_ref[j] for j in range(CONV_W - 1)] + [x_ref[j] for j in range(t)]
    cw = cw_ref[...]
    half_c_sp = (0.5 * LRU_C) * _softplus(-lam_ref[...])
    h = h0_ref[...]
    for j in range(t):
        xc = cb_ref[...]
        for i in range(CONV_W):
            xc = xc + xx[j + i] * cw[i:i + 1, :]
        a, u = _lru_gates(xc, wa_ref, ba_ref[...], wx_ref, bx_ref[...], half_c_sp)
        h = a * h + u
        o_ref[j] = (h * gl_ref[j].astype(f32)).astype(bf16)
    h_ref[...] = h
    for j in range(CONV_W - 1):
        cv_ref[j] = xx[t + j]


def _lru_sample(xl, gl, cs, h0, cw, cb, wa, ba, wx, bx, lam, bsz, t):
    full = lambda a: pl.BlockSpec(a.shape, lambda i: (0,) * a.ndim)
    args = (xl, gl, cs, h0, cw, cb.reshape(1, -1), wa, ba.reshape(1, -1), wx, bx.reshape(1, -1), lam.reshape(1, -1))
    return pl.pallas_call(
        functools.partial(_lru_sample_body, t=t),
        grid=(1,),
        in_specs=[full(a) for a in args],
        out_specs=[pl.BlockSpec((t, bsz, LRU_W), lambda i: (0, 0, 0)), pl.BlockSpec((bsz, LRU_W), lambda i: (0, 0)),
                   pl.BlockSpec((CONV_W - 1, bsz, LRU_W), lambda i: (0, 0, 0))],
        out_shape=[jax.ShapeDtypeStruct((t, bsz, LRU_W), bf16), jax.ShapeDtypeStruct((bsz, LRU_W), f32),
                   jax.ShapeDtypeStruct((CONV_W - 1, bsz, LRU_W), f32)],
        compiler_params=_cparams(("arbitrary",)),
        name="lru_sample",
    )(*args)


def _rope_tables(pos):
    inv = ROPE_BASE ** (-jnp.arange(0, RET_DK, 2, dtype=f32) / RET_DK)
    ang = pos.astype(f32)[:, None] * inv[None, :]
    cos, sin = jnp.cos(ang), jnp.sin(ang)
    return jnp.concatenate([cos, cos], axis=1), jnp.concatenate([-sin, sin], axis=1)


def _trunk(x, pos, w, states, bsz, t):
    cos, sin = _rope_tables(pos)
    tile_rows = min(TOKEN_TILE, bsz * t)
    if t < tile_rows:
        cos = jnp.tile(cos, (tile_rows // t, 1))
        sin = jnp.tile(sin, (tile_rows // t, 1))
    news = ([], [], [], [])
    ret_stack = hg_stack = None
    depth = w["w_ret"].shape[0]
    for l in range(depth):
        x = _ffn(x, w["ffn1_norm"][l], w["ffn1_wg"][l], w["ffn1_wu"][l], w["ffn1_wd"][l])
        q_r, k_r, v_r, g_r, xn = _proj_ret(x, w["mix_norm"][l], w["w_ret"][l], cos, sin)
        q_h, lf, i_h, g_h = _proj_hgrn(xn, w["w_hg"][l], w["hgrn_lb_logits"], l)
        lru_w = (w["conv_w"][l], w["conv_b"][l], w["lru_w_a"][l], w["lru_b_a"][l], w["lru_w_x"][l],
                 w["lru_b_x"][l], w["lru_lambda"][l])
        if states is None:
            o_r, ret_new = _ret_prompt(q_r, k_r, v_r, g_r, bsz, t)
            o_h, hg_new = _hgrn_prompt(q_h, lf, i_h, g_h, w["hgrn_norm"][l], bsz, t)
            o_l, gs, h_new, cv_new = _proj_lru_scan(xn, w["w_lru"][l], w["merge_bias"][l], *lru_w, bsz, t)
            news[0].append(ret_new)
            news[1].append(hg_new)
        else:
            st_ret, st_hg, st_lru, st_conv = states
            o_r, ret_stack = _ret_sample(q_r, k_r, v_r, g_r, st_ret, ret_stack, l, bsz, t)
            o_h, hg_stack = _hgrn_sample(q_h, lf, i_h, g_h, w["hgrn_norm"][l], st_hg, hg_stack, l, bsz, t)
            x_l, g_l, gs = _proj_lru(xn, w["w_lru"][l], w["merge_bias"][l])
            tm = lambda a: a.reshape(bsz, t, LRU_W).transpose(1, 0, 2)
            o_lt, h_new, cv_t = _lru_sample(tm(x_l), tm(g_l), st_conv[l].transpose(1, 0, 2), st_lru[l], *lru_w, bsz, t)
            o_l = o_lt.transpose(1, 0, 2).reshape(bsz * t, LRU_W)
            cv_new = cv_t.transpose(1, 0, 2)
        x = _mix_out(x, o_r, o_h, o_l, gs, w["w_ret_o"][l], w["w_hgrn_o"][l], w["w_lru_o"][l], w["w_mix_out"][l])
        last = l == depth - 1
        x = _ffn(x, w["ffn2_norm"][l], w["ffn2_wg"][l], w["ffn2_wu"][l], w["ffn2_wd"][l],
                 final_g=w["final_norm"] if last else None)
        news[2].append(h_new)
        news[3].append(cv_new)
    if states is None:
        return x, [jnp.stack(a, axis=0) for a in news]
    return x, [ret_stack, hg_stack, jnp.stack(news[2], axis=0), jnp.stack(news[3], axis=0)]


def kernel(x_prompt, x_sample, state_ret, state_hgrn, state_lru, state_conv, ffn1_norm, ffn1_w_gate, ffn1_w_up, ffn1_w_down, mix_norm, w_in, merge_bias, w_ret_o, hgrn_lb_logits, hgrn_norm, w_hgrn_o, conv_w, conv_b, lru_w_a, lru_b_a, lru_w_x, lru_b_x, lru_lambda, w_lru_o, w_mix_out, ffn2_norm, ffn2_w_gate, ffn2_w_up, ffn2_w_down, final_norm):
    c16 = lambda a: a.astype(bf16)
    def halves(*widths_and_flags):
        return jnp.concatenate([jnp.full((wd,), 0.5 if half else 1.0, f32) for wd, half in widths_and_flags])

    nqk, nv, hw = RET_HEADS * RET_DK, RET_HEADS * RET_DV, HG_HEADS * HG_DK
    ret_scale = halves((2 * nqk + nv, False), (nv, True))
    hg_scale = halves((hw, True), (2 * hw, False), (hw, True))
    lru_scale = halves((LRU_W, False), (LRU_W + N_BRANCH * D_MODEL, True))
    w = dict(
        ffn1_norm=ffn1_norm, ffn1_wg=c16(0.5 * ffn1_w_gate), ffn1_wu=c16(ffn1_w_up), ffn1_wd=c16(ffn1_w_down),
        mix_norm=mix_norm,
        w_ret=c16(w_in[:, :, :RET_COLS] * ret_scale),
        w_hg=c16(w_in[:, :, RET_COLS:RET_COLS + HG_COLS] * hg_scale),
        w_lru=c16(w_in[:, :, RET_COLS + HG_COLS:] * lru_scale),
        merge_bias=0.5 * merge_bias, w_ret_o=c16(w_ret_o), hgrn_lb_logits=hgrn_lb_logits, hgrn_norm=hgrn_norm,
        w_hgrn_o=c16(w_hgrn_o), conv_w=conv_w, conv_b=conv_b, lru_w_a=c16(0.5 * lru_w_a), lru_b_a=0.5 * lru_b_a,
        lru_w_x=c16(0.5 * lru_w_x), lru_b_x=0.5 * lru_b_x, lru_lambda=lru_lambda, w_lru_o=c16(w_lru_o),
        w_mix_out=c16(w_mix_out), ffn2_norm=ffn2_norm, ffn2_wg=c16(0.5 * ffn2_w_gate), ffn2_wu=c16(ffn2_w_up),
        ffn2_wd=c16(ffn2_w_down), final_norm=final_norm,
    )
    bp, tp, _ = x_prompt.shape
    bs, ts, _ = x_sample.shape
    y_p, st_p = _trunk(x_prompt.reshape(bp * tp, D_MODEL), jnp.arange(tp), w, None, bp, tp)
    y_s, st_s = _trunk(x_sample.reshape(bs * ts, D_MODEL), PAST_LEN + jnp.arange(ts), w,
                       (state_ret, state_hgrn, state_lru, state_conv), bs, ts)
    return (y_p.reshape(bp, tp, D_MODEL), y_s.reshape(bs, ts, D_MODEL),
            st_p[0], st_s[0], st_p[1], st_s[1], st_p[2], st_s[2], st_p[3], st_s[3])
```

```python
import functools

import jax
import jax.numpy as jnp
import numpy as np
from jax import lax
from jax.experimental import pallas as pl
from jax.experimental.pallas import tpu as pltpu

f32 = jnp.float32
bf16 = jnp.bfloat16

D_MODEL = 1024
FFN_HIDDEN = 2816
RET_HEADS, RET_DK, RET_DV = 4, 128, 256
HG_HEADS, HG_DK, HG_DV = 8, 128, 128
LRU_W, LRU_BLOCKS, LRU_BD = 1024, 8, 128
CONV_W = 4
LRU_C = 8.0
N_BRANCH = 3
ROPE_BASE = 10000.0
PAST_LEN = 16384
LOG2_E = 1.4426950408889634
EPS = 1e-6
RET_COLS = 2 * RET_HEADS * RET_DK + 2 * RET_HEADS * RET_DV
HG_COLS = 4 * HG_HEADS * HG_DK
LRU_COLS = 2 * LRU_W + N_BRANCH * D_MODEL

V7X_VMEM_BYTES = 64 * 1024 * 1024
VMEM_LIMIT = 56 * 1024 * 1024
SUBLANES = 8

BF16_ROWS = 16
TOKEN_TILE = 1024
FFN_SIDE_TILE = 512
LRU_TILE = 512
FFN_CHUNK = 256
RET_CHUNK = 256
HG_CHUNK = 128
HG_STEP_CHUNKS = 4
RET_STEP_CHUNKS = 4
HG_LEVELS = ((128, 64), (64, 32), (32, 16), (16, 8), (8, 4), (4, 2))
SAMPLE_BB = 8


def _cparams(sem):
    return pltpu.CompilerParams(dimension_semantics=sem, vmem_limit_bytes=VMEM_LIMIT)


def _resident(shape):
    nd = len(shape)
    return pl.BlockSpec(shape, lambda *_: (0,) * nd, pipeline_mode=pl.Buffered(1))


def _rmsnorm(x, g):
    return x * lax.rsqrt(jnp.mean(x * x, axis=-1, keepdims=True) + EPS) * g


def _dot(a, b):
    return jnp.dot(a, b, preferred_element_type=f32)


def _dot_tb(a, b):
    return lax.dot_general(a, b, (((1,), (1,)), ((), ())), preferred_element_type=f32)


def _dot_ta(a, b):
    return lax.dot_general(a, b, (((0,), (0,)), ((), ())), preferred_element_type=f32)


def _silu_of_half(hx):
    return hx + hx * jnp.tanh(hx)


def _gelu_of_half(hx):
    c = float(np.float32(np.sqrt(2.0 / np.pi)))
    return hx + hx * jnp.tanh(hx * (2.0 * c + (8.0 * c * 0.044715) * (hx * hx)))


def _softplus(x):
    return jnp.maximum(x, 0.0) + jnp.log1p(jnp.exp(-jnp.abs(x)))


def _block_diag(x, nb, rows_per):
    r, w = x.shape
    rb = lax.broadcasted_iota(jnp.int32, (r, nb * w), 0) // rows_per
    cb = lax.broadcasted_iota(jnp.int32, (r, nb * w), 1) // w
    return jnp.where(rb == cb, jnp.tile(x, (1, nb)), jnp.zeros((), x.dtype))


class _CastJob:
    def __init__(self, w, layer, col_off=0, col_w=None, scale=None):
        self.w, self.layer, self.col_off, self.scale = w, layer, col_off, scale
        self.rows = w.shape[1]
        self.col_w = w.shape[2] - col_off if col_w is None else col_w
        self.win = self.col_w if col_off == 0 else int(np.gcd(col_off, self.col_w))
        self.vector_scale = scale is not None and not isinstance(scale, float)

    def specs(self, steps):
        nblk = steps
        while self.rows % nblk or (self.rows // nblk) % BF16_ROWS:
            nblk //= 2
        rep, rb, layer, win, first = steps // nblk, self.rows // nblk, self.layer, self.win, self.col_off // self.win
        ins = [pl.BlockSpec((None, rb, win), lambda i, j=j: (layer, i // rep, first + j))
               for j in range(self.col_w // win)]
        args = [self.w] * len(ins)
        if self.vector_scale:
            ins.append(_resident((1, self.col_w)))
            args.append(self.scale.reshape(1, -1))
        out = pl.BlockSpec((rb, self.col_w), lambda i: (i // rep, 0))
        return args, ins, out, jax.ShapeDtypeStruct((self.rows, self.col_w), bf16)

    def run(self, in_refs, out_ref):
        nwin = self.col_w // self.win
        for j in range(nwin):
            cols = slice(j * self.win, (j + 1) * self.win)
            v = in_refs[j][...]
            if self.vector_scale:
                v = v * in_refs[nwin][:, cols]
            elif self.scale is not None:
                v = v * self.scale
            out_ref[:, cols] = v.astype(bf16)


def _ffn_body(x_ref, g_ref, wg_ref, wu_ref, wd_ref, fg_ref, *rest, final, jobs, job_nin):
    n_side_in = sum(job_nin)
    o_ref = rest[n_side_in]
    x = x_ref[...]
    xn = _rmsnorm(x, g_ref[...]).astype(bf16)
    acc = jnp.zeros(x.shape, f32)
    for c in range(FFN_HIDDEN // FFN_CHUNK):
        sl = slice(c * FFN_CHUNK, (c + 1) * FFN_CHUNK)
        g = _dot(xn, wg_ref[:, sl])
        u = _dot(xn, wu_ref[:, sl])
        h = (_silu_of_half(g) * u).astype(bf16)
        acc = acc + _dot(h, wd_ref[sl, :])
    y = x + 0.5 * acc
    if final:
        y = _rmsnorm(y, fg_ref[...])
    o_ref[...] = y
    start = 0
    for k, (job, nin) in enumerate(zip(jobs, job_nin)):
        job.run(rest[start:start + nin], rest[n_side_in + 1 + k])
        start += nin


def _ffn(x, g, wg, wu, wd, final_g=None, jobs=()):
    n = x.shape[0]
    tm = min(FFN_SIDE_TILE if jobs else TOKEN_TILE, n)
    steps = n // tm
    final = final_g is not None
    fg = final_g if final else g
    row = pl.BlockSpec((tm, D_MODEL), lambda i: (i, 0))
    args = [x, g.reshape(1, -1), wg, wu, wd, fg.reshape(1, -1)]
    in_specs = [row, _resident((1, D_MODEL)), _resident(wg.shape), _resident(wu.shape),
                _resident(wd.shape), _resident((1, D_MODEL))]
    out_specs, out_shape, job_nin = [row], [jax.ShapeDtypeStruct((n, D_MODEL), f32)], []
    for job in jobs:
        a, i_s, o_s, o_sh = job.specs(steps)
        args += a
        in_specs += i_s
        out_specs.append(o_s)
        out_shape.append(o_sh)
        job_nin.append(len(a))
    outs = pl.pallas_call(
        functools.partial(_ffn_body, final=final, jobs=tuple(jobs), job_nin=tuple(job_nin)),
        grid=(steps,),
        in_specs=in_specs,
        out_specs=out_specs,
        out_shape=out_shape,
        compiler_params=_cparams(("arbitrary",) if jobs else ("parallel",)),
        name="ffn",
    )(*args)
    return (outs[0], list(outs[1:])) if jobs else outs[0]


def _proj_ret_body(x_ref, g_ref, w_ref, cos_ref, sin_ref, q_ref, k_ref, v_ref, gr_ref, xn_ref):
    xn = _rmsnorm(x_ref[...], g_ref[...]).astype(bf16)
    xn_ref[...] = xn
    cos = cos_ref[...]
    sin = sin_ref[...]
    nqk = RET_HEADS * RET_DK

    def rot(z):
        return z * cos + pltpu.roll(z, RET_DK // 2, 1) * sin

    for c in range(nqk // 256):
        zq = _dot(xn, w_ref[:, c * 256:(c + 1) * 256])
        zk = _dot(xn, w_ref[:, nqk + c * 256:nqk + (c + 1) * 256])
        for j in range(2):
            sl = slice(c * 256 + j * 128, c * 256 + (j + 1) * 128)
            q_ref[:, sl] = rot(zq[:, j * 128:(j + 1) * 128]).astype(bf16)
            k_ref[:, sl] = (rot(zk[:, j * 128:(j + 1) * 128]) * (RET_DK ** -0.5)).astype(bf16)
    nv = RET_HEADS * RET_DV
    for c in range(nv // 256):
        sl = slice(c * 256, (c + 1) * 256)
        v_ref[:, sl] = _dot(xn, w_ref[:, 2 * nqk + c * 256:2 * nqk + (c + 1) * 256]).astype(bf16)
        zg = _dot(xn, w_ref[:, 2 * nqk + nv + c * 256:2 * nqk + nv + (c + 1) * 256])
        gr_ref[:, sl] = _silu_of_half(zg).astype(bf16)


def _proj_ret(x, g, w, cos, sin):
    n = x.shape[0]
    tm = min(TOKEN_TILE, n)
    row = lambda w_: pl.BlockSpec((tm, w_), lambda i: (i, 0))
    nqk, nv = RET_HEADS * RET_DK, RET_HEADS * RET_DV
    assert cos.shape[0] % tm == 0
    period = cos.shape[0] // tm
    rope = pl.BlockSpec((tm, RET_DK), lambda i: (i % period, 0))
    return pl.pallas_call(
        _proj_ret_body,
        grid=(n // tm,),
        in_specs=[row(D_MODEL), _resident((1, D_MODEL)), _resident(w.shape), rope, rope],
        out_specs=[row(nqk), row(nqk), row(nv), row(nv), row(D_MODEL)],
        out_shape=[jax.ShapeDtypeStruct((n, nqk), bf16), jax.ShapeDtypeStruct((n, nqk), bf16),
                   jax.ShapeDtypeStruct((n, nv), bf16), jax.ShapeDtypeStruct((n, nv), bf16),
                   jax.ShapeDtypeStruct((n, D_MODEL), bf16)],
        compiler_params=_cparams(("parallel",)),
        name="proj_ret",
    )(x, g.reshape(1, -1), w, cos, sin)


def _proj_hgrn_body(xn_ref, w_ref, lbl_ref, q_ref, lf_ref, i_ref, gh_ref, *, layer):
    xn = xn_ref[...]
    hw = HG_HEADS * HG_DK
    if layer > 0:
        lg = lbl_ref[...]
        e = jnp.exp(lg - jnp.max(lg, axis=0, keepdims=True))
        sm = e / jnp.sum(e, axis=0, keepdims=True)
        lb = jnp.sum(sm[1:layer + 1], axis=0, keepdims=True)
    for c in range(hw // 256):
        sl = slice(c * 256, (c + 1) * 256)
        q_ref[:, sl] = _silu_of_half(_dot(xn, w_ref[:, sl])).astype(bf16)
        zf = _dot(xn, w_ref[:, hw + c * 256:hw + (c + 1) * 256])
        t = jnp.exp2(jnp.abs(zf) * -LOG2_E)
        num = jnp.minimum(zf, 0.0)
        if layer > 0:
            lbc = lb[:, sl]
            num = jnp.maximum(jnp.log(jnp.where(zf < 0.0, lbc + t, 1.0 + lbc * t)), num)
        lf_ref[:, sl] = num - jnp.log(1.0 + t)
        i_ref[:, sl] = _dot(xn, w_ref[:, 2 * hw + c * 256:2 * hw + (c + 1) * 256]).astype(bf16)
        gh_ref[:, sl] = _silu_of_half(_dot(xn, w_ref[:, 3 * hw + c * 256:3 * hw + (c + 1) * 256])).astype(bf16)


def _proj_hgrn(xn, w, lb_logits, layer):
    n = xn.shape[0]
    tm = min(TOKEN_TILE, n)
    hw = HG_HEADS * HG_DK
    row = lambda w_: pl.BlockSpec((tm, w_), lambda i: (i, 0))
    return pl.pallas_call(
        functools.partial(_proj_hgrn_body, layer=layer),
        grid=(n // tm,),
        in_specs=[row(D_MODEL), _resident(w.shape), _resident(lb_logits.shape)],
        out_specs=[row(hw), row(hw), row(hw), row(hw)],
        out_shape=[jax.ShapeDtypeStruct((n, hw), bf16), jax.ShapeDtypeStruct((n, hw), f32),
                   jax.ShapeDtypeStruct((n, hw), bf16), jax.ShapeDtypeStruct((n, hw), bf16)],
        compiler_params=_cparams(("parallel",)),
        name="proj_hgrn",
    )(xn, w, lb_logits)


def _merge_gates(xn, w_ref, mb_ref, gs_ref, chunks=range(N_BRANCH * D_MODEL // 256)):
    for c in chunks:
        sl = slice(c * 256, (c + 1) * 256)
        half_z = _dot(xn, w_ref[:, 2 * LRU_W + c * 256:2 * LRU_W + (c + 1) * 256]) + mb_ref[:, sl]
        gs_ref[:, sl] = (0.5 + 0.5 * jnp.tanh(half_z)).astype(bf16)


def _proj_lru_body(xn_ref, w_ref, mb_ref, xl_ref, gl_ref, gs_ref):
    xn = xn_ref[...]
    for c in range(LRU_W // 256):
        sl = slice(c * 256, (c + 1) * 256)
        xl_ref[:, sl] = _dot(xn, w_ref[:, sl])
        zg = _dot(xn, w_ref[:, LRU_W + c * 256:LRU_W + (c + 1) * 256])
        gl_ref[:, sl] = _gelu_of_half(zg).astype(bf16)
    _merge_gates(xn, w_ref, mb_ref, gs_ref)


def _proj_lru(xn, w, merge_bias):
    n = xn.shape[0]
    tm = min(TOKEN_TILE, n)
    row = lambda w_: pl.BlockSpec((tm, w_), lambda i: (i, 0))
    ng = N_BRANCH * D_MODEL
    return pl.pallas_call(
        _proj_lru_body,
        grid=(n // tm,),
        in_specs=[row(D_MODEL), _resident(w.shape), _resident((1, ng))],
        out_specs=[row(LRU_W), row(LRU_W), row(ng)],
        out_shape=[jax.ShapeDtypeStruct((n, LRU_W), f32), jax.ShapeDtypeStruct((n, LRU_W), bf16),
                   jax.ShapeDtypeStruct((n, ng), bf16)],
        compiler_params=_cparams(("parallel",)),
        name="proj_lru",
    )(xn, w, merge_bias.reshape(1, -1))


def _mix_out_body(x_ref, or_ref, oh_ref, ol_ref, gs_ref, wr_ref, wh_ref, wl_ref, wo_ref, o_ref):
    d = D_MODEL
    merged = gs_ref[:, 0:d].astype(f32) * _dot(or_ref[...], wr_ref[...])
    merged = merged + gs_ref[:, d:2 * d].astype(f32) * _dot(oh_ref[...], wh_ref[...])
    merged = merged + gs_ref[:, 2 * d:3 * d].astype(f32) * _dot(ol_ref[...], wl_ref[...])
    o_ref[...] = x_ref[...] + _dot(merged.astype(bf16), wo_ref[...])


def _mix_out(x, o_r, o_h, o_l, gs, wr, wh, wl, wo):
    n = x.shape[0]
    tm = min(TOKEN_TILE, n)
    row = lambda w_: pl.BlockSpec((tm, w_), lambda i: (i, 0))
    return pl.pallas_call(
        _mix_out_body,
        grid=(n // tm,),
        in_specs=[row(D_MODEL), row(D_MODEL), row(D_MODEL), row(D_MODEL), row(N_BRANCH * D_MODEL),
                  _resident(wr.shape), _resident(wh.shape), _resident(wl.shape), _resident(wo.shape)],
        out_specs=row(D_MODEL),
        out_shape=jax.ShapeDtypeStruct((n, D_MODEL), f32),
        compiler_params=_cparams(("parallel",)),
        name="mix_out",
    )(x, o_r, o_h, o_l, gs, wr, wh, wl, wo)


def _ret_log_gamma():
    return jnp.log1p(-jnp.exp2(-5.0 - jnp.arange(RET_HEADS, dtype=f32)))


def _ret_tables(c):
    log_g = _ret_log_gamma()
    idx = jnp.arange(c, dtype=f32)
    diff = idx[:, None] - idx[None, :]
    causal = diff >= 0
    d_intra = jnp.where(causal, jnp.exp(log_g[:, None, None] * jnp.where(causal, diff, 0.0)), 0.0)
    q_dec = jnp.exp(log_g[:, None] * (idx + 1.0))[:, :, None]
    k_dec = jnp.exp(log_g[:, None] * (c - 1.0 - idx))[:, :, None]
    s_dec = jnp.exp(log_g * c)
    return d_intra, q_dec, k_dec, s_dec


def _head_rms_gate(o, gate):
    return o * lax.rsqrt(jnp.mean(o * o, axis=-1, keepdims=True) + EPS) * gate


def _ret_prompt_body(q_ref, k_ref, v_ref, g_ref, dm_ref, qd_ref, kd_ref, sd_ref, o_ref, st_ref, s_scr):
    c = pl.program_id(1)

    @pl.when(c == 0)
    def _():
        s_scr[...] = jnp.zeros(s_scr.shape, f32)

    heads = range(RET_HEADS)
    ks = [slice(h * RET_DK, (h + 1) * RET_DK) for h in heads]
    vs = [slice(h * RET_DV, (h + 1) * RET_DV) for h in heads]
    for r0 in range(0, q_ref.shape[0], RET_CHUNK):
        rows = slice(r0, r0 + RET_CHUNK)
        scores = [(_dot_tb(q_ref[rows, ks[h]], k_ref[rows, ks[h]]) * dm_ref[h]).astype(bf16) for h in heads]
        inter = [qd_ref[h] * _dot(q_ref[rows, ks[h]], s_scr[h].astype(bf16)) for h in heads]
        outs = [_dot(scores[h], v_ref[rows, vs[h]]) + inter[h] for h in heads]
        for h in heads:
            kd = (k_ref[rows, ks[h]].astype(f32) * kd_ref[h]).astype(bf16)
            s_scr[h] = s_scr[h] * sd_ref[h] + _dot_ta(kd, v_ref[rows, vs[h]])
        for h in heads:
            o_ref[rows, vs[h]] = _head_rms_gate(outs[h], g_ref[rows, vs[h]].astype(f32)).astype(bf16)

    @pl.when(c == pl.num_programs(1) - 1)
    def _():
        st_ref[0] = s_scr[...]


def _ret_prompt(q, k, v, g, bsz, t):
    c = RET_CHUNK
    rows = RET_STEP_CHUNKS * c
    nc = t // rows
    nqk, nv = RET_HEADS * RET_DK, RET_HEADS * RET_DV
    d_intra, q_dec, k_dec, s_dec = _ret_tables(c)
    sd = jnp.broadcast_to(s_dec[:, None, None], (RET_HEADS, 1, RET_DV))
    row = lambda w_: pl.BlockSpec((rows, w_), lambda b, i: (b * nc + i, 0))
    return pl.pallas_call(
        _ret_prompt_body,
        grid=(bsz, nc),
        in_specs=[row(nqk), row(nqk), row(nv), row(nv), _resident(d_intra.shape), _resident(q_dec.shape),
                  _resident(k_dec.shape), _resident(sd.shape)],
        out_specs=[row(nv), pl.BlockSpec((1, RET_HEADS, RET_DK, RET_DV), lambda b, i: (b, 0, 0, 0))],
        out_shape=[jax.ShapeDtypeStruct((bsz * t, nv), bf16),
                   jax.ShapeDtypeStruct((bsz, RET_HEADS, RET_DK, RET_DV), f32)],
        scratch_shapes=[pltpu.VMEM((RET_HEADS, RET_DK, RET_DV), f32)],
        compiler_params=_cparams(("parallel", "arbitrary")),
        name="ret_prompt",
    )(q, k, v, g, d_intra, q_dec, k_dec, sd)


def _hg_level_masks(c):
    t = np.arange(c)[:, None]
    s = np.arange(c)[None, :]
    ms = [((t // blk == s // blk) & (t % blk >= half) & (s % blk < half)) for blk, half in HG_LEVELS]
    return jnp.asarray(np.stack(ms).astype(np.float32))


def _boundary_rows(b, blk, half):
    c, w = b.shape
    if half >= SUBLANES:
        return jnp.concatenate(
            [jnp.broadcast_to(b[j * blk + half - 1:j * blk + half, :], (blk, w)) for j in range(c // blk)], axis=0)
    b3 = b.reshape(c // SUBLANES, SUBLANES, w)
    row8 = lax.broadcasted_iota(jnp.int32, (1, SUBLANES, 1), 1)
    out = b3[:, half - 1:half, :]
    for j in range(1, SUBLANES // blk):
        out = jnp.where(row8 >= j * blk, b3[:, j * blk + half - 1:j * blk + half, :], out)
    return jnp.broadcast_to(out, b3.shape).reshape(c, w)


def _upper_lower(up, low, blk, half):
    c = up.shape[0]
    if half >= SUBLANES:
        parts = []
        for j in range(c // blk):
            parts += [low[j * blk:j * blk + half, :], up[j * blk + half:(j + 1) * blk, :]]
        return jnp.concatenate(parts, axis=0)
    upper = lax.broadcasted_iota(jnp.int32, (c, 1), 0) % blk >= half
    return jnp.where(upper, up, low)


def _hgrn_chunk(q_ref, lf_ref, i_ref, g_ref, nw_ref, tri_ref, mask_ref, o_ref, s_scr, rows):
    c = HG_CHUNK
    lf_all = lf_ref[rows, :]
    hi = lf_all.astype(bf16)
    lo = (lf_all - hi.astype(f32)).astype(bf16)
    tri = tri_ref[...]
    b2_all = (_dot(tri, hi) + _dot(tri, lo)) * LOG2_E
    odd = lax.broadcasted_iota(jnp.int32, (c, 1), 0) % 2 == 1

    f_all = jnp.exp(lf_all)
    kk_all = 1.0 - f_all
    q_all = q_ref[rows, :].astype(f32)
    a_mats = [None] * HG_HEADS
    for lev, (blk, half) in enumerate(HG_LEVELS):
        keep = mask_ref[lev] > 0.5
        for h in range(HG_HEADS):
            sl = slice(h * HG_DK, (h + 1) * HG_DK)
            b2 = b2_all[:, sl]
            bm = _boundary_rows(b2, blk, half)
            e = jnp.exp2(_upper_lower(b2 - bm, bm - b2, blk, half))
            m = (_upper_lower(q_all[:, sl], kk_all[:, sl], blk, half) * e).astype(bf16)
            p = _dot_tb(m, m)
            a_mats[h] = jnp.where(keep, p, 0.0) if lev == 0 else jnp.where(keep, p, a_mats[h])
    a_mats = [a.astype(bf16) for a in a_mats]

    for h in range(HG_HEADS):
        sl = slice(h * HG_DK, (h + 1) * HG_DK)
        b2 = b2_all[:, sl]
        q = q_all[:, sl]
        f = f_all[:, sl]
        kk = kk_all[:, sl]
        v16 = i_ref[rows, sl]
        o = _dot(a_mats[h], v16)
        vf = v16.astype(f32)
        a0 = jnp.sum(q * kk, axis=-1, keepdims=True)
        a1 = jnp.where(odd, jnp.sum(q * f * pltpu.roll(kk, 1, 0), axis=-1, keepdims=True), 0.0)
        o = o + a0 * vf + a1 * pltpu.roll(vf, 1, 0)
        st = s_scr[h]
        o = o + _dot_tb((q * jnp.exp2(b2)).astype(bf16), st.astype(bf16))
        bl2 = b2[c - 1:c, :]
        kdec = (kk * jnp.exp2(bl2 - b2)).astype(bf16)
        s_scr[h] = st * jnp.exp2(bl2) + _dot_ta(v16, kdec)
        o_ref[rows, sl] = _head_rms_gate(o, nw_ref[:, sl] * g_ref[rows, sl].astype(f32)).astype(bf16)


def _hgrn_prompt_body(q_ref, lf_ref, i_ref, g_ref, nw_ref, tri_ref, mask_ref, o_ref, st_ref, s_scr):
    ci = pl.program_id(1)
    c = HG_CHUNK

    @pl.when(ci == 0)
    def _():
        s_scr[...] = jnp.zeros(s_scr.shape, f32)

    for r0 in range(0, q_ref.shape[0], c):
        _hgrn_chunk(q_ref, lf_ref, i_ref, g_ref, nw_ref, tri_ref, mask_ref, o_ref, s_scr, slice(r0, r0 + c))

    @pl.when(ci == pl.num_programs(1) - 1)
    def _():
        for h in range(HG_HEADS):
            st_ref[0, h] = s_scr[h].T


def _hgrn_prompt(q, lf, i, g, norm_w, bsz, t):
    c = HG_CHUNK
    rows = HG_STEP_CHUNKS * c
    nc = t // rows
    hw = HG_HEADS * HG_DK
    tri = jnp.asarray(np.tril(np.ones((c, c), np.float32))).astype(bf16)
    masks = _hg_level_masks(c)
    row = lambda: pl.BlockSpec((rows, hw), lambda b, j: (b * nc + j, 0))
    return pl.pallas_call(
        _hgrn_prompt_body,
        grid=(bsz, nc),
        in_specs=[row(), row(), row(), row(), _resident((1, hw)), _resident(tri.shape), _resident(masks.shape)],
        out_specs=[row(), pl.BlockSpec((1, HG_HEADS, HG_DK, HG_DV), lambda b, j: (b, 0, 0, 0))],
        out_shape=[jax.ShapeDtypeStruct((bsz * t, hw), bf16),
                   jax.ShapeDtypeStruct((bsz, HG_HEADS, HG_DK, HG_DV), f32)],
        scratch_shapes=[pltpu.VMEM((HG_HEADS, HG_DV, HG_DK), f32)],
        compiler_params=_cparams(("parallel", "arbitrary")),
        name="hgrn_prompt",
    )(q, lf, i, g, norm_w.reshape(1, -1), tri, masks)


def _lru_gate_math(xc, half_ya, half_yx, half_c_sp):
    neg_log_a = half_c_sp + half_c_sp * jnp.tanh(half_ya)
    a = jnp.exp(-neg_log_a)
    y = jnp.tanh(neg_log_a) * (1.0 + a * a)
    scale = jnp.where(y > 0.0, y * lax.rsqrt(y), 0.0)
    hx = 0.5 * xc
    return a, scale * (hx + hx * jnp.tanh(half_yx))


def _lru_gate_block(xc, wa, ba, wx, bx, half_c_sp):
    xb = xc.astype(bf16)
    return _lru_gate_math(xc, _dot(xb, wa) + ba, _dot(xb, wx) + bx, half_c_sp)


def _lru_gates(xc, wa_ref, ba, wx_ref, bx, half_c_sp):
    parts = []
    for n in range(LRU_BLOCKS):
        sl = slice(n * LRU_BD, (n + 1) * LRU_BD)
        parts.append(_lru_gate_block(xc[:, sl], wa_ref[n], ba[:, sl], wx_ref[n], bx[:, sl], half_c_sp[:, sl]))
    return jnp.concatenate([p[0] for p in parts], axis=1), jnp.concatenate([p[1] for p in parts], axis=1)


def _strided_rows(g, rows):
    j_count = rows // SUBLANES
    k, j0 = divmod(SUBLANES * g, j_count)
    return pl.ds(SUBLANES * j0 + k, SUBLANES, stride=SUBLANES)


def _shift_time(xp, tail, d):
    rows = xp.shape[0]
    row8 = lax.broadcasted_iota(jnp.int32, (SUBLANES, 1), 0)
    last = pltpu.roll(xp[rows - SUBLANES * d:rows, :].reshape(d, SUBLANES, xp.shape[1]), 1, 1)
    head = [jnp.where(row8 == 0, tail[SUBLANES - d + j:SUBLANES - d + j + 1, :], last[j]) for j in range(d)]
    return jnp.concatenate(head + [xp[0:rows - SUBLANES * d, :]], axis=0)


def _proj_lru_scan_body(xn_ref, w_ref, mb_ref, cw_ref, cb_ref, wa_ref, ba_ref, wx_ref, bx_ref, lam_ref,
                        o_ref, gs_ref, h_ref, cv_ref, xp_scr, hp_scr, ap_scr, tail_scr, h_scr):
    ti = pl.program_id(1)
    tm = xn_ref.shape[0]
    nj = tm // SUBLANES

    @pl.when(ti == 0)
    def _():
        tail_scr[...] = jnp.zeros(tail_scr.shape, f32)
        h_scr[...] = jnp.zeros(h_scr.shape, f32)

    xn = xn_ref[...]
    for c in range(LRU_W // 256):
        z = _dot(xn, w_ref[:, c * 256:(c + 1) * 256])
        for half in range(2):
            for g in range(nj):
                xp_scr[2 * c + half, _strided_rows(g, tm), :] = (
                    z[SUBLANES * g:SUBLANES * (g + 1), half * LRU_BD:(half + 1) * LRU_BD])
    cw = cw_ref[...]
    cb = cb_ref[...]
    ba = ba_ref[...]
    bx = bx_ref[...]
    half_c_sp = (0.5 * LRU_C) * _softplus(-lam_ref[...])
    n_gate = N_BRANCH * D_MODEL // 256
    n_pair = LRU_W // 256
    for p in range(n_pair):
        blocks = (2 * p, 2 * p + 1)
        sls = [slice(n * LRU_BD, (n + 1) * LRU_BD) for n in blocks]
        xcs, yas, yxs = [], [], []
        for n, sl in zip(blocks, sls):
            xp = xp_scr[n]
            tail = tail_scr[:, sl]
            xc = cb[:, sl] + xp * cw[CONV_W - 1:CONV_W, sl]
            for d in range(1, CONV_W):
                xc = xc + _shift_time(xp, tail, d) * cw[CONV_W - 1 - d:CONV_W - d, sl]
            xb = xc.astype(bf16)
            xcs.append(xc)
            yas.append(_dot(xb, wa_ref[n]) + ba[:, sl])
            yxs.append(_dot(xb, wx_ref[n]) + bx[:, sl])
        hseg = [jnp.zeros((SUBLANES, LRU_BD), f32) for _ in blocks]
        aseg = [jnp.ones((SUBLANES, LRU_BD), f32) for _ in blocks]
        for j in range(nj):
            rows = slice(SUBLANES * j, SUBLANES * (j + 1))
            for i, (n, sl) in enumerate(zip(blocks, sls)):
                a, u = _lru_gate_math(xcs[i][rows, :], yas[i][rows, :], yxs[i][rows, :], half_c_sp[:, sl])
                hseg[i] = a * hseg[i] + u
                aseg[i] = a * aseg[i]
                hp_scr[n, rows, :] = hseg[i]
                ap_scr[n, rows, :] = aseg[i]
        for i, (n, sl) in enumerate(zip(blocks, sls)):
            s = h_scr[0:1, sl]
            starts = []
            for k in range(SUBLANES):
                starts.append(s)
                s = aseg[i][k:k + 1, :] * s + hseg[i][k:k + 1, :]
            h_scr[:, sl] = jnp.broadcast_to(s, (h_scr.shape[0], LRU_BD))
            start8 = jnp.concatenate(starts, axis=0)
            hp_scr[n] = (hp_scr[n].reshape(nj, SUBLANES, LRU_BD)
                         + ap_scr[n].reshape(nj, SUBLANES, LRU_BD) * start8).reshape(tm, LRU_BD)
            tail_scr[:, sl] = xp_scr[n, _strided_rows(nj - 1, tm), :]
        zg = _gelu_of_half(_dot(xn, w_ref[:, LRU_W + p * 256:LRU_W + (p + 1) * 256]))
        for half in range(2):
            n = 2 * p + half
            hs = jnp.concatenate([hp_scr[n, _strided_rows(g, tm), :] for g in range(nj)], axis=0)
            o_ref[:, n * LRU_BD:(n + 1) * LRU_BD] = (hs * zg[:, half * LRU_BD:(half + 1) * LRU_BD]).astype(bf16)
        _merge_gates(xn, w_ref, mb_ref, gs_ref, range(p * n_gate // n_pair, (p + 1) * n_gate // n_pair))

    @pl.when(ti == pl.num_programs(1) - 1)
    def _():
        h_ref[0] = h_scr[0:1, :]
        cv_ref[0] = tail_scr[SUBLANES - (CONV_W - 1):SUBLANES, :]


def _proj_lru_scan(xn, w, merge_bias, cw, cb, wa, ba, wx, bx, lam, bsz, t):
    tm = min(LRU_TILE, t)
    assert tm % (SUBLANES * SUBLANES) == 0
    nt = t // tm
    ng = N_BRANCH * D_MODEL
    row = lambda w_: pl.BlockSpec((tm, w_), lambda b, j: (b * nt + j, 0))
    vec = _resident((1, LRU_W))
    o, gs, h, cv = pl.pallas_call(
        _proj_lru_scan_body,
        grid=(bsz, nt),
        in_specs=[row(D_MODEL), _resident(w.shape), _resident((1, ng)),
                  _resident((CONV_W, LRU_W)), vec, _resident(wa.shape), vec, _resident(wx.shape), vec, vec],
        out_specs=[row(LRU_W), row(ng), pl.BlockSpec((1, 1, LRU_W), lambda b, j: (b, 0, 0)),
                   pl.BlockSpec((1, CONV_W - 1, LRU_W), lambda b, j: (b, 0, 0))],
        out_shape=[jax.ShapeDtypeStruct((bsz * t, LRU_W), bf16), jax.ShapeDtypeStruct((bsz * t, ng), bf16),
                   jax.ShapeDtypeStruct((bsz, 1, LRU_W), f32), jax.ShapeDtypeStruct((bsz, CONV_W - 1, LRU_W), f32)],
        scratch_shapes=[pltpu.VMEM((LRU_BLOCKS, tm, LRU_BD), f32)] * 3 + [pltpu.VMEM((SUBLANES, LRU_W), f32)] * 2,
        compiler_params=_cparams(("parallel", "arbitrary")),
        name="proj_lru_scan",
    )(xn, w, merge_bias.reshape(1, -1), cw, cb.reshape(1, -1), wa, ba.reshape(1, -1), wx,
      bx.reshape(1, -1), lam.reshape(1, -1))
    return o, gs, h[:, 0], cv


def _ret_sample_tables(t):
    log_g = _ret_log_gamma()
    r = jnp.arange(SAMPLE_BB * t)
    tt = (r % t).astype(f32)
    same = (r[:, None] // t) == (r[None, :] // t)
    diff = tt[:, None] - tt[None, :]
    ok = same & (diff >= 0)
    dm = jnp.where(ok, jnp.exp(log_g[:, None, None] * jnp.where(ok, diff, 0.0)), 0.0)
    qd = jnp.exp(log_g[:, None] * (tt + 1.0))[:, :, None]
    kd = jnp.exp(log_g[:, None] * (t - 1.0 - tt))[:, :, None]
    sd = jnp.exp(log_g * t)
    return dm, qd, kd, sd


def _stacked_state(layer, depth, bsz, heads_shape):
    shape = jax.ShapeDtypeStruct((depth, bsz) + heads_shape, f32)
    if layer == 0:
        spec = pl.BlockSpec((depth, SAMPLE_BB) + heads_shape, lambda i: (0, i, 0, 0, 0))

        def put(so_ref, b, h, val):
            so_ref[0, b, h] = val

        def finish(so_ref):
            for l in range(1, depth):
                so_ref[l] = jnp.zeros((SAMPLE_BB,) + heads_shape, f32)
    else:
        spec = pl.BlockSpec((None, SAMPLE_BB) + heads_shape, lambda i: (layer, i, 0, 0, 0))

        def put(so_ref, b, h, val):
            so_ref[b, h] = val

        def finish(so_ref):
            pass
    return shape, spec, put, finish


def _ret_sample_body(q_ref, k_ref, v_ref, g_ref, s_ref, dm_ref, qd_ref, kd_ref, sd_ref, *rest, t, put, finish):
    o_ref, so_ref = rest[-2:]
    bb = SAMPLE_BB
    finish(so_ref)
    for h in range(RET_HEADS):
        ks = slice(h * RET_DK, (h + 1) * RET_DK)
        vs = slice(h * RET_DV, (h + 1) * RET_DV)
        q = q_ref[:, ks]
        k = k_ref[:, ks]
        v = v_ref[:, vs]
        state = s_ref[:, h]
        scores = (_dot_tb(q, k) * dm_ref[h]).astype(bf16)
        inter = _dot(_block_diag(q, bb, t), state.reshape(bb * RET_DK, RET_DV).astype(bf16))
        o = _dot(scores, v) + qd_ref[h] * inter
        kd = (k.astype(f32) * kd_ref[h]).astype(bf16)
        ds = _dot_ta(kd, _block_diag(v, bb, t))
        for b in range(bb):
            put(so_ref, b, h, state[b] * sd_ref[h] + ds[:, b * RET_DV:(b + 1) * RET_DV])
        o_ref[:, vs] = _head_rms_gate(o, g_ref[:, vs].astype(f32)).astype(bf16)


def _ret_sample(q, k, v, g, state, stack, layer, bsz, t):
    nqk, nv = RET_HEADS * RET_DK, RET_HEADS * RET_DV
    bb = SAMPLE_BB
    heads_shape = (RET_HEADS, RET_DK, RET_DV)
    dm, qd, kd, s_dec = _ret_sample_tables(t)
    sd = jnp.broadcast_to(s_dec[:, None, None], (RET_HEADS, 1, RET_DV))
    row = lambda w_: pl.BlockSpec((bb * t, w_), lambda i: (i, 0))
    st_shape, st_spec, put, finish = _stacked_state(layer, state.shape[0], bsz, heads_shape)
    args = [q, k, v, g, state, dm, qd, kd, sd]
    in_specs = [row(nqk), row(nqk), row(nv), row(nv),
                pl.BlockSpec((None, bb) + heads_shape, lambda i: (layer, i, 0, 0, 0)),
                _resident(dm.shape), _resident(qd.shape), _resident(kd.shape), _resident(sd.shape)]
    aliases = {}
    if stack is not None:
        aliases = {len(args): 1}
        args.append(stack)
        in_specs.append(pl.BlockSpec(memory_space=pl.ANY))
    return pl.pallas_call(
        functools.partial(_ret_sample_body, t=t, put=put, finish=finish),
        grid=(bsz // bb,),
        in_specs=in_specs,
        out_specs=[row(nv), st_spec],
        out_shape=[jax.ShapeDtypeStruct((bsz * t, nv), bf16), st_shape],
        input_output_aliases=aliases,
        compiler_params=_cparams(("parallel",)),
        name="ret_sample",
    )(*args)


def _hgrn_sample_body(q_ref, lf_ref, i_ref, g_ref, nw_ref, s_ref, *rest, t, put, finish):
    assert t >= 2
    o_ref, so_ref = rest[-2:]
    finish(so_ref)
    bb = SAMPLE_BB
    n = bb * t
    tt = lax.broadcasted_iota(jnp.int32, (n, 1), 0) % t
    f = jnp.exp(lf_ref[...])
    kk = 1.0 - f
    q = q_ref[...].astype(f32)
    v = i_ref[...].astype(f32)
    before = [jnp.where(tt >= d, pltpu.roll(f, d, 0), 1.0) for d in range(1, t)]
    after = [jnp.where(tt < t - d, pltpu.roll(f, n - d, 0), 1.0) for d in range(1, t)]
    dec = [None, f]
    for d in range(2, t):
        dec.append(dec[-1] * before[d - 2])
    pre = dec[t - 1] * before[t - 2]
    suf = after[0]
    for d in range(1, t - 1):
        suf = suf * after[d]
    total = pre * suf
    qp = (q * pre).astype(bf16)
    kp = (kk * suf).astype(bf16)
    k_back = [kk] + [pltpu.roll(kk, d, 0) for d in range(1, t)]
    v_back = [v] + [pltpu.roll(v, d, 0) for d in range(1, t)]
    for h in range(HG_HEADS):
        sl = slice(h * HG_DK, (h + 1) * HG_DK)
        o = jnp.zeros((n, HG_DV), f32)
        for d in range(t):
            p = q[:, sl] * k_back[d][:, sl]
            if d > 0:
                p = p * dec[d][:, sl]
            a = jnp.sum(p, axis=-1, keepdims=True)
            if d > 0:
                a = jnp.where(tt >= d, a, 0.0)
            o = o + a * v_back[d][:, sl]
        state = s_ref[:, h]
        o = o + _dot(_block_diag(qp[:, sl], bb, t), state.reshape(bb * HG_DK, HG_DV).astype(bf16))
        ds = _dot_ta(kp[:, sl], _block_diag(i_ref[:, sl], bb, t))
        total_t = total[:, sl].T
        for b in range(bb):
            put(so_ref, b, h, state[b] * total_t[:, b * t:b * t + 1] + ds[:, b * HG_DV:(b + 1) * HG_DV])
        o_ref[:, sl] = _head_rms_gate(o, nw_ref[:, sl] * g_ref[:, sl].astype(f32)).astype(bf16)


def _hgrn_sample(q, lf, i, g, norm_w, state, stack, layer, bsz, t):
    hw = HG_HEADS * HG_DK
    bb = SAMPLE_BB
    heads_shape = (HG_HEADS, HG_DK, HG_DV)
    row = lambda: pl.BlockSpec((bb * t, hw), lambda j: (j, 0))
    st_shape, st_spec, put, finish = _stacked_state(layer, state.shape[0], bsz, heads_shape)
    args = [q, lf, i, g, norm_w.reshape(1, -1), state]
    in_specs = [row(), row(), row(), row(), _resident((1, hw)),
                pl.BlockSpec((None, bb) + heads_shape, lambda j: (layer, j, 0, 0, 0))]
    aliases = {}
    if stack is not None:
        aliases = {len(args): 1}
        args.append(stack)
        in_specs.append(pl.BlockSpec(memory_space=pl.ANY))
    return pl.pallas_call(
        functools.partial(_hgrn_sample_body, t=t, put=put, finish=finish),
        grid=(bsz // bb,),
        in_specs=in_specs,
        out_specs=[row(), st_spec],
        out_shape=[jax.ShapeDtypeStruct((bsz * t, hw), bf16), st_shape],
        input_output_aliases=aliases,
        compiler_params=_cparams(("parallel",)),
        name="hgrn_sample",
    )(*args)


def _lru_sample_body(x_ref, gl_ref, cs_ref, h0_ref, cw_ref, cb_ref, wa_ref, ba_ref, wx_ref, bx_ref, lam_ref,
                     o_ref, h_ref, cv_ref, *, t):
    xx = [cs_ref[j] for j in range(CONV_W - 1)] + [x_ref[j] for j in range(t)]
    cw = cw_ref[...]
    half_c_sp = (0.5 * LRU_C) * _softplus(-lam_ref[...])
    h = h0_ref[...]
    for j in range(t):
        xc = cb_ref[...]
        for i in range(CONV_W):
            xc = xc + xx[j + i] * cw[i:i + 1, :]
        a, u = _lru_gates(xc, wa_ref, ba_ref[...], wx_ref, bx_ref[...], half_c_sp)
        h = a * h + u
        o_ref[j] = (h * gl_ref[j].astype(f32)).astype(bf16)
    h_ref[...] = h
    for j in range(CONV_W - 1):
        cv_ref[j] = xx[t + j]


def _lru_sample(xl, gl, cs, h0, cw, cb, wa, ba, wx, bx, lam, bsz, t):
    full = lambda a: pl.BlockSpec(a.shape, lambda i: (0,) * a.ndim)
    args = (xl, gl, cs, h0, cw, cb.reshape(1, -1), wa, ba.reshape(1, -1), wx, bx.reshape(1, -1), lam.reshape(1, -1))
    return pl.pallas_call(
        functools.partial(_lru_sample_body, t=t),
        grid=(1,),
        in_specs=[full(a) for a in args],
        out_specs=[pl.BlockSpec((t, bsz, LRU_W), lambda i: (0, 0, 0)), pl.BlockSpec((bsz, LRU_W), lambda i: (0, 0)),
                   pl.BlockSpec((CONV_W - 1, bsz, LRU_W), lambda i: (0, 0, 0))],
        out_shape=[jax.ShapeDtypeStruct((t, bsz, LRU_W), bf16), jax.ShapeDtypeStruct((bsz, LRU_W), f32),
                   jax.ShapeDtypeStruct((CONV_W - 1, bsz, LRU_W), f32)],
        compiler_params=_cparams(("arbitrary",)),
        name="lru_sample",
    )(*args)


def _rope_tables(pos):
    inv = ROPE_BASE ** (-jnp.arange(0, RET_DK, 2, dtype=f32) / RET_DK)
    ang = pos.astype(f32)[:, None] * inv[None, :]
    cos, sin = jnp.cos(ang), jnp.sin(ang)
    return jnp.concatenate([cos, cos], axis=1), jnp.concatenate([-sin, sin], axis=1)


def _ffn_cast_jobs(raw, which, l):
    return [(f"{which}_wg", _CastJob(raw[f"{which}_w_gate"], l, scale=0.5)),
            (f"{which}_wu", _CastJob(raw[f"{which}_w_up"], l)),
            (f"{which}_wd", _CastJob(raw[f"{which}_w_down"], l))]


def _mixer_cast_jobs(raw, l):
    jobs = [("w_ret", _CastJob(raw["w_in"], l, 0, RET_COLS, raw["ret_scale"])),
            ("w_hg", _CastJob(raw["w_in"], l, RET_COLS, HG_COLS, raw["hg_scale"])),
            ("w_lru", _CastJob(raw["w_in"], l, RET_COLS + HG_COLS, LRU_COLS, raw["lru_scale"]))]
    jobs += [(name, _CastJob(raw[name], l)) for name in ("w_ret_o", "w_hgrn_o", "w_lru_o", "w_mix_out")]
    return jobs + _ffn_cast_jobs(raw, "ffn2", l)


def _trunk(x, pos, w, big, raw, states, bsz, t):
    cos, sin = _rope_tables(pos)
    tile_rows = min(TOKEN_TILE, bsz * t)
    if t < tile_rows:
        cos = jnp.tile(cos, (tile_rows // t, 1))
        sin = jnp.tile(sin, (tile_rows // t, 1))
    news = ([], [], [], [])
    ret_stack = hg_stack = None
    depth = len(big)

    def ffn(x, norm, which, l, final_g=None, jobs=()):
        y = _ffn(x, norm, big[l][f"{which}_wg"], big[l][f"{which}_wu"], big[l][f"{which}_wd"], final_g,
                 [j for _, j in jobs])
        if not jobs:
            return y
        y, cast = y
        return y, {name: arr for (name, _), arr in zip(jobs, cast)}

    for l in range(depth):
        if raw is not None:
            x, cast = ffn(x, w["ffn1_norm"][l], "ffn1", l, jobs=_mixer_cast_jobs(raw, l))
            big[l].update(cast)
        else:
            x = ffn(x, w["ffn1_norm"][l], "ffn1", l)
        q_r, k_r, v_r, g_r, xn = _proj_ret(x, w["mix_norm"][l], big[l]["w_ret"], cos, sin)
        q_h, lf, i_h, g_h = _proj_hgrn(xn, big[l]["w_hg"], w["hgrn_lb_logits"], l)
        lru_w = (w["conv_w"][l], w["conv_b"][l], w["lru_w_a"][l], w["lru_b_a"][l], w["lru_w_x"][l],
                 w["lru_b_x"][l], w["lru_lambda"][l])
        if states is None:
            o_r, ret_new = _ret_prompt(q_r, k_r, v_r, g_r, bsz, t)
            o_h, hg_new = _hgrn_prompt(q_h, lf, i_h, g_h, w["hgrn_norm"][l], bsz, t)
            o_l, gs, h_new, cv_new = _proj_lru_scan(xn, big[l]["w_lru"], w["merge_bias"][l], *lru_w, bsz, t)
            news[0].append(ret_new)
            news[1].append(hg_new)
        else:
            st_ret, st_hg, st_lru, st_conv = states
            o_r, ret_stack = _ret_sample(q_r, k_r, v_r, g_r, st_ret, ret_stack, l, bsz, t)
            o_h, hg_stack = _hgrn_sample(q_h, lf, i_h, g_h, w["hgrn_norm"][l], st_hg, hg_stack, l, bsz, t)
            x_l, g_l, gs = _proj_lru(xn, big[l]["w_lru"], w["merge_bias"][l])
            tm = lambda a: a.reshape(bsz, t, LRU_W).transpose(1, 0, 2)
            o_lt, h_new, cv_t = _lru_sample(tm(x_l), tm(g_l), st_conv[l].transpose(1, 0, 2), st_lru[l], *lru_w, bsz, t)
            o_l = o_lt.transpose(1, 0, 2).reshape(bsz * t, LRU_W)
            cv_new = cv_t.transpose(1, 0, 2)
        x = _mix_out(x, o_r, o_h, o_l, gs, big[l]["w_ret_o"], big[l]["w_hgrn_o"], big[l]["w_lru_o"],
                     big[l]["w_mix_out"])
        last = l == depth - 1
        final_g = w["final_norm"] if last else None
        if raw is not None and not last:
            x, cast = ffn(x, w["ffn2_norm"][l], "ffn2", l, final_g, _ffn_cast_jobs(raw, "ffn1", l + 1))
            big[l + 1].update(cast)
        else:
            x = ffn(x, w["ffn2_norm"][l], "ffn2", l, final_g)
        news[2].append(h_new)
        news[3].append(cv_new)
    if states is None:
        return x, [jnp.stack(a, axis=0) for a in news]
    return x, [ret_stack, hg_stack, jnp.stack(news[2], axis=0), jnp.stack(news[3], axis=0)]


def kernel(x_prompt, x_sample, state_ret, state_hgrn, state_lru, state_conv, ffn1_norm, ffn1_w_gate, ffn1_w_up, ffn1_w_down, mix_norm, w_in, merge_bias, w_ret_o, hgrn_lb_logits, hgrn_norm, w_hgrn_o, conv_w, conv_b, lru_w_a, lru_b_a, lru_w_x, lru_b_x, lru_lambda, w_lru_o, w_mix_out, ffn2_norm, ffn2_w_gate, ffn2_w_up, ffn2_w_down, final_norm):
    c16 = lambda a: a.astype(bf16)
    def halves(*widths_and_flags):
        return jnp.concatenate([jnp.full((wd,), 0.5 if half else 1.0, f32) for wd, half in widths_and_flags])

    nqk, nv, hw = RET_HEADS * RET_DK, RET_HEADS * RET_DV, HG_HEADS * HG_DK
    ret_scale = halves((2 * nqk + nv, False), (nv, True))
    hg_scale = halves((hw, True), (2 * hw, False), (hw, True))
    lru_scale = halves((LRU_W, False), (LRU_W + N_BRANCH * D_MODEL, True))
    w = dict(
        ffn1_norm=ffn1_norm, mix_norm=mix_norm, merge_bias=0.5 * merge_bias, hgrn_lb_logits=hgrn_lb_logits,
        hgrn_norm=hgrn_norm, conv_w=conv_w, conv_b=conv_b, lru_w_a=c16(0.5 * lru_w_a), lru_b_a=0.5 * lru_b_a,
        lru_w_x=c16(0.5 * lru_w_x), lru_b_x=0.5 * lru_b_x, lru_lambda=lru_lambda, ffn2_norm=ffn2_norm,
        final_norm=final_norm,
    )
    raw = dict(ffn1_w_gate=ffn1_w_gate, ffn1_w_up=ffn1_w_up, ffn1_w_down=ffn1_w_down, w_in=w_in,
               w_ret_o=w_ret_o, w_hgrn_o=w_hgrn_o, w_lru_o=w_lru_o, w_mix_out=w_mix_out,
               ffn2_w_gate=ffn2_w_gate, ffn2_w_up=ffn2_w_up, ffn2_w_down=ffn2_w_down,
               ret_scale=ret_scale, hg_scale=hg_scale, lru_scale=lru_scale)
    big = [{} for _ in range(w_in.shape[0])]
    big[0].update(ffn1_wg=c16(0.5 * ffn1_w_gate[0]), ffn1_wu=c16(ffn1_w_up[0]), ffn1_wd=c16(ffn1_w_down[0]))
    bp, tp, _ = x_prompt.shape
    bs, ts, _ = x_sample.shape
    y_p, st_p = _trunk(x_prompt.reshape(bp * tp, D_MODEL), jnp.arange(tp), w, big, raw, None, bp, tp)
    y_s, st_s = _trunk(x_sample.reshape(bs * ts, D_MODEL), PAST_LEN + jnp.arange(ts), w, big, None,
                       (state_ret, state_hgrn, state_lru, state_conv), bs, ts)
    return (y_p.reshape(bp, tp, D_MODEL), y_s.reshape(bs, ts, D_MODEL),
            st_p[0], st_s[0], st_p[1], st_s[1], st_p[2], st_s[2], st_p[3], st_s[3])
```

```python
import functools

import jax
import jax.numpy as jnp
import numpy as np
from jax import lax
from jax.experimental import pallas as pl
from jax.experimental.pallas import tpu as pltpu

f32 = jnp.float32
bf16 = jnp.bfloat16

D_MODEL = 1024
FFN_HIDDEN = 2816
RET_HEADS, RET_DK, RET_DV = 4, 128, 256
HG_HEADS, HG_DK, HG_DV = 8, 128, 128
LRU_W, LRU_BLOCKS, LRU_BD = 1024, 8, 128
CONV_W = 4
LRU_C = 8.0
N_BRANCH = 3
ROPE_BASE = 10000.0
PAST_LEN = 16384
LOG2_E = 1.4426950408889634
EPS = 1e-6
RET_COLS = 2 * RET_HEADS * RET_DK + 2 * RET_HEADS * RET_DV
HG_COLS = 4 * HG_HEADS * HG_DK
LRU_COLS = 2 * LRU_W + N_BRANCH * D_MODEL

V7X_VMEM_BYTES = 64 * 1024 * 1024
VMEM_LIMIT = 56 * 1024 * 1024
SUBLANES = 8

BF16_ROWS = 16
TOKEN_TILE = 1024
FFN_SIDE_TILE = 512
LRU_TILE = 1024
FFN_CHUNK = 256
RET_CHUNK = 256
HG_CHUNK = 128
HG_STEP_CHUNKS = 4
RET_STEP_CHUNKS = 4
HG_LEVELS = ((128, 64), (64, 32), (32, 16), (16, 8), (8, 4), (4, 2))
SAMPLE_BB = 16


def _cparams(sem):
    return pltpu.CompilerParams(dimension_semantics=sem, vmem_limit_bytes=VMEM_LIMIT)


def _resident(shape):
    nd = len(shape)
    return pl.BlockSpec(shape, lambda *_: (0,) * nd, pipeline_mode=pl.Buffered(1))


def _rmsnorm(x, g):
    return x * lax.rsqrt(jnp.mean(x * x, axis=-1, keepdims=True) + EPS) * g


def _dot(a, b):
    return jnp.dot(a, b, preferred_element_type=f32)


def _dot_tb(a, b):
    return lax.dot_general(a, b, (((1,), (1,)), ((), ())), preferred_element_type=f32)


def _dot_ta(a, b):
    return lax.dot_general(a, b, (((0,), (0,)), ((), ())), preferred_element_type=f32)


def _silu_of_half(hx):
    return hx + hx * jnp.tanh(hx)


def _gelu_of_half(hx):
    c = float(np.float32(np.sqrt(2.0 / np.pi)))
    return hx + hx * jnp.tanh(hx * (2.0 * c + (8.0 * c * 0.044715) * (hx * hx)))


def _softplus(x):
    return jnp.maximum(x, 0.0) + jnp.log1p(jnp.exp(-jnp.abs(x)))


def _block_diag(x, nb, rows_per):
    r, w = x.shape
    rb = lax.broadcasted_iota(jnp.int32, (r, nb * w), 0) // rows_per
    cb = lax.broadcasted_iota(jnp.int32, (r, nb * w), 1) // w
    return jnp.where(rb == cb, jnp.tile(x, (1, nb)), jnp.zeros((), x.dtype))


class _CastJob:
    def __init__(self, w, layer, col_off=0, col_w=None, scale=None):
        self.w, self.layer, self.col_off, self.scale = w, layer, col_off, scale
        self.rows = w.shape[1]
        self.col_w = w.shape[2] - col_off if col_w is None else col_w
        self.win = self.col_w if col_off == 0 else int(np.gcd(col_off, self.col_w))
        self.vector_scale = scale is not None and not isinstance(scale, float)

    def specs(self, steps):
        nblk = steps
        while self.rows % nblk or (self.rows // nblk) % BF16_ROWS:
            nblk //= 2
        rep, rb, layer, win, first = steps // nblk, self.rows // nblk, self.layer, self.win, self.col_off // self.win
        ins = [pl.BlockSpec((None, rb, win), lambda i, j=j: (layer, i // rep, first + j))
               for j in range(self.col_w // win)]
        args = [self.w] * len(ins)
        if self.vector_scale:
            ins.append(_resident((1, self.col_w)))
            args.append(self.scale.reshape(1, -1))
        out = pl.BlockSpec((rb, self.col_w), lambda i: (i // rep, 0))
        return args, ins, out, jax.ShapeDtypeStruct((self.rows, self.col_w), bf16)

    def run(self, in_refs, out_ref):
        nwin = self.col_w // self.win
        for j in range(nwin):
            cols = slice(j * self.win, (j + 1) * self.win)
            v = in_refs[j][...]
            if self.vector_scale:
                v = v * in_refs[nwin][:, cols]
            elif self.scale is not None:
                v = v * self.scale
            out_ref[:, cols] = v.astype(bf16)


def _ffn_body(x_ref, g_ref, wg_ref, wu_ref, wd_ref, fg_ref, *rest, final, jobs, job_nin):
    n_side_in = sum(job_nin)
    o_ref = rest[n_side_in]
    x = x_ref[...]
    xn = _rmsnorm(x, g_ref[...]).astype(bf16)
    acc = jnp.zeros(x.shape, f32)
    for c in range(FFN_HIDDEN // FFN_CHUNK):
        sl = slice(c * FFN_CHUNK, (c + 1) * FFN_CHUNK)
        g = _dot(xn, wg_ref[:, sl])
        u = _dot(xn, wu_ref[:, sl])
        h = (_silu_of_half(g) * u).astype(bf16)
        acc = acc + _dot(h, wd_ref[sl, :])
    y = x + 0.5 * acc
    if final:
        y = _rmsnorm(y, fg_ref[...])
    o_ref[...] = y
    start = 0
    for k, (job, nin) in enumerate(zip(jobs, job_nin)):
        job.run(rest[start:start + nin], rest[n_side_in + 1 + k])
        start += nin


def _ffn(x, g, wg, wu, wd, final_g=None, jobs=()):
    n = x.shape[0]
    tm = min(FFN_SIDE_TILE if jobs else TOKEN_TILE, n)
    steps = n // tm
    final = final_g is not None
    fg = final_g if final else g
    row = pl.BlockSpec((tm, D_MODEL), lambda i: (i, 0))
    args = [x, g.reshape(1, -1), wg, wu, wd, fg.reshape(1, -1)]
    in_specs = [row, _resident((1, D_MODEL)), _resident(wg.shape), _resident(wu.shape),
                _resident(wd.shape), _resident((1, D_MODEL))]
    out_specs, out_shape, job_nin = [row], [jax.ShapeDtypeStruct((n, D_MODEL), f32)], []
    for job in jobs:
        a, i_s, o_s, o_sh = job.specs(steps)
        args += a
        in_specs += i_s
        out_specs.append(o_s)
        out_shape.append(o_sh)
        job_nin.append(len(a))
    outs = pl.pallas_call(
        functools.partial(_ffn_body, final=final, jobs=tuple(jobs), job_nin=tuple(job_nin)),
        grid=(steps,),
        in_specs=in_specs,
        out_specs=out_specs,
        out_shape=out_shape,
        compiler_params=_cparams(("arbitrary",) if jobs else ("parallel",)),
        name="ffn",
    )(*args)
    return (outs[0], list(outs[1:])) if jobs else outs[0]


def _proj_ret_body(x_ref, g_ref, w_ref, cos_ref, sin_ref, q_ref, k_ref, v_ref, gr_ref, xn_ref):
    xn = _rmsnorm(x_ref[...], g_ref[...]).astype(bf16)
    xn_ref[...] = xn
    cos = cos_ref[...]
    sin = sin_ref[...]
    nqk = RET_HEADS * RET_DK

    def rot(z):
        return z * cos + pltpu.roll(z, RET_DK // 2, 1) * sin

    for c in range(nqk // 256):
        zq = _dot(xn, w_ref[:, c * 256:(c + 1) * 256])
        zk = _dot(xn, w_ref[:, nqk + c * 256:nqk + (c + 1) * 256])
        for j in range(2):
            sl = slice(c * 256 + j * 128, c * 256 + (j + 1) * 128)
            q_ref[:, sl] = rot(zq[:, j * 128:(j + 1) * 128]).astype(bf16)
            k_ref[:, sl] = (rot(zk[:, j * 128:(j + 1) * 128]) * (RET_DK ** -0.5)).astype(bf16)
    nv = RET_HEADS * RET_DV
    for c in range(nv // 256):
        sl = slice(c * 256, (c + 1) * 256)
        v_ref[:, sl] = _dot(xn, w_ref[:, 2 * nqk + c * 256:2 * nqk + (c + 1) * 256]).astype(bf16)
        zg = _dot(xn, w_ref[:, 2 * nqk + nv + c * 256:2 * nqk + nv + (c + 1) * 256])
        gr_ref[:, sl] = _silu_of_half(zg).astype(bf16)


def _proj_ret(x, g, w, cos, sin):
    n = x.shape[0]
    tm = min(TOKEN_TILE, n)
    row = lambda w_: pl.BlockSpec((tm, w_), lambda i: (i, 0))
    nqk, nv = RET_HEADS * RET_DK, RET_HEADS * RET_DV
    assert cos.shape[0] % tm == 0
    period = cos.shape[0] // tm
    rope = pl.BlockSpec((tm, RET_DK), lambda i: (i % period, 0))
    return pl.pallas_call(
        _proj_ret_body,
        grid=(n // tm,),
        in_specs=[row(D_MODEL), _resident((1, D_MODEL)), _resident(w.shape), rope, rope],
        out_specs=[row(nqk), row(nqk), row(nv), row(nv), row(D_MODEL)],
        out_shape=[jax.ShapeDtypeStruct((n, nqk), bf16), jax.ShapeDtypeStruct((n, nqk), bf16),
                   jax.ShapeDtypeStruct((n, nv), bf16), jax.ShapeDtypeStruct((n, nv), bf16),
                   jax.ShapeDtypeStruct((n, D_MODEL), bf16)],
        compiler_params=_cparams(("parallel",)),
        name="proj_ret",
    )(x, g.reshape(1, -1), w, cos, sin)


def _proj_hgrn_body(xn_ref, w_ref, lbl_ref, q_ref, lf_ref, i_ref, gh_ref, *, layer):
    xn = xn_ref[...]
    hw = HG_HEADS * HG_DK
    if layer > 0:
        lg = lbl_ref[...]
        e = jnp.exp(lg - jnp.max(lg, axis=0, keepdims=True))
        sm = e / jnp.sum(e, axis=0, keepdims=True)
        lb = jnp.sum(sm[1:layer + 1], axis=0, keepdims=True)
    for c in range(hw // 256):
        sl = slice(c * 256, (c + 1) * 256)
        q_ref[:, sl] = _silu_of_half(_dot(xn, w_ref[:, sl])).astype(bf16)
        zf = _dot(xn, w_ref[:, hw + c * 256:hw + (c + 1) * 256])
        t = jnp.exp2(jnp.abs(zf) * -LOG2_E)
        num = jnp.minimum(zf, 0.0)
        if layer > 0:
            lbc = lb[:, sl]
            num = jnp.maximum(jnp.log(jnp.where(zf < 0.0, lbc + t, 1.0 + lbc * t)), num)
        lf_ref[:, sl] = num - jnp.log(1.0 + t)
        i_ref[:, sl] = _dot(xn, w_ref[:, 2 * hw + c * 256:2 * hw + (c + 1) * 256]).astype(bf16)
        gh_ref[:, sl] = _silu_of_half(_dot(xn, w_ref[:, 3 * hw + c * 256:3 * hw + (c + 1) * 256])).astype(bf16)


def _proj_hgrn(xn, w, lb_logits, layer):
    n = xn.shape[0]
    tm = min(TOKEN_TILE, n)
    hw = HG_HEADS * HG_DK
    row = lambda w_: pl.BlockSpec((tm, w_), lambda i: (i, 0))
    return pl.pallas_call(
        functools.partial(_proj_hgrn_body, layer=layer),
        grid=(n // tm,),
        in_specs=[row(D_MODEL), _resident(w.shape), _resident(lb_logits.shape)],
        out_specs=[row(hw), row(hw), row(hw), row(hw)],
        out_shape=[jax.ShapeDtypeStruct((n, hw), bf16), jax.ShapeDtypeStruct((n, hw), f32),
                   jax.ShapeDtypeStruct((n, hw), bf16), jax.ShapeDtypeStruct((n, hw), bf16)],
        compiler_params=_cparams(("parallel",)),
        name="proj_hgrn",
    )(xn, w, lb_logits)


def _merge_gates(xn, w_ref, mb_ref, gs_ref, chunks=range(N_BRANCH * D_MODEL // 256)):
    for c in chunks:
        sl = slice(c * 256, (c + 1) * 256)
        half_z = _dot(xn, w_ref[:, 2 * LRU_W + c * 256:2 * LRU_W + (c + 1) * 256]) + mb_ref[:, sl]
        gs_ref[:, sl] = (0.5 + 0.5 * jnp.tanh(half_z)).astype(bf16)


def _proj_lru_body(xn_ref, w_ref, mb_ref, xl_ref, gl_ref, gs_ref):
    xn = xn_ref[...]
    for c in range(LRU_W // 256):
        sl = slice(c * 256, (c + 1) * 256)
        xl_ref[:, sl] = _dot(xn, w_ref[:, sl])
        zg = _dot(xn, w_ref[:, LRU_W + c * 256:LRU_W + (c + 1) * 256])
        gl_ref[:, sl] = _gelu_of_half(zg).astype(bf16)
    _merge_gates(xn, w_ref, mb_ref, gs_ref)


def _proj_lru(xn, w, merge_bias):
    n = xn.shape[0]
    tm = min(TOKEN_TILE, n)
    row = lambda w_: pl.BlockSpec((tm, w_), lambda i: (i, 0))
    ng = N_BRANCH * D_MODEL
    return pl.pallas_call(
        _proj_lru_body,
        grid=(n // tm,),
        in_specs=[row(D_MODEL), _resident(w.shape), _resident((1, ng))],
        out_specs=[row(LRU_W), row(LRU_W), row(ng)],
        out_shape=[jax.ShapeDtypeStruct((n, LRU_W), f32), jax.ShapeDtypeStruct((n, LRU_W), bf16),
                   jax.ShapeDtypeStruct((n, ng), bf16)],
        compiler_params=_cparams(("parallel",)),
        name="proj_lru",
    )(xn, w, merge_bias.reshape(1, -1))


def _mix_out_body(x_ref, or_ref, oh_ref, ol_ref, gs_ref, wr_ref, wh_ref, wl_ref, wo_ref, o_ref):
    d = D_MODEL
    merged = gs_ref[:, 0:d].astype(f32) * _dot(or_ref[...], wr_ref[...])
    merged = merged + gs_ref[:, d:2 * d].astype(f32) * _dot(oh_ref[...], wh_ref[...])
    merged = merged + gs_ref[:, 2 * d:3 * d].astype(f32) * _dot(ol_ref[...], wl_ref[...])
    o_ref[...] = x_ref[...] + _dot(merged.astype(bf16), wo_ref[...])


def _mix_out(x, o_r, o_h, o_l, gs, wr, wh, wl, wo):
    n = x.shape[0]
    tm = min(TOKEN_TILE, n)
    row = lambda w_: pl.BlockSpec((tm, w_), lambda i: (i, 0))
    return pl.pallas_call(
        _mix_out_body,
        grid=(n // tm,),
        in_specs=[row(D_MODEL), row(D_MODEL), row(D_MODEL), row(D_MODEL), row(N_BRANCH * D_MODEL),
                  _resident(wr.shape), _resident(wh.shape), _resident(wl.shape), _resident(wo.shape)],
        out_specs=row(D_MODEL),
        out_shape=jax.ShapeDtypeStruct((n, D_MODEL), f32),
        compiler_params=_cparams(("parallel",)),
        name="mix_out",
    )(x, o_r, o_h, o_l, gs, wr, wh, wl, wo)


def _ret_log_gamma():
    return jnp.log1p(-jnp.exp2(-5.0 - jnp.arange(RET_HEADS, dtype=f32)))


def _ret_tables(c):
    log_g = _ret_log_gamma()
    idx = jnp.arange(c, dtype=f32)
    diff = idx[:, None] - idx[None, :]
    causal = diff >= 0
    d_intra = jnp.where(causal, jnp.exp(log_g[:, None, None] * jnp.where(causal, diff, 0.0)), 0.0)
    q_dec = jnp.exp(log_g[:, None] * (idx + 1.0))[:, :, None]
    k_dec = jnp.exp(log_g[:, None] * (c - 1.0 - idx))[:, :, None]
    s_dec = jnp.exp(log_g * c)
    return d_intra, q_dec, k_dec, s_dec


def _head_rms_gate(o, gate):
    return o * lax.rsqrt(jnp.mean(o * o, axis=-1, keepdims=True) + EPS) * gate


def _ret_prompt_body(q_ref, k_ref, v_ref, g_ref, dm_ref, qd_ref, kd_ref, sd_ref, o_ref, st_ref, s_scr):
    c = pl.program_id(1)

    @pl.when(c == 0)
    def _():
        s_scr[...] = jnp.zeros(s_scr.shape, f32)

    heads = range(RET_HEADS)
    ks = [slice(h * RET_DK, (h + 1) * RET_DK) for h in heads]
    vs = [slice(h * RET_DV, (h + 1) * RET_DV) for h in heads]
    for r0 in range(0, q_ref.shape[0], RET_CHUNK):
        rows = slice(r0, r0 + RET_CHUNK)
        scores = [(_dot_tb(q_ref[rows, ks[h]], k_ref[rows, ks[h]]) * dm_ref[h]).astype(bf16) for h in heads]
        inter = [qd_ref[h] * _dot(q_ref[rows, ks[h]], s_scr[h].astype(bf16)) for h in heads]
        outs = [_dot(scores[h], v_ref[rows, vs[h]]) + inter[h] for h in heads]
        for h in heads:
            kd = (k_ref[rows, ks[h]].astype(f32) * kd_ref[h]).astype(bf16)
            s_scr[h] = s_scr[h] * sd_ref[h] + _dot_ta(kd, v_ref[rows, vs[h]])
        for h in heads:
            o_ref[rows, vs[h]] = _head_rms_gate(outs[h], g_ref[rows, vs[h]].astype(f32)).astype(bf16)

    @pl.when(c == pl.num_programs(1) - 1)
    def _():
        st_ref[0] = s_scr[...]


def _ret_prompt(q, k, v, g, bsz, t):
    c = RET_CHUNK
    rows = RET_STEP_CHUNKS * c
    nc = t // rows
    nqk, nv = RET_HEADS * RET_DK, RET_HEADS * RET_DV
    d_intra, q_dec, k_dec, s_dec = _ret_tables(c)
    sd = jnp.broadcast_to(s_dec[:, None, None], (RET_HEADS, 1, RET_DV))
    row = lambda w_: pl.BlockSpec((rows, w_), lambda b, i: (b * nc + i, 0))
    return pl.pallas_call(
        _ret_prompt_body,
        grid=(bsz, nc),
        in_specs=[row(nqk), row(nqk), row(nv), row(nv), _resident(d_intra.shape), _resident(q_dec.shape),
                  _resident(k_dec.shape), _resident(sd.shape)],
        out_specs=[row(nv), pl.BlockSpec((1, RET_HEADS, RET_DK, RET_DV), lambda b, i: (b, 0, 0, 0))],
        out_shape=[jax.ShapeDtypeStruct((bsz * t, nv), bf16),
                   jax.ShapeDtypeStruct((bsz, RET_HEADS, RET_DK, RET_DV), f32)],
        scratch_shapes=[pltpu.VMEM((RET_HEADS, RET_DK, RET_DV), f32)],
        compiler_params=_cparams(("parallel", "arbitrary")),
        name="ret_prompt",
    )(q, k, v, g, d_intra, q_dec, k_dec, sd)


def _hg_level_masks(c):
    t = np.arange(c)[:, None]
    s = np.arange(c)[None, :]
    ms = [((t // blk == s // blk) & (t % blk >= half) & (s % blk < half)) for blk, half in HG_LEVELS]
    return jnp.asarray(np.stack(ms).astype(np.float32))


def _boundary_rows(b, blk, half):
    c, w = b.shape
    if half >= SUBLANES:
        return jnp.concatenate(
            [jnp.broadcast_to(b[j * blk + half - 1:j * blk + half, :], (blk, w)) for j in range(c // blk)], axis=0)
    b3 = b.reshape(c // SUBLANES, SUBLANES, w)
    row8 = lax.broadcasted_iota(jnp.int32, (1, SUBLANES, 1), 1)
    out = b3[:, half - 1:half, :]
    for j in range(1, SUBLANES // blk):
        out = jnp.where(row8 >= j * blk, b3[:, j * blk + half - 1:j * blk + half, :], out)
    return jnp.broadcast_to(out, b3.shape).reshape(c, w)


def _upper_lower(up, low, blk, half):
    c = up.shape[0]
    if half >= SUBLANES:
        parts = []
        for j in range(c // blk):
            parts += [low[j * blk:j * blk + half, :], up[j * blk + half:(j + 1) * blk, :]]
        return jnp.concatenate(parts, axis=0)
    upper = lax.broadcasted_iota(jnp.int32, (c, 1), 0) % blk >= half
    return jnp.where(upper, up, low)


def _hgrn_chunk(q_ref, lf_ref, i_ref, g_ref, nw_ref, tri_ref, mask_ref, o_ref, s_scr, rows):
    c = HG_CHUNK
    lf_all = lf_ref[rows, :]
    hi = lf_all.astype(bf16)
    lo = (lf_all - hi.astype(f32)).astype(bf16)
    tri = tri_ref[...]
    b2_all = (_dot(tri, hi) + _dot(tri, lo)) * LOG2_E
    odd = lax.broadcasted_iota(jnp.int32, (c, 1), 0) % 2 == 1

    f_all = jnp.exp(lf_all)
    kk_all = 1.0 - f_all
    q_all = q_ref[rows, :].astype(f32)
    a_mats = [None] * HG_HEADS
    for lev, (blk, half) in enumerate(HG_LEVELS):
        keep = mask_ref[lev] > 0.5
        for h in range(HG_HEADS):
            sl = slice(h * HG_DK, (h + 1) * HG_DK)
            b2 = b2_all[:, sl]
            bm = _boundary_rows(b2, blk, half)
            e = jnp.exp2(_upper_lower(b2 - bm, bm - b2, blk, half))
            m = (_upper_lower(q_all[:, sl], kk_all[:, sl], blk, half) * e).astype(bf16)
            p = _dot_tb(m, m)
            a_mats[h] = jnp.where(keep, p, 0.0) if lev == 0 else jnp.where(keep, p, a_mats[h])
    a_mats = [a.astype(bf16) for a in a_mats]

    for h in range(HG_HEADS):
        sl = slice(h * HG_DK, (h + 1) * HG_DK)
        b2 = b2_all[:, sl]
        q = q_all[:, sl]
        f = f_all[:, sl]
        kk = kk_all[:, sl]
        v16 = i_ref[rows, sl]
        o = _dot(a_mats[h], v16)
        vf = v16.astype(f32)
        a0 = jnp.sum(q * kk, axis=-1, keepdims=True)
        a1 = jnp.where(odd, jnp.sum(q * f * pltpu.roll(kk, 1, 0), axis=-1, keepdims=True), 0.0)
        o = o + a0 * vf + a1 * pltpu.roll(vf, 1, 0)
        st = s_scr[h]
        o = o + _dot_tb((q * jnp.exp2(b2)).astype(bf16), st.astype(bf16))
        bl2 = b2[c - 1:c, :]
        kdec = (kk * jnp.exp2(bl2 - b2)).astype(bf16)
        s_scr[h] = st * jnp.exp2(bl2) + _dot_ta(v16, kdec)
        o_ref[rows, sl] = _head_rms_gate(o, nw_ref[:, sl] * g_ref[rows, sl].astype(f32)).astype(bf16)


def _hgrn_prompt_body(q_ref, lf_ref, i_ref, g_ref, nw_ref, tri_ref, mask_ref, o_ref, st_ref, s_scr):
    ci = pl.program_id(1)
    c = HG_CHUNK

    @pl.when(ci == 0)
    def _():
        s_scr[...] = jnp.zeros(s_scr.shape, f32)

    for r0 in range(0, q_ref.shape[0], c):
        _hgrn_chunk(q_ref, lf_ref, i_ref, g_ref, nw_ref, tri_ref, mask_ref, o_ref, s_scr, slice(r0, r0 + c))

    @pl.when(ci == pl.num_programs(1) - 1)
    def _():
        for h in range(HG_HEADS):
            st_ref[0, h] = s_scr[h].T


def _hgrn_prompt(q, lf, i, g, norm_w, bsz, t):
    c = HG_CHUNK
    rows = HG_STEP_CHUNKS * c
    nc = t // rows
    hw = HG_HEADS * HG_DK
    tri = jnp.asarray(np.tril(np.ones((c, c), np.float32))).astype(bf16)
    masks = _hg_level_masks(c)
    row = lambda: pl.BlockSpec((rows, hw), lambda b, j: (b * nc + j, 0))
    return pl.pallas_call(
        _hgrn_prompt_body,
        grid=(bsz, nc),
        in_specs=[row(), row(), row(), row(), _resident((1, hw)), _resident(tri.shape), _resident(masks.shape)],
        out_specs=[row(), pl.BlockSpec((1, HG_HEADS, HG_DK, HG_DV), lambda b, j: (b, 0, 0, 0))],
        out_shape=[jax.ShapeDtypeStruct((bsz * t, hw), bf16),
                   jax.ShapeDtypeStruct((bsz, HG_HEADS, HG_DK, HG_DV), f32)],
        scratch_shapes=[pltpu.VMEM((HG_HEADS, HG_DV, HG_DK), f32)],
        compiler_params=_cparams(("parallel", "arbitrary")),
        name="hgrn_prompt",
    )(q, lf, i, g, norm_w.reshape(1, -1), tri, masks)


def _lru_gate_math(xc, half_ya, half_yx, half_c_sp):
    neg_log_a = half_c_sp + half_c_sp * jnp.tanh(half_ya)
    a = jnp.exp(-neg_log_a)
    y = jnp.tanh(neg_log_a) * (1.0 + a * a)
    scale = jnp.where(y > 0.0, y * lax.rsqrt(y), 0.0)
    hx = 0.5 * xc
    return a, scale * (hx + hx * jnp.tanh(half_yx))


def _lru_gate_block(xc, wa, ba, wx, bx, half_c_sp):
    xb = xc.astype(bf16)
    return _lru_gate_math(xc, _dot(xb, wa) + ba, _dot(xb, wx) + bx, half_c_sp)


def _lru_gates(xc, wa_ref, ba, wx_ref, bx, half_c_sp):
    parts = []
    for n in range(LRU_BLOCKS):
        sl = slice(n * LRU_BD, (n + 1) * LRU_BD)
        parts.append(_lru_gate_block(xc[:, sl], wa_ref[n], ba[:, sl], wx_ref[n], bx[:, sl], half_c_sp[:, sl]))
    return jnp.concatenate([p[0] for p in parts], axis=1), jnp.concatenate([p[1] for p in parts], axis=1)


def _strided_rows(g, rows):
    j_count = rows // SUBLANES
    k, j0 = divmod(SUBLANES * g, j_count)
    return pl.ds(SUBLANES * j0 + k, SUBLANES, stride=SUBLANES)


def _shift_time(xp, tail, d):
    rows = xp.shape[0]
    row8 = lax.broadcasted_iota(jnp.int32, (SUBLANES, 1), 0)
    last = pltpu.roll(xp[rows - SUBLANES * d:rows, :].reshape(d, SUBLANES, xp.shape[1]), 1, 1)
    head = [jnp.where(row8 == 0, tail[SUBLANES - d + j:SUBLANES - d + j + 1, :], last[j]) for j in range(d)]
    return jnp.concatenate(head + [xp[0:rows - SUBLANES * d, :]], axis=0)


def _proj_lru_scan_body(xn_ref, w_ref, mb_ref, cw_ref, cb_ref, wa_ref, ba_ref, wx_ref, bx_ref, lam_ref,
                        o_ref, gs_ref, h_ref, cv_ref, xp_scr, hp_scr, ap_scr, tail_scr, h_scr):
    ti = pl.program_id(1)
    tm = xn_ref.shape[0]
    nj = tm // SUBLANES

    @pl.when(ti == 0)
    def _():
        tail_scr[...] = jnp.zeros(tail_scr.shape, f32)
        h_scr[...] = jnp.zeros(h_scr.shape, f32)

    xn = xn_ref[...]
    for c in range(LRU_W // 256):
        z = _dot(xn, w_ref[:, c * 256:(c + 1) * 256])
        for half in range(2):
            for g in range(nj):
                xp_scr[2 * c + half, _strided_rows(g, tm), :] = (
                    z[SUBLANES * g:SUBLANES * (g + 1), half * LRU_BD:(half + 1) * LRU_BD])
    cw = cw_ref[...]
    cb = cb_ref[...]
    ba = ba_ref[...]
    bx = bx_ref[...]
    half_c_sp = (0.5 * LRU_C) * _softplus(-lam_ref[...])
    n_gate = N_BRANCH * D_MODEL // 256
    n_pair = LRU_W // 256
    for p in range(n_pair):
        blocks = (2 * p, 2 * p + 1)
        sls = [slice(n * LRU_BD, (n + 1) * LRU_BD) for n in blocks]
        xcs, yas, yxs = [], [], []
        for n, sl in zip(blocks, sls):
            xp = xp_scr[n]
            tail = tail_scr[:, sl]
            xc = cb[:, sl] + xp * cw[CONV_W - 1:CONV_W, sl]
            for d in range(1, CONV_W):
                xc = xc + _shift_time(xp, tail, d) * cw[CONV_W - 1 - d:CONV_W - d, sl]
            xb = xc.astype(bf16)
            xcs.append(xc)
            yas.append(_dot(xb, wa_ref[n]) + ba[:, sl])
            yxs.append(_dot(xb, wx_ref[n]) + bx[:, sl])
        hseg = [jnp.zeros((SUBLANES, LRU_BD), f32) for _ in blocks]
        aseg = [jnp.ones((SUBLANES, LRU_BD), f32) for _ in blocks]
        for j in range(nj):
            rows = slice(SUBLANES * j, SUBLANES * (j + 1))
            for i, (n, sl) in enumerate(zip(blocks, sls)):
                a, u = _lru_gate_math(xcs[i][rows, :], yas[i][rows, :], yxs[i][rows, :], half_c_sp[:, sl])
                hseg[i] = a * hseg[i] + u
                aseg[i] = a * aseg[i]
                hp_scr[n, rows, :] = hseg[i]
                ap_scr[n, rows, :] = aseg[i]
        for i, (n, sl) in enumerate(zip(blocks, sls)):
            s = h_scr[0:1, sl]
            starts = []
            for k in range(SUBLANES):
                starts.append(s)
                s = aseg[i][k:k + 1, :] * s + hseg[i][k:k + 1, :]
            h_scr[:, sl] = jnp.broadcast_to(s, (h_scr.shape[0], LRU_BD))
            start8 = jnp.concatenate(starts, axis=0)
            hp_scr[n] = (hp_scr[n].reshape(nj, SUBLANES, LRU_BD)
                         + ap_scr[n].reshape(nj, SUBLANES, LRU_BD) * start8).reshape(tm, LRU_BD)
            tail_scr[:, sl] = xp_scr[n, _strided_rows(nj - 1, tm), :]
        zg = _gelu_of_half(_dot(xn, w_ref[:, LRU_W + p * 256:LRU_W + (p + 1) * 256]))
        for half in range(2):
            n = 2 * p + half
            hs = jnp.concatenate([hp_scr[n, _strided_rows(g, tm), :] for g in range(nj)], axis=0)
            o_ref[:, n * LRU_BD:(n + 1) * LRU_BD] = (hs * zg[:, half * LRU_BD:(half + 1) * LRU_BD]).astype(bf16)
        _merge_gates(xn, w_ref, mb_ref, gs_ref, range(p * n_gate // n_pair, (p + 1) * n_gate // n_pair))

    @pl.when(ti == pl.num_programs(1) - 1)
    def _():
        h_ref[0] = h_scr[0:1, :]
        cv_ref[0] = tail_scr[SUBLANES - (CONV_W - 1):SUBLANES, :]


def _proj_lru_scan(xn, w, merge_bias, cw, cb, wa, ba, wx, bx, lam, bsz, t):
    tm = min(LRU_TILE, t)
    assert tm % (SUBLANES * SUBLANES) == 0
    nt = t // tm
    ng = N_BRANCH * D_MODEL
    row = lambda w_: pl.BlockSpec((tm, w_), lambda b, j: (b * nt + j, 0))
    vec = _resident((1, LRU_W))
    o, gs, h, cv = pl.pallas_call(
        _proj_lru_scan_body,
        grid=(bsz, nt),
        in_specs=[row(D_MODEL), _resident(w.shape), _resident((1, ng)),
                  _resident((CONV_W, LRU_W)), vec, _resident(wa.shape), vec, _resident(wx.shape), vec, vec],
        out_specs=[row(LRU_W), row(ng), pl.BlockSpec((1, 1, LRU_W), lambda b, j: (b, 0, 0)),
                   pl.BlockSpec((1, CONV_W - 1, LRU_W), lambda b, j: (b, 0, 0))],
        out_shape=[jax.ShapeDtypeStruct((bsz * t, LRU_W), bf16), jax.ShapeDtypeStruct((bsz * t, ng), bf16),
                   jax.ShapeDtypeStruct((bsz, 1, LRU_W), f32), jax.ShapeDtypeStruct((bsz, CONV_W - 1, LRU_W), f32)],
        scratch_shapes=[pltpu.VMEM((LRU_BLOCKS, tm, LRU_BD), f32)] * 3 + [pltpu.VMEM((SUBLANES, LRU_W), f32)] * 2,
        compiler_params=_cparams(("parallel", "arbitrary")),
        name="proj_lru_scan",
    )(xn, w, merge_bias.reshape(1, -1), cw, cb.reshape(1, -1), wa, ba.reshape(1, -1), wx,
      bx.reshape(1, -1), lam.reshape(1, -1))
    return o, gs, h[:, 0], cv


def _ret_sample_tables(t):
    log_g = _ret_log_gamma()
    r = jnp.arange(SAMPLE_BB * t)
    tt = (r % t).astype(f32)
    same = (r[:, None] // t) == (r[None, :] // t)
    diff = tt[:, None] - tt[None, :]
    ok = same & (diff >= 0)
    dm = jnp.where(ok, jnp.exp(log_g[:, None, None] * jnp.where(ok, diff, 0.0)), 0.0)
    qd = jnp.exp(log_g[:, None] * (tt + 1.0))[:, :, None]
    kd = jnp.exp(log_g[:, None] * (t - 1.0 - tt))[:, :, None]
    sd = jnp.exp(log_g * t)
    return dm, qd, kd, sd


def _stacked_state(layer, depth, bsz, heads_shape):
    shape = jax.ShapeDtypeStruct((depth, bsz) + heads_shape, f32)
    if layer == 0:
        spec = pl.BlockSpec((depth, SAMPLE_BB) + heads_shape, lambda i: (0, i, 0, 0, 0))

        def put(so_ref, b, h, val):
            so_ref[0, b, h] = val

        def finish(so_ref):
            for l in range(1, depth):
                so_ref[l] = jnp.zeros((SAMPLE_BB,) + heads_shape, f32)
    else:
        spec = pl.BlockSpec((None, SAMPLE_BB) + heads_shape, lambda i: (layer, i, 0, 0, 0))

        def put(so_ref, b, h, val):
            so_ref[b, h] = val

        def finish(so_ref):
            pass
    return shape, spec, put, finish


def _ret_sample_body(q_ref, k_ref, v_ref, g_ref, s_ref, dm_ref, qd_ref, kd_ref, sd_ref, *rest, t, put, finish):
    o_ref, so_ref = rest[-2:]
    bb = SAMPLE_BB
    finish(so_ref)
    for h in range(RET_HEADS):
        ks = slice(h * RET_DK, (h + 1) * RET_DK)
        vs = slice(h * RET_DV, (h + 1) * RET_DV)
        q = q_ref[:, ks]
        k = k_ref[:, ks]
        v = v_ref[:, vs]
        state = s_ref[:, h]
        scores = (_dot_tb(q, k) * dm_ref[h]).astype(bf16)
        inter = _dot(_block_diag(q, bb, t), state.reshape(bb * RET_DK, RET_DV).astype(bf16))
        o = _dot(scores, v) + qd_ref[h] * inter
        kd = (k.astype(f32) * kd_ref[h]).astype(bf16)
        ds = _dot_ta(kd, _block_diag(v, bb, t))
        for b in range(bb):
            put(so_ref, b, h, state[b] * sd_ref[h] + ds[:, b * RET_DV:(b + 1) * RET_DV])
        o_ref[:, vs] = _head_rms_gate(o, g_ref[:, vs].astype(f32)).astype(bf16)


def _ret_sample(q, k, v, g, state, stack, layer, bsz, t):
    nqk, nv = RET_HEADS * RET_DK, RET_HEADS * RET_DV
    bb = SAMPLE_BB
    heads_shape = (RET_HEADS, RET_DK, RET_DV)
    dm, qd, kd, s_dec = _ret_sample_tables(t)
    sd = jnp.broadcast_to(s_dec[:, None, None], (RET_HEADS, 1, RET_DV))
    row = lambda w_: pl.BlockSpec((bb * t, w_), lambda i: (i, 0))
    st_shape, st_spec, put, finish = _stacked_state(layer, state.shape[0], bsz, heads_shape)
    args = [q, k, v, g, state, dm, qd, kd, sd]
    in_specs = [row(nqk), row(nqk), row(nv), row(nv),
                pl.BlockSpec((None, bb) + heads_shape, lambda i: (layer, i, 0, 0, 0)),
                _resident(dm.shape), _resident(qd.shape), _resident(kd.shape), _resident(sd.shape)]
    aliases = {}
    if stack is not None:
        aliases = {len(args): 1}
        args.append(stack)
        in_specs.append(pl.BlockSpec(memory_space=pl.ANY))
    return pl.pallas_call(
        functools.partial(_ret_sample_body, t=t, put=put, finish=finish),
        grid=(bsz // bb,),
        in_specs=in_specs,
        out_specs=[row(nv), st_spec],
        out_shape=[jax.ShapeDtypeStruct((bsz * t, nv), bf16), st_shape],
        input_output_aliases=aliases,
        compiler_params=_cparams(("parallel",)),
        name="ret_sample",
    )(*args)


def _hgrn_sample_body(q_ref, lf_ref, i_ref, g_ref, nw_ref, s_ref, *rest, t, put, finish):
    assert t >= 2
    o_ref, so_ref = rest[-2:]
    finish(so_ref)
    bb = SAMPLE_BB
    n = bb * t
    tt = lax.broadcasted_iota(jnp.int32, (n, 1), 0) % t
    f = jnp.exp(lf_ref[...])
    kk = 1.0 - f
    q = q_ref[...].astype(f32)
    v = i_ref[...].astype(f32)
    before = [jnp.where(tt >= d, pltpu.roll(f, d, 0), 1.0) for d in range(1, t)]
    after = [jnp.where(tt < t - d, pltpu.roll(f, n - d, 0), 1.0) for d in range(1, t)]
    dec = [None, f]
    for d in range(2, t):
        dec.append(dec[-1] * before[d - 2])
    pre = dec[t - 1] * before[t - 2]
    suf = after[0]
    for d in range(1, t - 1):
        suf = suf * after[d]
    total = pre * suf
    qp = (q * pre).astype(bf16)
    kp = (kk * suf).astype(bf16)
    k_back = [kk] + [pltpu.roll(kk, d, 0) for d in range(1, t)]
    v_back = [v] + [pltpu.roll(v, d, 0) for d in range(1, t)]
    for h in range(HG_HEADS):
        sl = slice(h * HG_DK, (h + 1) * HG_DK)
        o = jnp.zeros((n, HG_DV), f32)
        for d in range(t):
            p = q[:, sl] * k_back[d][:, sl]
            if d > 0:
                p = p * dec[d][:, sl]
            a = jnp.sum(p, axis=-1, keepdims=True)
            if d > 0:
                a = jnp.where(tt >= d, a, 0.0)
            o = o + a * v_back[d][:, sl]
        state = s_ref[:, h]
        o = o + _dot(_block_diag(qp[:, sl], bb, t), state.reshape(bb * HG_DK, HG_DV).astype(bf16))
        ds = _dot_ta(kp[:, sl], _block_diag(i_ref[:, sl], bb, t))
        total_t = total[:, sl].T
        for b in range(bb):
            put(so_ref, b, h, state[b] * total_t[:, b * t:b * t + 1] + ds[:, b * HG_DV:(b + 1) * HG_DV])
        o_ref[:, sl] = _head_rms_gate(o, nw_ref[:, sl] * g_ref[:, sl].astype(f32)).astype(bf16)


def _hgrn_sample(q, lf, i, g, norm_w, state, stack, layer, bsz, t):
    hw = HG_HEADS * HG_DK
    bb = SAMPLE_BB
    heads_shape = (HG_HEADS, HG_DK, HG_DV)
    row = lambda: pl.BlockSpec((bb * t, hw), lambda j: (j, 0))
    st_shape, st_spec, put, finish = _stacked_state(layer, state.shape[0], bsz, heads_shape)
    args = [q, lf, i, g, norm_w.reshape(1, -1), state]
    in_specs = [row(), row(), row(), row(), _resident((1, hw)),
                pl.BlockSpec((None, bb) + heads_shape, lambda j: (layer, j, 0, 0, 0))]
    aliases = {}
    if stack is not None:
        aliases = {len(args): 1}
        args.append(stack)
        in_specs.append(pl.BlockSpec(memory_space=pl.ANY))
    return pl.pallas_call(
        functools.partial(_hgrn_sample_body, t=t, put=put, finish=finish),
        grid=(bsz // bb,),
        in_specs=in_specs,
        out_specs=[row(), st_spec],
        out_shape=[jax.ShapeDtypeStruct((bsz * t, hw), bf16), st_shape],
        input_output_aliases=aliases,
        compiler_params=_cparams(("parallel",)),
        name="hgrn_sample",
    )(*args)


def _lru_sample_body(x_ref, gl_ref, cs_ref, h0_ref, cw_ref, cb_ref, wa_ref, ba_ref, wx_ref, bx_ref, lam_ref,
                     o_ref, h_ref, cv_ref, *, t):
    xx = [cs_ref[j] for j in range(CONV_W - 1)] + [x_ref[j] for j in range(t)]
    cw = cw_ref[...]
    half_c_sp = (0.5 * LRU_C) * _softplus(-lam_ref[...])
    h = h0_ref[...]
    for j in range(t):
        xc = cb_ref[...]
        for i in range(CONV_W):
            xc = xc + xx[j + i] * cw[i:i + 1, :]
        a, u = _lru_gates(xc, wa_ref, ba_ref[...], wx_ref, bx_ref[...], half_c_sp)
        h = a * h + u
        o_ref[j] = (h * gl_ref[j].astype(f32)).astype(bf16)
    h_ref[...] = h
    for j in range(CONV_W - 1):
        cv_ref[j] = xx[t + j]


def _lru_sample(xl, gl, cs, h0, cw, cb, wa, ba, wx, bx, lam, bsz, t):
    full = lambda a: pl.BlockSpec(a.shape, lambda i: (0,) * a.ndim)
    args = (xl, gl, cs, h0, cw, cb.reshape(1, -1), wa, ba.reshape(1, -1), wx, bx.reshape(1, -1), lam.reshape(1, -1))
    return pl.pallas_call(
        functools.partial(_lru_sample_body, t=t),
        grid=(1,),
        in_specs=[full(a) for a in args],
        out_specs=[pl.BlockSpec((t, bsz, LRU_W), lambda i: (0, 0, 0)), pl.BlockSpec((bsz, LRU_W), lambda i: (0, 0)),
                   pl.BlockSpec((CONV_W - 1, bsz, LRU_W), lambda i: (0, 0, 0))],
        out_shape=[jax.ShapeDtypeStruct((t, bsz, LRU_W), bf16), jax.ShapeDtypeStruct((bsz, LRU_W), f32),
                   jax.ShapeDtypeStruct((CONV_W - 1, bsz, LRU_W), f32)],
        compiler_params=_cparams(("arbitrary",)),
        name="lru_sample",
    )(*args)


def _rope_tables(pos):
    inv = ROPE_BASE ** (-jnp.arange(0, RET_DK, 2, dtype=f32) / RET_DK)
    ang = pos.astype(f32)[:, None] * inv[None, :]
    cos, sin = jnp.cos(ang), jnp.sin(ang)
    return jnp.concatenate([cos, cos], axis=1), jnp.concatenate([-sin, sin], axis=1)


def _ffn_cast_jobs(raw, which, l):
    return [(f"{which}_wg", _CastJob(raw[f"{which}_w_gate"], l, scale=0.5)),
            (f"{which}_wu", _CastJob(raw[f"{which}_w_up"], l)),
            (f"{which}_wd", _CastJob(raw[f"{which}_w_down"], l))]


def _mixer_cast_jobs(raw, l):
    jobs = [("w_ret", _CastJob(raw["w_in"], l, 0, RET_COLS, raw["ret_scale"])),
            ("w_hg", _CastJob(raw["w_in"], l, RET_COLS, HG_COLS, raw["hg_scale"])),
            ("w_lru", _CastJob(raw["w_in"], l, RET_COLS + HG_COLS, LRU_COLS, raw["lru_scale"]))]
    jobs += [(name, _CastJob(raw[name], l)) for name in ("w_ret_o", "w_hgrn_o", "w_lru_o", "w_mix_out")]
    return jobs + _ffn_cast_jobs(raw, "ffn2", l)


def _trunk(x, pos, w, big, raw, states, bsz, t):
    cos, sin = _rope_tables(pos)
    tile_rows = min(TOKEN_TILE, bsz * t)
    if t < tile_rows:
        cos = jnp.tile(cos, (tile_rows // t, 1))
        sin = jnp.tile(sin, (tile_rows // t, 1))
    news = ([], [], [], [])
    ret_stack = hg_stack = None
    depth = len(big)

    def ffn(x, norm, which, l, final_g=None, jobs=()):
        y = _ffn(x, norm, big[l][f"{which}_wg"], big[l][f"{which}_wu"], big[l][f"{which}_wd"], final_g,
                 [j for _, j in jobs])
        if not jobs:
            return y
        y, cast = y
        return y, {name: arr for (name, _), arr in zip(jobs, cast)}

    for l in range(depth):
        if raw is not None:
            x, cast = ffn(x, w["ffn1_norm"][l], "ffn1", l, jobs=_mixer_cast_jobs(raw, l))
            big[l].update(cast)
        else:
            x = ffn(x, w["ffn1_norm"][l], "ffn1", l)
        q_r, k_r, v_r, g_r, xn = _proj_ret(x, w["mix_norm"][l], big[l]["w_ret"], cos, sin)
        q_h, lf, i_h, g_h = _proj_hgrn(xn, big[l]["w_hg"], w["hgrn_lb_logits"], l)
        lru_w = (w["conv_w"][l], w["conv_b"][l], w["lru_w_a"][l], w["lru_b_a"][l], w["lru_w_x"][l],
                 w["lru_b_x"][l], w["lru_lambda"][l])
        if states is None:
            o_r, ret_new = _ret_prompt(q_r, k_r, v_r, g_r, bsz, t)
            o_h, hg_new = _hgrn_prompt(q_h, lf, i_h, g_h, w["hgrn_norm"][l], bsz, t)
            o_l, gs, h_new, cv_new = _proj_lru_scan(xn, big[l]["w_lru"], w["merge_bias"][l], *lru_w, bsz, t)
            news[0].append(ret_new)
            news[1].append(hg_new)
        else:
            st_ret, st_hg, st_lru, st_conv = states
            o_r, ret_stack = _ret_sample(q_r, k_r, v_r, g_r, st_ret, ret_stack, l, bsz, t)
            o_h, hg_stack = _hgrn_sample(q_h, lf, i_h, g_h, w["hgrn_norm"][l], st_hg, hg_stack, l, bsz, t)
            x_l, g_l, gs = _proj_lru(xn, big[l]["w_lru"], w["merge_bias"][l])
            tm = lambda a: a.reshape(bsz, t, LRU_W).transpose(1, 0, 2)
            o_lt, h_new, cv_t = _lru_sample(tm(x_l), tm(g_l), st_conv[l].transpose(1, 0, 2), st_lru[l], *lru_w, bsz, t)
            o_l = o_lt.transpose(1, 0, 2).reshape(bsz * t, LRU_W)
            cv_new = cv_t.transpose(1, 0, 2)
        x = _mix_out(x, o_r, o_h, o_l, gs, big[l]["w_ret_o"], big[l]["w_hgrn_o"], big[l]["w_lru_o"],
                     big[l]["w_mix_out"])
        last = l == depth - 1
        final_g = w["final_norm"] if last else None
        if raw is not None and not last:
            x, cast = ffn(x, w["ffn2_norm"][l], "ffn2", l, final_g, _ffn_cast_jobs(raw, "ffn1", l + 1))
            big[l + 1].update(cast)
        else:
            x = ffn(x, w["ffn2_norm"][l], "ffn2", l, final_g)
        news[2].append(h_new)
        news[3].append(cv_new)
    if states is None:
        return x, [jnp.stack(a, axis=0) for a in news]
    return x, [ret_stack, hg_stack, jnp.stack(news[2], axis=0), jnp.stack(news[3], axis=0)]


def kernel(x_prompt, x_sample, state_ret, state_hgrn, state_lru, state_conv, ffn1_norm, ffn1_w_gate, ffn1_w_up, ffn1_w_down, mix_norm, w_in, merge_bias, w_ret_o, hgrn_lb_logits, hgrn_norm, w_hgrn_o, conv_w, conv_b, lru_w_a, lru_b_a, lru_w_x, lru_b_x, lru_lambda, w_lru_o, w_mix_out, ffn2_norm, ffn2_w_gate, ffn2_w_up, ffn2_w_down, final_norm):
    c16 = lambda a: a.astype(bf16)
    def halves(*widths_and_flags):
        return jnp.concatenate([jnp.full((wd,), 0.5 if half else 1.0, f32) for wd, half in widths_and_flags])

    nqk, nv, hw = RET_HEADS * RET_DK, RET_HEADS * RET_DV, HG_HEADS * HG_DK
    ret_scale = halves((2 * nqk + nv, False), (nv, True))
    hg_scale = halves((hw, True), (2 * hw, False), (hw, True))
    lru_scale = halves((LRU_W, False), (LRU_W + N_BRANCH * D_MODEL, True))
    w = dict(
        ffn1_norm=ffn1_norm, mix_norm=mix_norm, merge_bias=0.5 * merge_bias, hgrn_lb_logits=hgrn_lb_logits,
        hgrn_norm=hgrn_norm, conv_w=conv_w, conv_b=conv_b, lru_w_a=c16(0.5 * lru_w_a), lru_b_a=0.5 * lru_b_a,
        lru_w_x=c16(0.5 * lru_w_x), lru_b_x=0.5 * lru_b_x, lru_lambda=lru_lambda, ffn2_norm=ffn2_norm,
        final_norm=final_norm,
    )
    raw = dict(ffn1_w_gate=ffn1_w_gate, ffn1_w_up=ffn1_w_up, ffn1_w_down=ffn1_w_down, w_in=w_in,
               w_ret_o=w_ret_o, w_hgrn_o=w_hgrn_o, w_lru_o=w_lru_o, w_mix_out=w_mix_out,
               ffn2_w_gate=ffn2_w_gate, ffn2_w_up=ffn2_w_up, ffn2_w_down=ffn2_w_down,
               ret_scale=ret_scale, hg_scale=hg_scale, lru_scale=lru_scale)
    big = [{} for _ in range(w_in.shape[0])]
    big[0].update(ffn1_wg=c16(0.5 * ffn1_w_gate[0]), ffn1_wu=c16(ffn1_w_up[0]), ffn1_wd=c16(ffn1_w_down[0]))
    bp, tp, _ = x_prompt.shape
    bs, ts, _ = x_sample.shape
    y_p, st_p = _trunk(x_prompt.reshape(bp * tp, D_MODEL), jnp.arange(tp), w, big, raw, None, bp, tp)
    y_s, st_s = _trunk(x_sample.reshape(bs * ts, D_MODEL), PAST_LEN + jnp.arange(ts), w, big, None,
                       (state_ret, state_hgrn, state_lru, state_conv), bs, ts)
    return (y_p.reshape(bp, tp, D_MODEL), y_s.reshape(bs, ts, D_MODEL),
            st_p[0], st_s[0], st_p[1], st_s[1], st_p[2], st_s[2], st_p[3], st_s[3])
```

```python
import functools

import jax
import jax.numpy as jnp
import numpy as np
from jax import lax
from jax.experimental import pallas as pl
from jax.experimental.pallas import tpu as pltpu

f32 = jnp.float32
bf16 = jnp.bfloat16

D_MODEL = 1024
FFN_HIDDEN = 2816
RET_HEADS, RET_DK, RET_DV = 4, 128, 256
HG_HEADS, HG_DK, HG_DV = 8, 128, 128
LRU_W, LRU_BLOCKS, LRU_BD = 1024, 8, 128
CONV_W = 4
LRU_C = 8.0
N_BRANCH = 3
ROPE_BASE = 10000.0
PAST_LEN = 16384
LOG2_E = 1.4426950408889634
EPS = 1e-6
RET_COLS = 2 * RET_HEADS * RET_DK + 2 * RET_HEADS * RET_DV
HG_COLS = 4 * HG_HEADS * HG_DK
LRU_COLS = 2 * LRU_W + N_BRANCH * D_MODEL

V7X_VMEM_BYTES = 64 * 1024 * 1024
VMEM_LIMIT = 56 * 1024 * 1024
SUBLANES = 8

BF16_ROWS = 16
TOKEN_TILE = 1024
MIX_TILE = 512
FFN_SIDE_TILE = 512
LRU_TILE = 1024
FFN_CHUNK = 256
RET_CHUNK = 256
HG_CHUNK = 128
HG_STEP_CHUNKS = 4
RET_STEP_CHUNKS = 4
HG_LEVELS = ((128, 64), (64, 32), (32, 16), (16, 8), (8, 4), (4, 2))
SAMPLE_BB = 16


def _cparams(sem):
    return pltpu.CompilerParams(dimension_semantics=sem, vmem_limit_bytes=VMEM_LIMIT)


def _resident(shape):
    nd = len(shape)
    return pl.BlockSpec(shape, lambda *_: (0,) * nd, pipeline_mode=pl.Buffered(1))


def _rmsnorm(x, g):
    return x * lax.rsqrt(jnp.mean(x * x, axis=-1, keepdims=True) + EPS) * g


def _dot(a, b):
    return jnp.dot(a, b, preferred_element_type=f32)


def _dot_tb(a, b):
    return lax.dot_general(a, b, (((1,), (1,)), ((), ())), preferred_element_type=f32)


def _dot_ta(a, b):
    return lax.dot_general(a, b, (((0,), (0,)), ((), ())), preferred_element_type=f32)


def _silu_of_half(hx):
    return hx + hx * jnp.tanh(hx)


def _gelu_of_half(hx):
    c = float(np.float32(np.sqrt(2.0 / np.pi)))
    return hx + hx * jnp.tanh(hx * (2.0 * c + (8.0 * c * 0.044715) * (hx * hx)))


def _softplus(x):
    return jnp.maximum(x, 0.0) + jnp.log1p(jnp.exp(-jnp.abs(x)))


def _block_diag(x, nb, rows_per):
    r, w = x.shape
    rb = lax.broadcasted_iota(jnp.int32, (r, nb * w), 0) // rows_per
    cb = lax.broadcasted_iota(jnp.int32, (r, nb * w), 1) // w
    return jnp.where(rb == cb, jnp.tile(x, (1, nb)), jnp.zeros((), x.dtype))


class _CastJob:
    def __init__(self, w, layer, col_off=0, col_w=None, scale=None):
        self.w, self.layer, self.col_off, self.scale = w, layer, col_off, scale
        self.rows = w.shape[1]
        self.col_w = w.shape[2] - col_off if col_w is None else col_w
        self.win = self.col_w if col_off == 0 else int(np.gcd(col_off, self.col_w))
        self.vector_scale = scale is not None and not isinstance(scale, float)

    def specs(self, steps):
        nblk = steps
        while self.rows % nblk or (self.rows // nblk) % BF16_ROWS:
            nblk //= 2
        rep, rb, layer, win, first = steps // nblk, self.rows // nblk, self.layer, self.win, self.col_off // self.win
        ins = [pl.BlockSpec((None, rb, win), lambda i, j=j: (layer, i // rep, first + j))
               for j in range(self.col_w // win)]
        args = [self.w] * len(ins)
        if self.vector_scale:
            ins.append(_resident((1, self.col_w)))
            args.append(self.scale.reshape(1, -1))
        out = pl.BlockSpec((rb, self.col_w), lambda i: (i // rep, 0))
        return args, ins, out, jax.ShapeDtypeStruct((self.rows, self.col_w), bf16)

    def run(self, in_refs, out_ref):
        nwin = self.col_w // self.win
        for j in range(nwin):
            cols = slice(j * self.win, (j + 1) * self.win)
            v = in_refs[j][...]
            if self.vector_scale:
                v = v * in_refs[nwin][:, cols]
            elif self.scale is not None:
                v = v * self.scale
            out_ref[:, cols] = v.astype(bf16)


def _ffn_body(x_ref, g_ref, wg_ref, wu_ref, wd_ref, fg_ref, *rest, final, jobs, job_nin):
    n_side_in = sum(job_nin)
    o_ref = rest[n_side_in]
    x = x_ref[...]
    xn = _rmsnorm(x, g_ref[...]).astype(bf16)
    acc = jnp.zeros(x.shape, f32)
    for c in range(FFN_HIDDEN // FFN_CHUNK):
        sl = slice(c * FFN_CHUNK, (c + 1) * FFN_CHUNK)
        g = _dot(xn, wg_ref[:, sl])
        u = _dot(xn, wu_ref[:, sl])
        h = (_silu_of_half(g) * u).astype(bf16)
        acc = acc + _dot(h, wd_ref[sl, :])
    y = x + 0.5 * acc
    if final:
        y = _rmsnorm(y, fg_ref[...])
    o_ref[...] = y
    start = 0
    for k, (job, nin) in enumerate(zip(jobs, job_nin)):
        job.run(rest[start:start + nin], rest[n_side_in + 1 + k])
        start += nin


def _ffn(x, g, wg, wu, wd, final_g=None, jobs=()):
    n = x.shape[0]
    tm = min(FFN_SIDE_TILE if jobs else TOKEN_TILE, n)
    steps = n // tm
    final = final_g is not None
    fg = final_g if final else g
    row = pl.BlockSpec((tm, D_MODEL), lambda i: (i, 0))
    args = [x, g.reshape(1, -1), wg, wu, wd, fg.reshape(1, -1)]
    in_specs = [row, _resident((1, D_MODEL)), _resident(wg.shape), _resident(wu.shape),
                _resident(wd.shape), _resident((1, D_MODEL))]
    out_specs, out_shape, job_nin = [row], [jax.ShapeDtypeStruct((n, D_MODEL), f32)], []
    for job in jobs:
        a, i_s, o_s, o_sh = job.specs(steps)
        args += a
        in_specs += i_s
        out_specs.append(o_s)
        out_shape.append(o_sh)
        job_nin.append(len(a))
    outs = pl.pallas_call(
        functools.partial(_ffn_body, final=final, jobs=tuple(jobs), job_nin=tuple(job_nin)),
        grid=(steps,),
        in_specs=in_specs,
        out_specs=out_specs,
        out_shape=out_shape,
        compiler_params=_cparams(("arbitrary",) if jobs else ("parallel",)),
        name="ffn",
    )(*args)
    return (outs[0], list(outs[1:])) if jobs else outs[0]


def _proj_ret_body(x_ref, g_ref, w_ref, cos_ref, sin_ref, q_ref, k_ref, v_ref, gr_ref, xn_ref):
    xn = _rmsnorm(x_ref[...], g_ref[...]).astype(bf16)
    xn_ref[...] = xn
    cos = cos_ref[...]
    sin = sin_ref[...]
    nqk = RET_HEADS * RET_DK

    def rot(z):
        return z * cos + pltpu.roll(z, RET_DK // 2, 1) * sin

    for c in range(nqk // 256):
        zq = _dot(xn, w_ref[:, c * 256:(c + 1) * 256])
        zk = _dot(xn, w_ref[:, nqk + c * 256:nqk + (c + 1) * 256])
        for j in range(2):
            sl = slice(c * 256 + j * 128, c * 256 + (j + 1) * 128)
            q_ref[:, sl] = rot(zq[:, j * 128:(j + 1) * 128]).astype(bf16)
            k_ref[:, sl] = (rot(zk[:, j * 128:(j + 1) * 128]) * (RET_DK ** -0.5)).astype(bf16)
    nv = RET_HEADS * RET_DV
    for c in range(nv // 256):
        sl = slice(c * 256, (c + 1) * 256)
        v_ref[:, sl] = _dot(xn, w_ref[:, 2 * nqk + c * 256:2 * nqk + (c + 1) * 256]).astype(bf16)
        zg = _dot(xn, w_ref[:, 2 * nqk + nv + c * 256:2 * nqk + nv + (c + 1) * 256])
        gr_ref[:, sl] = _silu_of_half(zg).astype(bf16)


def _proj_ret(x, g, w, cos, sin):
    n = x.shape[0]
    tm = min(TOKEN_TILE, n)
    row = lambda w_: pl.BlockSpec((tm, w_), lambda i: (i, 0))
    nqk, nv = RET_HEADS * RET_DK, RET_HEADS * RET_DV
    assert cos.shape[0] % tm == 0
    period = cos.shape[0] // tm
    rope = pl.BlockSpec((tm, RET_DK), lambda i: (i % period, 0))
    return pl.pallas_call(
        _proj_ret_body,
        grid=(n // tm,),
        in_specs=[row(D_MODEL), _resident((1, D_MODEL)), _resident(w.shape), rope, rope],
        out_specs=[row(nqk), row(nqk), row(nv), row(nv), row(D_MODEL)],
        out_shape=[jax.ShapeDtypeStruct((n, nqk), bf16), jax.ShapeDtypeStruct((n, nqk), bf16),
                   jax.ShapeDtypeStruct((n, nv), bf16), jax.ShapeDtypeStruct((n, nv), bf16),
                   jax.ShapeDtypeStruct((n, D_MODEL), bf16)],
        compiler_params=_cparams(("parallel",)),
        name="proj_ret",
    )(x, g.reshape(1, -1), w, cos, sin)


def _proj_hgrn_body(xn_ref, w_ref, lbl_ref, q_ref, lf_ref, i_ref, gh_ref, *, layer):
    xn = xn_ref[...]
    hw = HG_HEADS * HG_DK
    if layer > 0:
        lg = lbl_ref[...]
        e = jnp.exp(lg - jnp.max(lg, axis=0, keepdims=True))
        sm = e / jnp.sum(e, axis=0, keepdims=True)
        lb = jnp.sum(sm[1:layer + 1], axis=0, keepdims=True)
    for c in range(hw // 256):
        sl = slice(c * 256, (c + 1) * 256)
        q_ref[:, sl] = _silu_of_half(_dot(xn, w_ref[:, sl])).astype(bf16)
        zf = _dot(xn, w_ref[:, hw + c * 256:hw + (c + 1) * 256])
        t = jnp.exp2(jnp.abs(zf) * -LOG2_E)
        num = jnp.minimum(zf, 0.0)
        if layer > 0:
            lbc = lb[:, sl]
            num = jnp.maximum(jnp.log(jnp.where(zf < 0.0, lbc + t, 1.0 + lbc * t)), num)
        lf_ref[:, sl] = num - jnp.log(1.0 + t)
        i_ref[:, sl] = _dot(xn, w_ref[:, 2 * hw + c * 256:2 * hw + (c + 1) * 256]).astype(bf16)
        gh_ref[:, sl] = _silu_of_half(_dot(xn, w_ref[:, 3 * hw + c * 256:3 * hw + (c + 1) * 256])).astype(bf16)


def _proj_hgrn(xn, w, lb_logits, layer):
    n = xn.shape[0]
    tm = min(TOKEN_TILE, n)
    hw = HG_HEADS * HG_DK
    row = lambda w_: pl.BlockSpec((tm, w_), lambda i: (i, 0))
    return pl.pallas_call(
        functools.partial(_proj_hgrn_body, layer=layer),
        grid=(n // tm,),
        in_specs=[row(D_MODEL), _resident(w.shape), _resident(lb_logits.shape)],
        out_specs=[row(hw), row(hw), row(hw), row(hw)],
        out_shape=[jax.ShapeDtypeStruct((n, hw), bf16), jax.ShapeDtypeStruct((n, hw), f32),
                   jax.ShapeDtypeStruct((n, hw), bf16), jax.ShapeDtypeStruct((n, hw), bf16)],
        compiler_params=_cparams(("parallel",)),
        name="proj_hgrn",
    )(xn, w, lb_logits)


def _proj_lru_body(xn_ref, w_ref, xl_ref, gl_ref):
    xn = xn_ref[...]
    for c in range(LRU_W // 256):
        sl = slice(c * 256, (c + 1) * 256)
        xl_ref[:, sl] = _dot(xn, w_ref[:, sl])
        zg = _dot(xn, w_ref[:, LRU_W + c * 256:LRU_W + (c + 1) * 256])
        gl_ref[:, sl] = _gelu_of_half(zg).astype(bf16)


def _proj_lru(xn, w):
    n = xn.shape[0]
    tm = min(TOKEN_TILE, n)
    row = lambda w_: pl.BlockSpec((tm, w_), lambda i: (i, 0))
    return pl.pallas_call(
        _proj_lru_body,
        grid=(n // tm,),
        in_specs=[row(D_MODEL), _resident(w.shape)],
        out_specs=[row(LRU_W), row(LRU_W)],
        out_shape=[jax.ShapeDtypeStruct((n, LRU_W), f32), jax.ShapeDtypeStruct((n, LRU_W), bf16)],
        compiler_params=_cparams(("parallel",)),
        name="proj_lru",
    )(xn, w)


def _mix_out_body(x_ref, xn_ref, or_ref, oh_ref, ol_ref, wg_ref, mb_ref, wr_ref, wh_ref, wl_ref, wo_ref, o_ref):
    d = D_MODEL
    xn = xn_ref[...]
    merged = None
    for j, (b_ref, w_ref) in enumerate(((or_ref, wr_ref), (oh_ref, wh_ref), (ol_ref, wl_ref))):
        parts = []
        for c in range(d // 256):
            lo = j * d + c * 256
            half_z = _dot(xn, wg_ref[:, 2 * LRU_W + lo:2 * LRU_W + lo + 256]) + mb_ref[:, lo:lo + 256]
            parts.append(0.5 + 0.5 * jnp.tanh(half_z))
        term = jnp.concatenate(parts, axis=1) * _dot(b_ref[...], w_ref[...])
        merged = term if merged is None else merged + term
    o_ref[...] = x_ref[...] + _dot(merged.astype(bf16), wo_ref[...])


def _mix_out(x, xn, o_r, o_h, o_l, w_lru, merge_bias, wr, wh, wl, wo):
    n = x.shape[0]
    tm = min(MIX_TILE, n)
    row = lambda w_: pl.BlockSpec((tm, w_), lambda i: (i, 0))
    return pl.pallas_call(
        _mix_out_body,
        grid=(n // tm,),
        in_specs=[row(D_MODEL), row(D_MODEL), row(D_MODEL), row(D_MODEL), row(D_MODEL),
                  _resident(w_lru.shape), _resident((1, N_BRANCH * D_MODEL)),
                  _resident(wr.shape), _resident(wh.shape), _resident(wl.shape), _resident(wo.shape)],
        out_specs=row(D_MODEL),
        out_shape=jax.ShapeDtypeStruct((n, D_MODEL), f32),
        compiler_params=_cparams(("parallel",)),
        name="mix_out",
    )(x, xn, o_r, o_h, o_l, w_lru, merge_bias.reshape(1, -1), wr, wh, wl, wo)


def _ret_log_gamma():
    return jnp.log1p(-jnp.exp2(-5.0 - jnp.arange(RET_HEADS, dtype=f32)))


def _ret_tables(c):
    log_g = _ret_log_gamma()
    idx = jnp.arange(c, dtype=f32)
    diff = idx[:, None] - idx[None, :]
    causal = diff >= 0
    d_intra = jnp.where(causal, jnp.exp(log_g[:, None, None] * jnp.where(causal, diff, 0.0)), 0.0)
    q_dec = jnp.exp(log_g[:, None] * (idx + 1.0))[:, :, None]
    k_dec = jnp.exp(log_g[:, None] * (c - 1.0 - idx))[:, :, None]
    s_dec = jnp.exp(log_g * c)
    return d_intra, q_dec, k_dec, s_dec


def _head_rms_gate(o, gate):
    return o * lax.rsqrt(jnp.mean(o * o, axis=-1, keepdims=True) + EPS) * gate


def _ret_prompt_body(q_ref, k_ref, v_ref, g_ref, dm_ref, qd_ref, kd_ref, sd_ref, o_ref, st_ref, s_scr):
    c = pl.program_id(1)

    @pl.when(c == 0)
    def _():
        s_scr[...] = jnp.zeros(s_scr.shape, f32)

    heads = range(RET_HEADS)
    ks = [slice(h * RET_DK, (h + 1) * RET_DK) for h in heads]
    vs = [slice(h * RET_DV, (h + 1) * RET_DV) for h in heads]
    for r0 in range(0, q_ref.shape[0], RET_CHUNK):
        rows = slice(r0, r0 + RET_CHUNK)
        scores = [(_dot_tb(q_ref[rows, ks[h]], k_ref[rows, ks[h]]) * dm_ref[h]).astype(bf16) for h in heads]
        inter = [qd_ref[h] * _dot(q_ref[rows, ks[h]], s_scr[h].astype(bf16)) for h in heads]
        outs = [_dot(scores[h], v_ref[rows, vs[h]]) + inter[h] for h in heads]
        for h in heads:
            kd = (k_ref[rows, ks[h]].astype(f32) * kd_ref[h]).astype(bf16)
            s_scr[h] = s_scr[h] * sd_ref[h] + _dot_ta(kd, v_ref[rows, vs[h]])
        for h in heads:
            o_ref[rows, vs[h]] = _head_rms_gate(outs[h], g_ref[rows, vs[h]].astype(f32)).astype(bf16)

    @pl.when(c == pl.num_programs(1) - 1)
    def _():
        st_ref[0] = s_scr[...]


def _ret_prompt(q, k, v, g, bsz, t):
    c = RET_CHUNK
    rows = RET_STEP_CHUNKS * c
    nc = t // rows
    nqk, nv = RET_HEADS * RET_DK, RET_HEADS * RET_DV
    d_intra, q_dec, k_dec, s_dec = _ret_tables(c)
    sd = jnp.broadcast_to(s_dec[:, None, None], (RET_HEADS, 1, RET_DV))
    row = lambda w_: pl.BlockSpec((rows, w_), lambda b, i: (b * nc + i, 0))
    return pl.pallas_call(
        _ret_prompt_body,
        grid=(bsz, nc),
        in_specs=[row(nqk), row(nqk), row(nv), row(nv), _resident(d_intra.shape), _resident(q_dec.shape),
                  _resident(k_dec.shape), _resident(sd.shape)],
        out_specs=[row(nv), pl.BlockSpec((1, RET_HEADS, RET_DK, RET_DV), lambda b, i: (b, 0, 0, 0))],
        out_shape=[jax.ShapeDtypeStruct((bsz * t, nv), bf16),
                   jax.ShapeDtypeStruct((bsz, RET_HEADS, RET_DK, RET_DV), f32)],
        scratch_shapes=[pltpu.VMEM((RET_HEADS, RET_DK, RET_DV), f32)],
        compiler_params=_cparams(("parallel", "arbitrary")),
        name="ret_prompt",
    )(q, k, v, g, d_intra, q_dec, k_dec, sd)


def _hg_level_masks(c):
    t = np.arange(c)[:, None]
    s = np.arange(c)[None, :]
    ms = [((t // blk == s // blk) & (t % blk >= half) & (s % blk < half)) for blk, half in HG_LEVELS]
    return jnp.asarray(np.stack(ms).astype(np.float32))


def _boundary_rows(b, blk, half):
    c, w = b.shape
    if half >= SUBLANES:
        return jnp.concatenate(
            [jnp.broadcast_to(b[j * blk + half - 1:j * blk + half, :], (blk, w)) for j in range(c // blk)], axis=0)
    b3 = b.reshape(c // SUBLANES, SUBLANES, w)
    row8 = lax.broadcasted_iota(jnp.int32, (1, SUBLANES, 1), 1)
    out = b3[:, half - 1:half, :]
    for j in range(1, SUBLANES // blk):
        out = jnp.where(row8 >= j * blk, b3[:, j * blk + half - 1:j * blk + half, :], out)
    return jnp.broadcast_to(out, b3.shape).reshape(c, w)


def _upper_lower(up, low, blk, half):
    c = up.shape[0]
    if half >= SUBLANES:
        parts = []
        for j in range(c // blk):
            parts += [low[j * blk:j * blk + half, :], up[j * blk + half:(j + 1) * blk, :]]
        return jnp.concatenate(parts, axis=0)
    upper = lax.broadcasted_iota(jnp.int32, (c, 1), 0) % blk >= half
    return jnp.where(upper, up, low)


def _hgrn_chunk(q_ref, lf_ref, i_ref, g_ref, nw_ref, tri_ref, mask_ref, o_ref, s_scr, rows):
    c = HG_CHUNK
    lf_all = lf_ref[rows, :]
    hi = lf_all.astype(bf16)
    lo = (lf_all - hi.astype(f32)).astype(bf16)
    tri = tri_ref[...]
    b2_all = (_dot(tri, hi) + _dot(tri, lo)) * LOG2_E
    odd = lax.broadcasted_iota(jnp.int32, (c, 1), 0) % 2 == 1

    f_all = jnp.exp(lf_all)
    kk_all = 1.0 - f_all
    q_all = q_ref[rows, :].astype(f32)
    a_mats = [None] * HG_HEADS
    for lev, (blk, half) in enumerate(HG_LEVELS):
        keep = mask_ref[lev] > 0.5
        for h in range(HG_HEADS):
            sl = slice(h * HG_DK, (h + 1) * HG_DK)
            b2 = b2_all[:, sl]
            bm = _boundary_rows(b2, blk, half)
            e = jnp.exp2(_upper_lower(b2 - bm, bm - b2, blk, half))
            m = (_upper_lower(q_all[:, sl], kk_all[:, sl], blk, half) * e).astype(bf16)
            p = _dot_tb(m, m)
            a_mats[h] = jnp.where(keep, p, 0.0) if lev == 0 else jnp.where(keep, p, a_mats[h])
    a_mats = [a.astype(bf16) for a in a_mats]

    for h in range(HG_HEADS):
        sl = slice(h * HG_DK, (h + 1) * HG_DK)
        b2 = b2_all[:, sl]
        q = q_all[:, sl]
        f = f_all[:, sl]
        kk = kk_all[:, sl]
        v16 = i_ref[rows, sl]
        o = _dot(a_mats[h], v16)
        vf = v16.astype(f32)
        a0 = jnp.sum(q * kk, axis=-1, keepdims=True)
        a1 = jnp.where(odd, jnp.sum(q * f * pltpu.roll(kk, 1, 0), axis=-1, keepdims=True), 0.0)
        o = o + a0 * vf + a1 * pltpu.roll(vf, 1, 0)
        st = s_scr[h]
        o = o + _dot_tb((q * jnp.exp2(b2)).astype(bf16), st.astype(bf16))
        bl2 = b2[c - 1:c, :]
        kdec = (kk * jnp.exp2(bl2 - b2)).astype(bf16)
        s_scr[h] = st * jnp.exp2(bl2) + _dot_ta(v16, kdec)
        o_ref[rows, sl] = _head_rms_gate(o, nw_ref[:, sl] * g_ref[rows, sl].astype(f32)).astype(bf16)


def _hgrn_prompt_body(q_ref, lf_ref, i_ref, g_ref, nw_ref, tri_ref, mask_ref, o_ref, st_ref, s_scr):
    ci = pl.program_id(1)
    c = HG_CHUNK

    @pl.when(ci == 0)
    def _():
        s_scr[...] = jnp.zeros(s_scr.shape, f32)

    for r0 in range(0, q_ref.shape[0], c):
        _hgrn_chunk(q_ref, lf_ref, i_ref, g_ref, nw_ref, tri_ref, mask_ref, o_ref, s_scr, slice(r0, r0 + c))

    @pl.when(ci == pl.num_programs(1) - 1)
    def _():
        for h in range(HG_HEADS):
            st_ref[0, h] = s_scr[h].T


def _hgrn_prompt(q, lf, i, g, norm_w, bsz, t):
    c = HG_CHUNK
    rows = HG_STEP_CHUNKS * c
    nc = t // rows
    hw = HG_HEADS * HG_DK
    tri = jnp.asarray(np.tril(np.ones((c, c), np.float32))).astype(bf16)
    masks = _hg_level_masks(c)
    row = lambda: pl.BlockSpec((rows, hw), lambda b, j: (b * nc + j, 0))
    return pl.pallas_call(
        _hgrn_prompt_body,
        grid=(bsz, nc),
        in_specs=[row(), row(), row(), row(), _resident((1, hw)), _resident(tri.shape), _resident(masks.shape)],
        out_specs=[row(), pl.BlockSpec((1, HG_HEADS, HG_DK, HG_DV), lambda b, j: (b, 0, 0, 0))],
        out_shape=[jax.ShapeDtypeStruct((bsz * t, hw), bf16),
                   jax.ShapeDtypeStruct((bsz, HG_HEADS, HG_DK, HG_DV), f32)],
        scratch_shapes=[pltpu.VMEM((HG_HEADS, HG_DV, HG_DK), f32)],
        compiler_params=_cparams(("parallel", "arbitrary")),
        name="hgrn_prompt",
    )(q, lf, i, g, norm_w.reshape(1, -1), tri, masks)


def _lru_gate_math(xc, half_ya, half_yx, half_c_sp):
    neg_log_a = half_c_sp + half_c_sp * jnp.tanh(half_ya)
    a = jnp.exp(-neg_log_a)
    y = jnp.tanh(neg_log_a) * (1.0 + a * a)
    scale = jnp.where(y > 0.0, y * lax.rsqrt(y), 0.0)
    hx = 0.5 * xc
    return a, scale * (hx + hx * jnp.tanh(half_yx))


def _lru_gate_block(xc, wa, ba, wx, bx, half_c_sp):
    xb = xc.astype(bf16)
    return _lru_gate_math(xc, _dot(xb, wa) + ba, _dot(xb, wx) + bx, half_c_sp)


def _lru_gates(xc, wa_ref, ba, wx_ref, bx, half_c_sp):
    parts = []
    for n in range(LRU_BLOCKS):
        sl = slice(n * LRU_BD, (n + 1) * LRU_BD)
        parts.append(_lru_gate_block(xc[:, sl], wa_ref[n], ba[:, sl], wx_ref[n], bx[:, sl], half_c_sp[:, sl]))
    return jnp.concatenate([p[0] for p in parts], axis=1), jnp.concatenate([p[1] for p in parts], axis=1)


def _strided_rows(g, rows):
    j_count = rows // SUBLANES
    k, j0 = divmod(SUBLANES * g, j_count)
    return pl.ds(SUBLANES * j0 + k, SUBLANES, stride=SUBLANES)


def _shift_time(xp, tail, d):
    rows = xp.shape[0]
    row8 = lax.broadcasted_iota(jnp.int32, (SUBLANES, 1), 0)
    last = pltpu.roll(xp[rows - SUBLANES * d:rows, :].reshape(d, SUBLANES, xp.shape[1]), 1, 1)
    head = [jnp.where(row8 == 0, tail[SUBLANES - d + j:SUBLANES - d + j + 1, :], last[j]) for j in range(d)]
    return jnp.concatenate(head + [xp[0:rows - SUBLANES * d, :]], axis=0)


def _proj_lru_scan_body(xn_ref, w_ref, cw_ref, cb_ref, wa_ref, ba_ref, wx_ref, bx_ref, lam_ref,
                        o_ref, h_ref, cv_ref, xp_scr, hp_scr, ap_scr, tail_scr, h_scr):
    ti = pl.program_id(1)
    tm = xn_ref.shape[0]
    nj = tm // SUBLANES

    @pl.when(ti == 0)
    def _():
        tail_scr[...] = jnp.zeros(tail_scr.shape, f32)
        h_scr[...] = jnp.zeros(h_scr.shape, f32)

    xn = xn_ref[...]
    for c in range(LRU_W // 256):
        z = _dot(xn, w_ref[:, c * 256:(c + 1) * 256])
        for half in range(2):
            for g in range(nj):
                xp_scr[2 * c + half, _strided_rows(g, tm), :] = (
                    z[SUBLANES * g:SUBLANES * (g + 1), half * LRU_BD:(half + 1) * LRU_BD])
    cw = cw_ref[...]
    cb = cb_ref[...]
    ba = ba_ref[...]
    bx = bx_ref[...]
    half_c_sp = (0.5 * LRU_C) * _softplus(-lam_ref[...])
    n_pair = LRU_W // 256
    for p in range(n_pair):
        blocks = (2 * p, 2 * p + 1)
        sls = [slice(n * LRU_BD, (n + 1) * LRU_BD) for n in blocks]
        xcs, yas, yxs = [], [], []
        for n, sl in zip(blocks, sls):
            xp = xp_scr[n]
            tail = tail_scr[:, sl]
            xc = cb[:, sl] + xp * cw[CONV_W - 1:CONV_W, sl]
            for d in range(1, CONV_W):
                xc = xc + _shift_time(xp, tail, d) * cw[CONV_W - 1 - d:CONV_W - d, sl]
            xb = xc.astype(bf16)
            xcs.append(xc)
            yas.append(_dot(xb, wa_ref[n]) + ba[:, sl])
            yxs.append(_dot(xb, wx_ref[n]) + bx[:, sl])
        hseg = [jnp.zeros((SUBLANES, LRU_BD), f32) for _ in blocks]
        aseg = [jnp.ones((SUBLANES, LRU_BD), f32) for _ in blocks]
        for j in range(nj):
            rows = slice(SUBLANES * j, SUBLANES * (j + 1))
            for i, (n, sl) in enumerate(zip(blocks, sls)):
                a, u = _lru_gate_math(xcs[i][rows, :], yas[i][rows, :], yxs[i][rows, :], half_c_sp[:, sl])
                hseg[i] = a * hseg[i] + u
                aseg[i] = a * aseg[i]
                hp_scr[n, rows, :] = hseg[i]
                ap_scr[n, rows, :] = aseg[i]
        for i, (n, sl) in enumerate(zip(blocks, sls)):
            s = h_scr[0:1, sl]
            starts = []
            for k in range(SUBLANES):
                starts.append(s)
                s = aseg[i][k:k + 1, :] * s + hseg[i][k:k + 1, :]
            h_scr[:, sl] = jnp.broadcast_to(s, (h_scr.shape[0], LRU_BD))
            start8 = jnp.concatenate(starts, axis=0)
            hp_scr[n] = (hp_scr[n].reshape(nj, SUBLANES, LRU_BD)
                         + ap_scr[n].reshape(nj, SUBLANES, LRU_BD) * start8).reshape(tm, LRU_BD)
            tail_scr[:, sl] = xp_scr[n, _strided_rows(nj - 1, tm), :]
        zg = _gelu_of_half(_dot(xn, w_ref[:, LRU_W + p * 256:LRU_W + (p + 1) * 256]))
        for half in range(2):
            n = 2 * p + half
            hs = jnp.concatenate([hp_scr[n, _strided_rows(g, tm), :] for g in range(nj)], axis=0)
            o_ref[:, n * LRU_BD:(n + 1) * LRU_BD] = (hs * zg[:, half * LRU_BD:(half + 1) * LRU_BD]).astype(bf16)

    @pl.when(ti == pl.num_programs(1) - 1)
    def _():
        h_ref[0] = h_scr[0:1, :]
        cv_ref[0] = tail_scr[SUBLANES - (CONV_W - 1):SUBLANES, :]


def _proj_lru_scan(xn, w, cw, cb, wa, ba, wx, bx, lam, bsz, t):
    tm = min(LRU_TILE, t)
    assert tm % (SUBLANES * SUBLANES) == 0
    nt = t // tm
    row = lambda w_: pl.BlockSpec((tm, w_), lambda b, j: (b * nt + j, 0))
    vec = _resident((1, LRU_W))
    o, h, cv = pl.pallas_call(
        _proj_lru_scan_body,
        grid=(bsz, nt),
        in_specs=[row(D_MODEL), _resident(w.shape), _resident((CONV_W, LRU_W)), vec, _resident(wa.shape), vec,
                  _resident(wx.shape), vec, vec],
        out_specs=[row(LRU_W), pl.BlockSpec((1, 1, LRU_W), lambda b, j: (b, 0, 0)),
                   pl.BlockSpec((1, CONV_W - 1, LRU_W), lambda b, j: (b, 0, 0))],
        out_shape=[jax.ShapeDtypeStruct((bsz * t, LRU_W), bf16), jax.ShapeDtypeStruct((bsz, 1, LRU_W), f32),
                   jax.ShapeDtypeStruct((bsz, CONV_W - 1, LRU_W), f32)],
        scratch_shapes=[pltpu.VMEM((LRU_BLOCKS, tm, LRU_BD), f32)] * 3 + [pltpu.VMEM((SUBLANES, LRU_W), f32)] * 2,
        compiler_params=_cparams(("parallel", "arbitrary")),
        name="proj_lru_scan",
    )(xn, w, cw, cb.reshape(1, -1), wa, ba.reshape(1, -1), wx, bx.reshape(1, -1), lam.reshape(1, -1))
    return o, h[:, 0], cv


def _ret_sample_tables(t):
    log_g = _ret_log_gamma()
    r = jnp.arange(SAMPLE_BB * t)
    tt = (r % t).astype(f32)
    same = (r[:, None] // t) == (r[None, :] // t)
    diff = tt[:, None] - tt[None, :]
    ok = same & (diff >= 0)
    dm = jnp.where(ok, jnp.exp(log_g[:, None, None] * jnp.where(ok, diff, 0.0)), 0.0)
    qd = jnp.exp(log_g[:, None] * (tt + 1.0))[:, :, None]
    kd = jnp.exp(log_g[:, None] * (t - 1.0 - tt))[:, :, None]
    sd = jnp.exp(log_g * t)
    return dm, qd, kd, sd


def _stacked_state(layer, depth, bsz, heads_shape):
    shape = jax.ShapeDtypeStruct((depth, bsz) + heads_shape, f32)
    if layer == 0:
        spec = pl.BlockSpec((depth, SAMPLE_BB) + heads_shape, lambda i: (0, i, 0, 0, 0))

        def put(so_ref, b, h, val):
            so_ref[0, b, h] = val

        def finish(so_ref):
            for l in range(1, depth):
                so_ref[l] = jnp.zeros((SAMPLE_BB,) + heads_shape, f32)
    else:
        spec = pl.BlockSpec((None, SAMPLE_BB) + heads_shape, lambda i: (layer, i, 0, 0, 0))

        def put(so_ref, b, h, val):
            so_ref[b, h] = val

        def finish(so_ref):
            pass
    return shape, spec, put, finish


def _ret_sample_body(q_ref, k_ref, v_ref, g_ref, s_ref, dm_ref, qd_ref, kd_ref, sd_ref, *rest, t, put, finish):
    o_ref, so_ref = rest[-2:]
    bb = SAMPLE_BB
    finish(so_ref)
    for h in range(RET_HEADS):
        ks = slice(h * RET_DK, (h + 1) * RET_DK)
        vs = slice(h * RET_DV, (h + 1) * RET_DV)
        q = q_ref[:, ks]
        k = k_ref[:, ks]
        v = v_ref[:, vs]
        state = s_ref[:, h]
        scores = (_dot_tb(q, k) * dm_ref[h]).astype(bf16)
        inter = _dot(_block_diag(q, bb, t), state.reshape(bb * RET_DK, RET_DV).astype(bf16))
        o = _dot(scores, v) + qd_ref[h] * inter
        kd = (k.astype(f32) * kd_ref[h]).astype(bf16)
        ds = _dot_ta(kd, _block_diag(v, bb, t))
        for b in range(bb):
            put(so_ref, b, h, state[b] * sd_ref[h] + ds[:, b * RET_DV:(b + 1) * RET_DV])
        o_ref[:, vs] = _head_rms_gate(o, g_ref[:, vs].astype(f32)).astype(bf16)


def _ret_sample(q, k, v, g, state, stack, layer, bsz, t):
    nqk, nv = RET_HEADS * RET_DK, RET_HEADS * RET_DV
    bb = SAMPLE_BB
    heads_shape = (RET_HEADS, RET_DK, RET_DV)
    dm, qd, kd, s_dec = _ret_sample_tables(t)
    sd = jnp.broadcast_to(s_dec[:, None, None], (RET_HEADS, 1, RET_DV))
    row = lambda w_: pl.BlockSpec((bb * t, w_), lambda i: (i, 0))
    st_shape, st_spec, put, finish = _stacked_state(layer, state.shape[0], bsz, heads_shape)
    args = [q, k, v, g, state, dm, qd, kd, sd]
    in_specs = [row(nqk), row(nqk), row(nv), row(nv),
                pl.BlockSpec((None, bb) + heads_shape, lambda i: (layer, i, 0, 0, 0)),
                _resident(dm.shape), _resident(qd.shape), _resident(kd.shape), _resident(sd.shape)]
    aliases = {}
    if stack is not None:
        aliases = {len(args): 1}
        args.append(stack)
        in_specs.append(pl.BlockSpec(memory_space=pl.ANY))
    return pl.pallas_call(
        functools.partial(_ret_sample_body, t=t, put=put, finish=finish),
        grid=(bsz // bb,),
        in_specs=in_specs,
        out_specs=[row(nv), st_spec],
        out_shape=[jax.ShapeDtypeStruct((bsz * t, nv), bf16), st_shape],
        input_output_aliases=aliases,
        compiler_params=_cparams(("parallel",)),
        name="ret_sample",
    )(*args)


def _hgrn_sample_body(q_ref, lf_ref, i_ref, g_ref, nw_ref, s_ref, *rest, t, put, finish):
    assert t >= 2
    o_ref, so_ref = rest[-2:]
    finish(so_ref)
    bb = SAMPLE_BB
    n = bb * t
    tt = lax.broadcasted_iota(jnp.int32, (n, 1), 0) % t
    f = jnp.exp(lf_ref[...])
    kk = 1.0 - f
    q = q_ref[...].astype(f32)
    v = i_ref[...].astype(f32)
    before = [jnp.where(tt >= d, pltpu.roll(f, d, 0), 1.0) for d in range(1, t)]
    after = [jnp.where(tt < t - d, pltpu.roll(f, n - d, 0), 1.0) for d in range(1, t)]
    dec = [None, f]
    for d in range(2, t):
        dec.append(dec[-1] * before[d - 2])
    pre = dec[t - 1] * before[t - 2]
    suf = after[0]
    for d in range(1, t - 1):
        suf = suf * after[d]
    total = pre * suf
    qp = (q * pre).astype(bf16)
    kp = (kk * suf).astype(bf16)
    k_back = [kk] + [pltpu.roll(kk, d, 0) for d in range(1, t)]
    v_back = [v] + [pltpu.roll(v, d, 0) for d in range(1, t)]
    for h in range(HG_HEADS):
        sl = slice(h * HG_DK, (h + 1) * HG_DK)
        o = jnp.zeros((n, HG_DV), f32)
        for d in range(t):
            p = q[:, sl] * k_back[d][:, sl]
            if d > 0:
                p = p * dec[d][:, sl]
            a = jnp.sum(p, axis=-1, keepdims=True)
            if d > 0:
                a = jnp.where(tt >= d, a, 0.0)
            o = o + a * v_back[d][:, sl]
        state = s_ref[:, h]
        o = o + _dot(_block_diag(qp[:, sl], bb, t), state.reshape(bb * HG_DK, HG_DV).astype(bf16))
        ds = _dot_ta(kp[:, sl], _block_diag(i_ref[:, sl], bb, t))
        total_t = total[:, sl].T
        for b in range(bb):
            put(so_ref, b, h, state[b] * total_t[:, b * t:b * t + 1] + ds[:, b * HG_DV:(b + 1) * HG_DV])
        o_ref[:, sl] = _head_rms_gate(o, nw_ref[:, sl] * g_ref[:, sl].astype(f32)).astype(bf16)


def _hgrn_sample(q, lf, i, g, norm_w, state, stack, layer, bsz, t):
    hw = HG_HEADS * HG_DK
    bb = SAMPLE_BB
    heads_shape = (HG_HEADS, HG_DK, HG_DV)
    row = lambda: pl.BlockSpec((bb * t, hw), lambda j: (j, 0))
    st_shape, st_spec, put, finish = _stacked_state(layer, state.shape[0], bsz, heads_shape)
    args = [q, lf, i, g, norm_w.reshape(1, -1), state]
    in_specs = [row(), row(), row(), row(), _resident((1, hw)),
                pl.BlockSpec((None, bb) + heads_shape, lambda j: (layer, j, 0, 0, 0))]
    aliases = {}
    if stack is not None:
        aliases = {len(args): 1}
        args.append(stack)
        in_specs.append(pl.BlockSpec(memory_space=pl.ANY))
    return pl.pallas_call(
        functools.partial(_hgrn_sample_body, t=t, put=put, finish=finish),
        grid=(bsz // bb,),
        in_specs=in_specs,
        out_specs=[row(), st_spec],
        out_shape=[jax.ShapeDtypeStruct((bsz * t, hw), bf16), st_shape],
        input_output_aliases=aliases,
        compiler_params=_cparams(("parallel",)),
        name="hgrn_sample",
    )(*args)


def _lru_sample_body(x_ref, gl_ref, cs_ref, h0_ref, cw_ref, cb_ref, wa_ref, ba_ref, wx_ref, bx_ref, lam_ref,
                     o_ref, h_ref, cv_ref, *, t):
    xx = [cs_ref[j] for j in range(CONV_W - 1)] + [x_ref[j] for j in range(t)]
    cw = cw_ref[...]
    half_c_sp = (0.5 * LRU_C) * _softplus(-lam_ref[...])
    h = h0_ref[...]
    for j in range(t):
        xc = cb_ref[...]
        for i in range(CONV_W):
            xc = xc + xx[j + i] * cw[i:i + 1, :]
        a, u = _lru_gates(xc, wa_ref, ba_ref[...], wx_ref, bx_ref[...], half_c_sp)
        h = a * h + u
        o_ref[j] = (h * gl_ref[j].astype(f32)).astype(bf16)
    h_ref[...] = h
    for j in range(CONV_W - 1):
        cv_ref[j] = xx[t + j]


def _lru_sample(xl, gl, cs, h0, cw, cb, wa, ba, wx, bx, lam, bsz, t):
    full = lambda a: pl.BlockSpec(a.shape, lambda i: (0,) * a.ndim)
    args = (xl, gl, cs, h0, cw, cb.reshape(1, -1), wa, ba.reshape(1, -1), wx, bx.reshape(1, -1), lam.reshape(1, -1))
    return pl.pallas_call(
        functools.partial(_lru_sample_body, t=t),
        grid=(1,),
        in_specs=[full(a) for a in args],
        out_specs=[pl.BlockSpec((t, bsz, LRU_W), lambda i: (0, 0, 0)), pl.BlockSpec((bsz, LRU_W), lambda i: (0, 0)),
                   pl.BlockSpec((CONV_W - 1, bsz, LRU_W), lambda i: (0, 0, 0))],
        out_shape=[jax.ShapeDtypeStruct((t, bsz, LRU_W), bf16), jax.ShapeDtypeStruct((bsz, LRU_W), f32),
                   jax.ShapeDtypeStruct((CONV_W - 1, bsz, LRU_W), f32)],
        compiler_params=_cparams(("arbitrary",)),
        name="lru_sample",
    )(*args)


def _rope_tables(pos):
    inv = ROPE_BASE ** (-jnp.arange(0, RET_DK, 2, dtype=f32) / RET_DK)
    ang = pos.astype(f32)[:, None] * inv[None, :]
    cos, sin = jnp.cos(ang), jnp.sin(ang)
    return jnp.concatenate([cos, cos], axis=1), jnp.concatenate([-sin, sin], axis=1)


def _ffn_cast_jobs(raw, which, l):
    return [(f"{which}_wg", _CastJob(raw[f"{which}_w_gate"], l, scale=0.5)),
            (f"{which}_wu", _CastJob(raw[f"{which}_w_up"], l)),
            (f"{which}_wd", _CastJob(raw[f"{which}_w_down"], l))]


def _mixer_cast_jobs(raw, l):
    jobs = [("w_ret", _CastJob(raw["w_in"], l, 0, RET_COLS, raw["ret_scale"])),
            ("w_hg", _CastJob(raw["w_in"], l, RET_COLS, HG_COLS, raw["hg_scale"])),
            ("w_lru", _CastJob(raw["w_in"], l, RET_COLS + HG_COLS, LRU_COLS, raw["lru_scale"]))]
    jobs += [(name, _CastJob(raw[name], l)) for name in ("w_ret_o", "w_hgrn_o", "w_lru_o", "w_mix_out")]
    return jobs + _ffn_cast_jobs(raw, "ffn2", l)


def _trunk(x, pos, w, big, raw, states, bsz, t):
    cos, sin = _rope_tables(pos)
    tile_rows = min(TOKEN_TILE, bsz * t)
    if t < tile_rows:
        cos = jnp.tile(cos, (tile_rows // t, 1))
        sin = jnp.tile(sin, (tile_rows // t, 1))
    news = ([], [], [], [])
    ret_stack = hg_stack = None
    depth = len(big)

    def ffn(x, norm, which, l, final_g=None, jobs=()):
        y = _ffn(x, norm, big[l][f"{which}_wg"], big[l][f"{which}_wu"], big[l][f"{which}_wd"], final_g,
                 [j for _, j in jobs])
        if not jobs:
            return y
        y, cast = y
        return y, {name: arr for (name, _), arr in zip(jobs, cast)}

    for l in range(depth):
        if raw is not None:
            x, cast = ffn(x, w["ffn1_norm"][l], "ffn1", l, jobs=_mixer_cast_jobs(raw, l))
            big[l].update(cast)
        else:
            x = ffn(x, w["ffn1_norm"][l], "ffn1", l)
        q_r, k_r, v_r, g_r, xn = _proj_ret(x, w["mix_norm"][l], big[l]["w_ret"], cos, sin)
        q_h, lf, i_h, g_h = _proj_hgrn(xn, big[l]["w_hg"], w["hgrn_lb_logits"], l)
        lru_w = (w["conv_w"][l], w["conv_b"][l], w["lru_w_a"][l], w["lru_b_a"][l], w["lru_w_x"][l],
                 w["lru_b_x"][l], w["lru_lambda"][l])
        if states is None:
            o_r, ret_new = _ret_prompt(q_r, k_r, v_r, g_r, bsz, t)
            o_h, hg_new = _hgrn_prompt(q_h, lf, i_h, g_h, w["hgrn_norm"][l], bsz, t)
            o_l, h_new, cv_new = _proj_lru_scan(xn, big[l]["w_lru"], *lru_w, bsz, t)
            news[0].append(ret_new)
            news[1].append(hg_new)
        else:
            st_ret, st_hg, st_lru, st_conv = states
            o_r, ret_stack = _ret_sample(q_r, k_r, v_r, g_r, st_ret, ret_stack, l, bsz, t)
            o_h, hg_stack = _hgrn_sample(q_h, lf, i_h, g_h, w["hgrn_norm"][l], st_hg, hg_stack, l, bsz, t)
            x_l, g_l = _proj_lru(xn, big[l]["w_lru"])
            tm = lambda a: a.reshape(bsz, t, LRU_W).transpose(1, 0, 2)
            o_lt, h_new, cv_t = _lru_sample(tm(x_l), tm(g_l), st_conv[l].transpose(1, 0, 2), st_lru[l], *lru_w, bsz, t)
            o_l = o_lt.transpose(1, 0, 2).reshape(bsz * t, LRU_W)
            cv_new = cv_t.transpose(1, 0, 2)
        x = _mix_out(x, xn, o_r, o_h, o_l, big[l]["w_lru"], w["merge_bias"][l], big[l]["w_ret_o"],
                     big[l]["w_hgrn_o"], big[l]["w_lru_o"], big[l]["w_mix_out"])
        last = l == depth - 1
        final_g = w["final_norm"] if last else None
        if raw is not None and not last:
            x, cast = ffn(x, w["ffn2_norm"][l], "ffn2", l, final_g, _ffn_cast_jobs(raw, "ffn1", l + 1))
            big[l + 1].update(cast)
        else:
            x = ffn(x, w["ffn2_norm"][l], "ffn2", l, final_g)
        news[2].append(h_new)
        news[3].append(cv_new)
    if states is None:
        return x, [jnp.stack(a, axis=0) for a in news]
    return x, [ret_stack, hg_stack, jnp.stack(news[2], axis=0), jnp.stack(news[3], axis=0)]


def kernel(x_prompt, x_sample, state_ret, state_hgrn, state_lru, state_conv, ffn1_norm, ffn1_w_gate, ffn1_w_up, ffn1_w_down, mix_norm, w_in, merge_bias, w_ret_o, hgrn_lb_logits, hgrn_norm, w_hgrn_o, conv_w, conv_b, lru_w_a, lru_b_a, lru_w_x, lru_b_x, lru_lambda, w_lru_o, w_mix_out, ffn2_norm, ffn2_w_gate, ffn2_w_up, ffn2_w_down, final_norm):
    c16 = lambda a: a.astype(bf16)
    def halves(*widths_and_flags):
        return jnp.concatenate([jnp.full((wd,), 0.5 if half else 1.0, f32) for wd, half in widths_and_flags])

    nqk, nv, hw = RET_HEADS * RET_DK, RET_HEADS * RET_DV, HG_HEADS * HG_DK
    ret_scale = halves((2 * nqk + nv, False), (nv, True))
    hg_scale = halves((hw, True), (2 * hw, False), (hw, True))
    lru_scale = halves((LRU_W, False), (LRU_W + N_BRANCH * D_MODEL, True))
    w = dict(
        ffn1_norm=ffn1_norm, mix_norm=mix_norm, merge_bias=0.5 * merge_bias, hgrn_lb_logits=hgrn_lb_logits,
        hgrn_norm=hgrn_norm, conv_w=conv_w, conv_b=conv_b, lru_w_a=c16(0.5 * lru_w_a), lru_b_a=0.5 * lru_b_a,
        lru_w_x=c16(0.5 * lru_w_x), lru_b_x=0.5 * lru_b_x, lru_lambda=lru_lambda, ffn2_norm=ffn2_norm,
        final_norm=final_norm,
    )
    raw = dict(ffn1_w_gate=ffn1_w_gate, ffn1_w_up=ffn1_w_up, ffn1_w_down=ffn1_w_down, w_in=w_in,
               w_ret_o=w_ret_o, w_hgrn_o=w_hgrn_o, w_lru_o=w_lru_o, w_mix_out=w_mix_out,
               ffn2_w_gate=ffn2_w_gate, ffn2_w_up=ffn2_w_up, ffn2_w_down=ffn2_w_down,
               ret_scale=ret_scale, hg_scale=hg_scale, lru_scale=lru_scale)
    big = [{} for _ in range(w_in.shape[0])]
    big[0].update(ffn1_wg=c16(0.5 * ffn1_w_gate[0]), ffn1_wu=c16(ffn1_w_up[0]), ffn1_wd=c16(ffn1_w_down[0]))
    bp, tp, _ = x_prompt.shape
    bs, ts, _ = x_sample.shape
    y_p, st_p = _trunk(x_prompt.reshape(bp * tp, D_MODEL), jnp.arange(tp), w, big, raw, None, bp, tp)
    y_s, st_s = _trunk(x_sample.reshape(bs * ts, D_MODEL), PAST_LEN + jnp.arange(ts), w, big, None,
                       (state_ret, state_hgrn, state_lru, state_conv), bs, ts)
    return (y_p.reshape(bp, tp, D_MODEL), y_s.reshape(bs, ts, D_MODEL),
            st_p[0], st_s[0], st_p[1], st_s[1], st_p[2], st_s[2], st_p[3], st_s[3])
```
